```python
import math
import jax
import jax.numpy as jnp
from jax import lax
import numpy as np

D_MODEL = 1024
BATCH = 32
SEQ = 2048
DEPTH = 4
DEC_BATCH = 4
DEC_SEQ = 4096
PAST_LEN = 128

N_MIXERS = 4
N_HY = (DEPTH + 3) // 4
N_GDN = (DEPTH + 2) // 4
N_SWA = (DEPTH + 1) // 4
N_MLA = DEPTH // 4
EPS = 1e-6
D_FF = ((8 * D_MODEL + 767) // 768) * 256

HY_SHORT = 3
HY_EMB = 33
HY_BANDS = (HY_EMB - 1) // 2
HY_FILT = 64
HY_TARGET = 1e-2
HY_FAST = 0.3
HY_SLOW = 1.5

GDN_HK = 8
GDN_HV = 16
GDN_DK = 128
GDN_DV = 128
GDN_CONV = 3
GDN_CHUNK = 64
GDN_QKV = 2 * GDN_HK * GDN_DK + GDN_HV * GDN_DV
GDN_IN = GDN_QKV + GDN_HV * GDN_DV

SWA_HQ = 16
SWA_HKV = 4
SWA_DH = 64
SWA_WINDOW = 128
SWA_BLOCK = 128

MLA_H = 16
MLA_NOPE = 64
MLA_ROPE = 32
MLA_DV = 64
MLA_QRANK = 256
MLA_KVRANK = 256
MLA_QBLOCK = 128
ROPE_THETA = 10000.0

kernel_name = 'hybrid_bidir_encoder_hyena_gdn_swa_mla'


def rmsnorm(x, w):
    xf = x.astype(jnp.float32)
    y = xf * lax.rsqrt(jnp.mean(xf * xf, axis=-1, keepdims=True) + EPS)
    return (y * w.astype(jnp.float32)).astype(x.dtype)


def centred_dwconv(x, w, b):
    K = w.shape[0]
    pad = K // 2
    L = x.shape[1]
    xp = jnp.pad(x, ((0, 0), (pad, pad), (0, 0)))
    y = b + w[0] * xp[:, 0:L]
    for j in range(1, K):
        y = y + w[j] * xp[:, j:j + L]
    return y


def swiglu(h, w_gu, w_down):
    gate, up = jnp.split(h @ w_gu, 2, axis=-1)
    return (jax.nn.silu(gate) * up) @ w_down


def hyena_filter(L, fw1, fb1, ff1, fw2, fb2, ff2, fw3):
    f32 = jnp.float32
    pos = jnp.arange(L, dtype=f32)[:, None]
    t = pos / max(L - 1, 1)
    freqs = jnp.linspace(1e-4, HY_BANDS - 1, HY_BANDS, dtype=f32)[None, :]
    ang = freqs * (2.0 * math.pi / L) * pos
    feats = jnp.concatenate([t, jnp.cos(ang), -jnp.sin(ang)], axis=-1)
    z = jnp.sin(ff1.astype(f32) * (feats @ fw1.astype(f32) + fb1.astype(f32)))
    z = jnp.sin(ff2.astype(f32) * (z @ fw2.astype(f32) + fb2.astype(f32)))
    z = z @ fw3.astype(f32)
    rates = jnp.abs(jnp.linspace(math.log(HY_TARGET) / HY_SLOW, math.log(HY_TARGET) / HY_FAST, D_MODEL, dtype=f32))
    window = jnp.exp(-t * rates)
    h_fwd = z[:, :D_MODEL] * window
    h_bwd = z[:, D_MODEL:] * window
    return jnp.concatenate([h_fwd, jnp.zeros((1, D_MODEL), f32), h_bwd[:0:-1]], axis=0)


def hyena_mixer(h, w_in, conv_w, conv_b, fw1, fb1, ff1, fw2, fb2, ff2, fw3, skip, w_out):
    B, L, _ = h.shape
    u = centred_dwconv(h @ w_in, conv_w, conv_b)
    x0, x1, v = jnp.split(u, 3, axis=-1)
    v = (v * x1).astype(jnp.float32)
    kern_f = jnp.fft.rfft(hyena_filter(L, fw1, fb1, ff1, fw2, fb2, ff2, fw3), axis=0)
    v_f = jnp.fft.rfft(v, n=2 * L, axis=1)
    y = jnp.fft.irfft(v_f * kern_f, n=2 * L, axis=1)[:, :L] + v * skip.astype(jnp.float32)
    return (y.astype(h.dtype) * x0) @ w_out


def l2norm(x):
    xf = x.astype(jnp.float32)
    return xf * lax.rsqrt(jnp.sum(xf * xf, axis=-1, keepdims=True) + EPS)


def chunk_gated_delta(q, k, v, g, beta):
    f32 = jnp.float32
    B, L, H, DK = q.shape
    DV = v.shape[-1]
    C = GDN_CHUNK
    N = L // C

    def blocks(t):
        t = t.astype(f32).reshape((B, N, C, H) + t.shape[3:])
        return jnp.swapaxes(t, 2, 3)

    q, k, v, g, beta = blocks(q), blocks(k), blocks(v), blocks(g), blocks(beta)
    gc = jnp.cumsum(g, axis=-1)
    incl = jnp.tril(jnp.ones((C, C), dtype=bool))
    strict = jnp.tril(jnp.ones((C, C), dtype=bool), -1)
    decay = jnp.exp(jnp.where(incl, gc[..., :, None] - gc[..., None, :], -jnp.inf))
    k_beta = k * beta[..., None]
    a_mat = jnp.where(strict, jnp.einsum('bnhik,bnhjk->bnhij', k_beta, k) * decay, 0.0)
    rhs = jnp.concatenate([v * beta[..., None], k_beta * jnp.exp(gc)[..., None]], axis=-1)
    sol = lax.linalg.triangular_solve(a_mat, rhs, left_side=True, lower=True, unit_diagonal=True)
    u, w = sol[..., :DV], sol[..., DV:]
    intra = jnp.einsum('bnhik,bnhjk->bnhij', q, k) * decay
    q_dec = q * jnp.exp(gc)[..., None]
    k_dec = k * jnp.exp(gc[..., -1:] - gc)[..., None]
    g_end = jnp.exp(gc[..., -1])

    def step(S, xs):
        qd, kd, u_c, w_c, a_c, ge = xs
        v_new = u_c - jnp.einsum('bhck,bhkv->bhcv', w_c, S)
        o = jnp.einsum('bhck,bhkv->bhcv', qd, S) + jnp.einsum('bhij,bhjv->bhiv', a_c, v_new)
        S = S * ge[..., None, None] + jnp.einsum('bhck,bhcv->bhkv', kd, v_new)
        return S, o

    xs = tuple(jnp.moveaxis(t, 1, 0) for t in (q_dec, k_dec, u, w, intra, g_end))
    _, o = lax.scan(step, jnp.zeros((B, H, DK, DV), f32), xs)
    return jnp.swapaxes(jnp.moveaxis(o, 0, 1), 2, 3).reshape(B, L, H, DV)


def gdn_mixer(h, w_in, conv_w, conv_b, w_ab, a_log, dt_bias, norm_w, w_out):
    B, L, _ = h.shape
    proj = h @ w_in
    qkv = jax.nn.silu(centred_dwconv(proj[..., :GDN_QKV], conv_w, conv_b))
    z = proj[..., GDN_QKV:].reshape(B, L, GDN_HV, GDN_DV)
    nq = GDN_HK * GDN_DK
    q = l2norm(qkv[..., :nq].reshape(B, L, GDN_HK, GDN_DK)) * (GDN_DK ** -0.5)
    k = l2norm(qkv[..., nq:2 * nq].reshape(B, L, GDN_HK, GDN_DK))
    v = qkv[..., 2 * nq:].reshape(B, L, GDN_HV, GDN_DV)
    rep = GDN_HV // GDN_HK
    q = jnp.repeat(q, rep, axis=2)
    k = jnp.repeat(k, rep, axis=2)
    ab = (h @ w_ab).astype(jnp.float32)
    a = ab[..., :2 * GDN_HV].reshape(B, L, 2, GDN_HV)
    bb = ab[..., 2 * GDN_HV:].reshape(B, L, 2, GDN_HV)
    g = -jnp.exp(a_log.astype(jnp.float32)) * jax.nn.softplus(a + dt_bias.astype(jnp.float32))
    beta = jax.nn.sigmoid(bb)
    o_fwd = chunk_gated_delta(q, k, v, g[:, :, 0], beta[:, :, 0])
    flip = lambda t: jnp.flip(t, axis=1)
    o_bwd = flip(chunk_gated_delta(flip(q), flip(k), flip(v), flip(g[:, :, 1]), flip(beta[:, :, 1])))
    o = (o_fwd + o_bwd).astype(h.dtype)
    o = rmsnorm(o, norm_w) * jax.nn.silu(z)
    return o.reshape(B, L, GDN_HV * GDN_DV) @ w_out


def swa_mixer(h, w_qkv, sink, w_out):
    f32 = jnp.float32
    B, L, _ = h.shape
    W = SWA_BLOCK
    NB = L // W
    G = SWA_HQ // SWA_HKV
    qkv = h @ w_qkv
    nq = SWA_HQ * SWA_DH
    nk = SWA_HKV * SWA_DH
    q = qkv[..., :nq].reshape(B, NB, W, SWA_HKV, G, SWA_DH)
    k = qkv[..., nq:nq + nk].reshape(B, L, SWA_HKV, SWA_DH)
    v = qkv[..., nq + nk:].reshape(B, L, SWA_HKV, SWA_DH)

    def band(t):
        tp = jnp.pad(t, ((0, 0), (W, W), (0, 0), (0, 0))).reshape(B, NB + 2, W, SWA_HKV, SWA_DH)
        return jnp.concatenate([tp[:, :-2], tp[:, 1:-1], tp[:, 2:]], axis=2)

    kb, vb = band(k), band(v)
    rel = jnp.arange(3 * W)[None, :] - W - jnp.arange(W)[:, None]
    key_pos = (jnp.arange(NB)[:, None] - 1) * W + jnp.arange(3 * W)[None, :]
    valid = (jnp.abs(rel) <= SWA_WINDOW)[None] & ((key_pos >= 0) & (key_pos < L))[:, None, :]
    slopes = (2.0 ** (-8.0 * jnp.arange(1, SWA_HQ + 1, dtype=f32) / SWA_HQ)).reshape(SWA_HKV, G)
    dist = jnp.abs(rel).astype(f32)
    s = jnp.einsum('bnqhgd,bnkhd->bnhgqk', q, kb).astype(f32) * (SWA_DH ** -0.5)
    s = s - slopes[:, :, None, None] * dist
    s = jnp.where(valid[:, None, None], s, -jnp.inf)
    sink_l = sink.astype(f32).reshape(SWA_HKV, G)[:, :, None, None]
    m = jnp.maximum(jnp.max(s, axis=-1, keepdims=True), sink_l)
    e = jnp.exp(s - m)
    p = e / (jnp.sum(e, axis=-1, keepdims=True) + jnp.exp(sink_l - m))
    o = jnp.einsum('bnhgqk,bnkhd->bnqhgd', p.astype(vb.dtype), vb).reshape(B, L, nq)
    return o @ w_out


def rope_tables(L):
    inv = ROPE_THETA ** (-jnp.arange(0, MLA_ROPE, 2, dtype=jnp.float32) / MLA_ROPE)
    ang = jnp.arange(L, dtype=jnp.float32)[:, None] * inv[None, :]
    return jnp.cos(ang), jnp.sin(ang)


def apply_rope(x, cos, sin):
    x1, x2 = jnp.split(x, 2, axis=-1)
    cos = cos.astype(x.dtype)
    sin = sin.astype(x.dtype)
    return jnp.concatenate([x1 * cos - x2 * sin, x2 * cos + x1 * sin], axis=-1)


def mla_mixer(h, w_down, q_norm_w, w_uq, kv_norm_w, w_ukv, w_out):
    B, L, _ = h.shape
    d = h @ w_down
    cq = rmsnorm(d[..., :MLA_QRANK], q_norm_w)
    ckv = rmsnorm(d[..., MLA_QRANK:MLA_QRANK + MLA_KVRANK], kv_norm_w)
    k_rope = d[..., MLA_QRANK + MLA_KVRANK:]
    q = (cq @ w_uq).reshape(B, L, MLA_H, MLA_NOPE + MLA_ROPE)
    kv = (ckv @ w_ukv).reshape(B, L, MLA_H, MLA_NOPE + MLA_DV)
    k_nope, v = kv[..., :MLA_NOPE], kv[..., MLA_NOPE:]
    cos, sin = rope_tables(L)
    q_nope = q[..., :MLA_NOPE]
    q_rope = apply_rope(q[..., MLA_NOPE:], cos[:, None, :], sin[:, None, :])
    k_rope = apply_rope(k_rope, cos, sin)
    scale = (MLA_NOPE + MLA_ROPE) ** -0.5
    NB = L // MLA_QBLOCK
    qn = jnp.moveaxis(q_nope.reshape(B, NB, MLA_QBLOCK, MLA_H, MLA_NOPE), 1, 0)
    qr = jnp.moveaxis(q_rope.reshape(B, NB, MLA_QBLOCK, MLA_H, MLA_ROPE), 1, 0)

    def block(args):
        qn_b, qr_b = args
        s = (jnp.einsum('bqhd,bkhd->bhqk', qn_b, k_nope) + jnp.einsum('bqhd,bkd->bhqk', qr_b, k_rope)).astype(jnp.float32) * scale
        p = jax.nn.softmax(s, axis=-1)
        return jnp.einsum('bhqk,bkhd->bqhd', p.astype(v.dtype), v)

    o = lax.map(block, (qn, qr))
    o = jnp.moveaxis(o, 0, 1).reshape(B, L, MLA_H * MLA_DV)
    return o @ w_out


def encoder_trunk(x, c, ada_w, ada_b, norm_w, hy, gdn, swa, mla, ffn_w_gu, ffn_w_down, final_norm_w):
    c_act = jax.nn.silu(c)
    for i in range(DEPTH):
        kind, j = i % N_MIXERS, i // N_MIXERS
        mod = (c_act @ ada_w[i] + ada_b[i])[:, None, :]
        sh1, sc1, g1, sh2, sc2, g2 = jnp.split(mod, 6, axis=-1)
        h = rmsnorm(x, norm_w[i, 0]) * (1.0 + sc1) + sh1
        if kind == 0:
            out = hyena_mixer(h, *[p[j] for p in hy])
        elif kind == 1:
            out = gdn_mixer(h, *[p[j] for p in gdn])
        elif kind == 2:
            out = swa_mixer(h, *[p[j] for p in swa])
        else:
            out = mla_mixer(h, *[p[j] for p in mla])
        x = x + g1 * out
        h = rmsnorm(x, norm_w[i, 1]) * (1.0 + sc2) + sh2
        x = x + g2 * swiglu(h, ffn_w_gu[i], ffn_w_down[i])
    return rmsnorm(x, final_norm_w)


def setup_inputs(seed: int = 0) -> dict:
    key = jax.random.key(seed)
    keys = iter(jax.random.split(key, 64))
    f32 = jnp.float32

    def nrm(shape, scale):
        return jax.random.normal(next(keys), shape, f32) * scale

    def gain(shape):
        return 1.0 + nrm(shape, 0.01)

    D = D_MODEL
    dt = jnp.exp(jax.random.uniform(next(keys), (N_GDN, 2, GDN_HV), f32, math.log(1e-3), math.log(1e-1)))
    return {
        'x_prompt': nrm((BATCH, SEQ, D), 1.0),
        'x_sample': nrm((DEC_BATCH, DEC_SEQ, D), 1.0),
        'c_prompt': nrm((BATCH, D), 1.0),
        'c_sample': nrm((DEC_BATCH, D), 1.0),
        'ada_w': nrm((DEPTH, D, 6 * D), 0.5 * D ** -0.5),
        'ada_b': nrm((DEPTH, 6 * D), 0.01),
        'norm_w': gain((DEPTH, 2, D)),
        'hy_w_in': nrm((N_HY, D, 3 * D), D ** -0.5),
        'hy_conv_w': nrm((N_HY, HY_SHORT, 3 * D), HY_SHORT ** -0.5),
        'hy_conv_b': nrm((N_HY, 3 * D), 0.01),
        'hy_filt_w1': nrm((N_HY, HY_EMB, HY_FILT), HY_EMB ** -0.5),
        'hy_filt_b1': nrm((N_HY, HY_FILT), 0.01),
        'hy_filt_freq1': gain((N_HY, HY_FILT)),
        'hy_filt_w2': nrm((N_HY, HY_FILT, HY_FILT), HY_FILT ** -0.5),
        'hy_filt_b2': nrm((N_HY, HY_FILT), 0.01),
        'hy_filt_freq2': gain((N_HY, HY_FILT)),
        'hy_filt_w3': nrm((N_HY, HY_FILT, 2 * D), 0.008),
        'hy_skip': nrm((N_HY, D), 1.0),
        'hy_w_out': nrm((N_HY, D, D), D ** -0.5),
        'gdn_w_in': nrm((N_GDN, D, GDN_IN), D ** -0.5),
        'gdn_conv_w': nrm((N_GDN, GDN_CONV, GDN_QKV), GDN_CONV ** -0.5),
        'gdn_conv_b': nrm((N_GDN, GDN_QKV), 0.01),
        'gdn_w_ab': nrm((N_GDN, D, 4 * GDN_HV), D ** -0.5),
        'gdn_a_log': jnp.log(jax.random.uniform(next(keys), (N_GDN, 2, GDN_HV), f32, 1.0, 16.0)),
        'gdn_dt_bias': dt + jnp.log(-jnp.expm1(-dt)),
        'gdn_norm_w': gain((N_GDN, GDN_DV)),
        'gdn_w_out': nrm((N_GDN, GDN_HV * GDN_DV, D), (GDN_HV * GDN_DV) ** -0.5),
        'swa_w_qkv': nrm((N_SWA, D, (SWA_HQ + 2 * SWA_HKV) * SWA_DH), D ** -0.5),
        'swa_sink': nrm((N_SWA, SWA_HQ), 1.0),
        'swa_w_out': nrm((N_SWA, SWA_HQ * SWA_DH, D), (SWA_HQ * SWA_DH) ** -0.5),
        'mla_w_down': nrm((N_MLA, D, MLA_QRANK + MLA_KVRANK + MLA_ROPE), D ** -0.5),
        'mla_q_norm_w': gain((N_MLA, MLA_QRANK)),
        'mla_w_uq': nrm((N_MLA, MLA_QRANK, MLA_H * (MLA_NOPE + MLA_ROPE)), MLA_QRANK ** -0.5),
        'mla_kv_norm_w': gain((N_MLA, MLA_KVRANK)),
        'mla_w_ukv': nrm((N_MLA, MLA_KVRANK, MLA_H * (MLA_NOPE + MLA_DV)), MLA_KVRANK ** -0.5),
        'mla_w_out': nrm((N_MLA, MLA_H * MLA_DV, D), (MLA_H * MLA_DV) ** -0.5),
        'ffn_w_gu': nrm((DEPTH, D, 2 * D_FF), D ** -0.5),
        'ffn_w_down': nrm((DEPTH, D_FF, D), D_FF ** -0.5),
        'final_norm_w': gain((D,)),
    }


def reference(x_prompt, x_sample, c_prompt, c_sample, ada_w, ada_b, norm_w,
              hy_w_in, hy_conv_w, hy_conv_b, hy_filt_w1, hy_filt_b1, hy_filt_freq1,
              hy_filt_w2, hy_filt_b2, hy_filt_freq2, hy_filt_w3, hy_skip, hy_w_out,
              gdn_w_in, gdn_conv_w, gdn_conv_b, gdn_w_ab, gdn_a_log, gdn_dt_bias, gdn_norm_w, gdn_w_out,
              swa_w_qkv, swa_sink, swa_w_out,
              mla_w_down, mla_q_norm_w, mla_w_uq, mla_kv_norm_w, mla_w_ukv, mla_w_out,
              ffn_w_gu, ffn_w_down, final_norm_w):
    hy = (hy_w_in, hy_conv_w, hy_conv_b, hy_filt_w1, hy_filt_b1, hy_filt_freq1,
          hy_filt_w2, hy_filt_b2, hy_filt_freq2, hy_filt_w3, hy_skip, hy_w_out)
    gdn = (gdn_w_in, gdn_conv_w, gdn_conv_b, gdn_w_ab, gdn_a_log, gdn_dt_bias, gdn_norm_w, gdn_w_out)
    swa = (swa_w_qkv, swa_sink, swa_w_out)
    mla = (mla_w_down, mla_q_norm_w, mla_w_uq, mla_kv_norm_w, mla_w_ukv, mla_w_out)
    y_prompt = encoder_trunk(x_prompt, c_prompt, ada_w, ada_b, norm_w, hy, gdn, swa, mla,
                             ffn_w_gu, ffn_w_down, final_norm_w)
    y_sample = encoder_trunk(x_sample, c_sample, ada_w, ada_b, norm_w, hy, gdn, swa, mla,
                             ffn_w_gu, ffn_w_down, final_norm_w)
    return (y_prompt, y_sample)
```

```python
import functools
import math

import jax
import jax.numpy as jnp
from jax import lax
from jax.experimental import pallas as pl
from jax.experimental.pallas import tpu as pltpu

F32 = jnp.float32
BF16 = jnp.bfloat16
HI = lax.Precision.HIGHEST

D = 1024
DEPTH = 4
EPS = 1e-6
D_FF = 2816

HY_BANDS = 16
HY_FILT = 64
HY_TARGET = 1e-2
HY_FAST = 0.3
HY_SLOW = 1.5

GDN_HK = 8
GDN_HV = 16
GDN_DK = 128
GDN_DV = 128
GDN_CHUNK = 64
GDN_QKV = 2 * GDN_HK * GDN_DK + GDN_HV * GDN_DV
GDN_IN = GDN_QKV + GDN_HV * GDN_DV

SWA_HQ = 16
SWA_HKV = 4
SWA_G = SWA_HQ // SWA_HKV
SWA_DH = 64
SWA_WINDOW = 128
SWA_BLOCK = 128

MLA_H = 16
MLA_NOPE = 64
MLA_ROPE = 32
MLA_DV = 64
MLA_QRANK = 256
MLA_KVRANK = 256
MLA_HP = 128
ROPE_THETA = 10000.0

LANE = 128
MIB = 2 ** 20
NEG = -1e30


def _cparams(sem, vmem_mb):
    return pltpu.CompilerParams(dimension_semantics=sem, vmem_limit_bytes=vmem_mb * MIB)


def _resident(shape):
    nd = len(shape)
    return pl.BlockSpec(shape, lambda *_: (0,) * nd, pipeline_mode=pl.Buffered(1))


def _bdot(a, b):
    return jnp.dot(a, b, preferred_element_type=F32)


def _dot_nt(a, b, precision=None):
    return lax.dot_general(a, b, (((1,), (1,)), ((), ())), preferred_element_type=F32, precision=precision)


def _dot_tn(a, b, precision=None):
    return lax.dot_general(a, b, (((0,), (0,)), ((), ())), preferred_element_type=F32, precision=precision)


def _rms(x):
    return x * lax.rsqrt(jnp.mean(x * x, axis=-1, keepdims=True) + EPS)


def _norm_mod(x, nw, mod_ref, sh_row, sc_row):
    return _rms(x) * nw * (1.0 + mod_ref[0, sc_row:sc_row + 1, :]) + mod_ref[0, sh_row:sh_row + 1, :]


def _ada_kernel(c_ref, w_ref, b_ref, o_ref):
    c = c_ref[...]
    ca = c * jax.nn.sigmoid(c)
    o_ref[0] = jnp.dot(ca, w_ref[0], preferred_element_type=F32, precision=HI) + b_ref[0]


def ada_modulation(c, ada_w, ada_b):
    B = c.shape[0]
    tn = 1024
    out = pl.pallas_call(
        _ada_kernel,
        grid=(DEPTH, 6 * D // tn),
        in_specs=[
            pl.BlockSpec((B, D), lambda i, j: (0, 0)),
            pl.BlockSpec((1, D, tn), lambda i, j: (i, 0, j)),
            pl.BlockSpec((1, 1, tn), lambda i, j: (i, 0, j)),
        ],
        out_specs=pl.BlockSpec((1, B, tn), lambda i, j: (i, 0, j)),
        out_shape=jax.ShapeDtypeStruct((DEPTH, B, 6 * D), F32),
        compiler_params=_cparams(("parallel", "parallel"), 32),
        name="ada_modulation",
    )(c, ada_w, ada_b.reshape(DEPTH, 1, 6 * D))
    return out.reshape(DEPTH, B, 6, D)


def _nmm_kernel(x_ref, mod_ref, nw_ref, w_ref, o_ref, h_scr, *, sh_row, sc_row, tn):
    h_scr[...] = _norm_mod(x_ref[0], nw_ref[...], mod_ref, sh_row, sc_row).astype(BF16)
    n = w_ref.shape[1]
    for c0 in range(0, n, tn):
        c1 = min(c0 + tn, n)
        o_ref[0, :, c0:c1] = _bdot(h_scr[...], w_ref[:, c0:c1])


def norm_mod_matmul(x, mod, nw, w, *, sh_row=0, sc_row=1, tm=512, tn=512):
    B, L, _ = x.shape
    N = w.shape[1]
    return pl.pallas_call(
        functools.partial(_nmm_kernel, sh_row=sh_row, sc_row=sc_row, tn=tn),
        grid=(B, L // tm),
        in_specs=[
            pl.BlockSpec((1, tm, D), lambda b, i: (b, i, 0)),
            pl.BlockSpec((1, 6, D), lambda b, i: (b, 0, 0)),
            _resident((1, D)),
            _resident((D, N)),
        ],
        out_specs=pl.BlockSpec((1, tm, N), lambda b, i: (b, i, 0)),
        out_shape=jax.ShapeDtypeStruct((B, L, N), F32),
        scratch_shapes=[pltpu.VMEM((tm, D), BF16)],
        compiler_params=_cparams(("parallel", "parallel"), 48),
        name="norm_mod_matmul",
    )(x, mod, nw.reshape(1, D), w)


def _mmres_kernel(a_ref, x_ref, mod_ref, w_ref, o_ref, *, g_row):
    y = _bdot(a_ref[0].astype(BF16), w_ref[...])
    o_ref[0] = x_ref[0] + mod_ref[0, g_row:g_row + 1, :] * y


def matmul_gated_residual(a, x, mod, w, *, g_row=2, tm=512):
    B, L, K = a.shape
    return pl.pallas_call(
        functools.partial(_mmres_kernel, g_row=g_row),
        grid=(B, L // tm),
        in_specs=[
            pl.BlockSpec((1, tm, K), lambda b, i: (b, i, 0)),
            pl.BlockSpec((1, tm, D), lambda b, i: (b, i, 0)),
            pl.BlockSpec((1, 6, D), lambda b, i: (b, 0, 0)),
            _resident((K, D)),
        ],
        out_specs=pl.BlockSpec((1, tm, D), lambda b, i: (b, i, 0)),
        out_shape=jax.ShapeDtypeStruct((B, L, D), F32),
        compiler_params=_cparams(("parallel", "parallel"), 40),
        name="matmul_gated_residual",
    )(a, x, mod, w)


def _ffn_kernel(x_ref, mod_ref, nw_ref, wgu_ref, wd_ref, o_ref, h_scr, act_scr, *, tf):
    x = x_ref[0]
    h_scr[...] = _norm_mod(x, nw_ref[...], mod_ref, 3, 4).astype(BF16)
    for c0 in range(0, D_FF, tf):
        gate = _bdot(h_scr[...], wgu_ref[:, c0:c0 + tf])
        up = _bdot(h_scr[...], wgu_ref[:, D_FF + c0:D_FF + c0 + tf])
        act_scr[:, c0:c0 + tf] = (gate * jax.nn.sigmoid(gate) * up).astype(BF16)
    o_ref[0] = x + mod_ref[0, 5:6, :] * _bdot(act_scr[...], wd_ref[...])


def ffn_block(x, mod, nw, w_gu, w_down, *, tm=512, tf=256):
    B, L, _ = x.shape
    return pl.pallas_call(
        functools.partial(_ffn_kernel, tf=tf),
        grid=(B, L // tm),
        in_specs=[
            pl.BlockSpec((1, tm, D), lambda b, i: (b, i, 0)),
            pl.BlockSpec((1, 6, D), lambda b, i: (b, 0, 0)),
            _resident((1, D)),
            _resident((D, 2 * D_FF)),
            _resident((D_FF, D)),
        ],
        out_specs=pl.BlockSpec((1, tm, D), lambda b, i: (b, i, 0)),
        out_shape=jax.ShapeDtypeStruct((B, L, D), F32),
        scratch_shapes=[pltpu.VMEM((tm, D), BF16), pltpu.VMEM((tm, D_FF), BF16)],
        compiler_params=_cparams(("parallel", "parallel"), 48),
        name="ffn_block",
    )(x, mod, nw.reshape(1, D), w_gu, w_down)


def _final_norm_kernel(x_ref, nw_ref, o_ref):
    o_ref[0] = _rms(x_ref[0]) * nw_ref[...]


def final_norm(x, nw, *, tm=1024):
    B, L, _ = x.shape
    tm = min(tm, L)
    return pl.pallas_call(
        _final_norm_kernel,
        grid=(B, L // tm),
        in_specs=[pl.BlockSpec((1, tm, D), lambda b, i: (b, i, 0)), _resident((1, D))],
        out_specs=pl.BlockSpec((1, tm, D), lambda b, i: (b, i, 0)),
        out_shape=jax.ShapeDtypeStruct((B, L, D), F32),
        compiler_params=_cparams(("parallel", "parallel"), 32),
        name="final_norm",
    )(x, nw.reshape(1, D))


def _hy_filter_kernel(freq_ref, w1t_ref, w1c_ref, w1s_ref, b1_ref, f1_ref, w2_ref, b2_ref, f2_ref, w3_ref,
                      rate_ref, hs_ref, hd_ref, *, L, tl):
    pos = (pl.program_id(0) * tl + lax.broadcasted_iota(jnp.int32, (tl, 1), 0)).astype(F32)
    t = pos / max(L - 1, 1)
    ang = freq_ref[...] * (2.0 * math.pi / L) * pos
    z = (t * w1t_ref[...] + jnp.dot(jnp.cos(ang), w1c_ref[...], preferred_element_type=F32, precision=HI)
         - jnp.dot(jnp.sin(ang), w1s_ref[...], preferred_element_type=F32, precision=HI) + b1_ref[...])
    z = jnp.sin(f1_ref[...] * z)
    z = jnp.sin(f2_ref[...] * (jnp.dot(z, w2_ref[...], preferred_element_type=F32, precision=HI) + b2_ref[...]))
    z = jnp.dot(z, w3_ref[...], preferred_element_type=F32, precision=HI)
    window = jnp.exp(-t * rate_ref[...])
    hf = z[:, :D] * window
    hb = jnp.where(pos == 0.0, 0.0, z[:, D:] * window)
    hs_ref[...] = hf + hb
    hd_ref[...] = hb - hf


def hyena_filter_taps(L, fw1, fb1, ff1, fw2, fb2, ff2, fw3):
    tl = 256
    freqs = jnp.linspace(1e-4, HY_BANDS - 1, HY_BANDS, dtype=F32).reshape(1, HY_BANDS)
    rates = jnp.abs(jnp.linspace(math.log(HY_TARGET) / HY_SLOW, math.log(HY_TARGET) / HY_FAST, D, dtype=F32))
    small = [freqs, fw1[0:1], fw1[1:1 + HY_BANDS], fw1[1 + HY_BANDS:], fb1.reshape(1, -1), ff1.reshape(1, -1),
             fw2, fb2.reshape(1, -1), ff2.reshape(1, -1), fw3, rates.reshape(1, D)]
    return pl.pallas_call(
        functools.partial(_hy_filter_kernel, L=L, tl=tl),
        grid=(L // tl,),
        in_specs=[_resident(a.shape) for a in small],
        out_specs=[pl.BlockSpec((tl, D), lambda i: (i, 0))] * 2,
        out_shape=[jax.ShapeDtypeStruct((L, D), F32)] * 2,
        compiler_params=_cparams(("parallel",), 32),
        name="hyena_filter_taps",
    )(*small)


def _hy_filter_dft_kernel(hs_ref, hd_ref, cf_ref, sf_ref, kre_ref, kim_ref, knyq_ref, *, L, fk):
    k = pl.program_id(1)
    hs = hs_ref[...]
    inv_n = 1.0 / (2 * L)
    row = k * fk + lax.broadcasted_iota(jnp.int32, (fk, 1), 0)
    wk = jnp.where(row == 0, inv_n, 2.0 * inv_n)
    kre_ref[...] = wk * _bdot(cf_ref[...], hs.astype(BF16))
    kim_ref[...] = wk * _bdot(sf_ref[...], hd_ref[...].astype(BF16))

    @pl.when(k == 0)
    def _():
        t = lax.broadcasted_iota(jnp.int32, hs.shape, 0)
        sgn = (1 - 2 * (t & 1)).astype(F32)
        knyq_ref[...] = inv_n * jnp.sum(hs * sgn, axis=0, keepdims=True)


def hyena_filter_dft(hs, hd, cmat, smat, *, ct=256, fk=256):
    L = hs.shape[0]
    return pl.pallas_call(
        functools.partial(_hy_filter_dft_kernel, L=L, fk=fk),
        grid=(D // ct, L // fk),
        in_specs=[
            pl.BlockSpec((L, ct), lambda j, k: (0, j)),
            pl.BlockSpec((L, ct), lambda j, k: (0, j)),
            pl.BlockSpec((fk, L), lambda j, k: (k, 0)),
            pl.BlockSpec((fk, L), lambda j, k: (k, 0)),
        ],
        out_specs=[
            pl.BlockSpec((fk, ct), lambda j, k: (k, j)),
            pl.BlockSpec((fk, ct), lambda j, k: (k, j)),
            pl.BlockSpec((1, ct), lambda j, k: (0, j)),
        ],
        out_shape=[jax.ShapeDtypeStruct((L, D), F32), jax.ShapeDtypeStruct((L, D), F32),
                   jax.ShapeDtypeStruct((1, D), F32)],
        compiler_params=_cparams(("parallel", "arbitrary"), 48),
        name="hyena_filter_dft",
    )(hs, hd, cmat, smat)


def _hy_pre_kernel(u0_ref, u1_ref, u2_ref, w0_ref, w1_ref, w2_ref, b0_ref, b1_ref, b2_ref, x0_ref, vg_ref, vb_ref):
    L = u0_ref.shape[1]
    t = lax.broadcasted_iota(jnp.int32, (L, 1), 0)
    first, last = t == 0, t == L - 1

    def dwconv(u_ref, w_ref, b_ref):
        u = u_ref[0]
        prev = jnp.where(first, 0.0, pltpu.roll(u, 1, 0))
        nxt = jnp.where(last, 0.0, pltpu.roll(u, L - 1, 0))
        return b_ref[...] + w_ref[0:1, :] * prev + w_ref[1:2, :] * u + w_ref[2:3, :] * nxt

    x0_ref[0] = dwconv(u0_ref, w0_ref, b0_ref)
    vg = dwconv(u2_ref, w2_ref, b2_ref) * dwconv(u1_ref, w1_ref, b1_ref)
    vg_ref[0] = vg
    vb_ref[0] = vg.astype(BF16)


def hyena_pre(u, conv_w, conv_b, *, ct=128):
    B, L, _ = u.shape
    nj = D // ct
    conv_b = conv_b.reshape(1, 3 * D)
    ublk = [pl.BlockSpec((1, L, ct), lambda b, j, s=s: (b, 0, s * nj + j)) for s in range(3)]
    wblk = [pl.BlockSpec((3, ct), lambda b, j, s=s: (0, s * nj + j)) for s in range(3)]
    bblk = [pl.BlockSpec((1, ct), lambda b, j, s=s: (0, s * nj + j)) for s in range(3)]
    oblk = pl.BlockSpec((1, L, ct), lambda b, j: (b, 0, j))
    return pl.pallas_call(
        _hy_pre_kernel,
        grid=(B, nj),
        in_specs=ublk + wblk + bblk,
        out_specs=[oblk, oblk, oblk],
        out_shape=[jax.ShapeDtypeStruct((B, L, D), F32), jax.ShapeDtypeStruct((B, L, D), F32),
                   jax.ShapeDtypeStruct((B, L, D), BF16)],
        compiler_params=_cparams(("parallel", "parallel"), 48),
        name="hyena_pre",
    )(u, u, u, conv_w, conv_w, conv_w, conv_b, conv_b, conv_b)


def _hy_conv_kernel(vb_ref, cf_ref, sf_ref, ci_ref, si_ref, kre_ref, kim_ref, knyq_ref, o_ref):
    k = pl.program_id(2)
    vb = vb_ref[0]

    @pl.when(k == 0)
    def _():
        t = lax.broadcasted_iota(jnp.int32, vb.shape, 0)
        sgn = (1 - 2 * (t & 1)).astype(F32)
        vnyq = jnp.sum(vb.astype(F32) * sgn, axis=0, keepdims=True)
        o_ref[0] = sgn * (vnyq * knyq_ref[...])

    vre = _bdot(cf_ref[...], vb)
    vim = _bdot(sf_ref[...], vb)
    kre, kim = kre_ref[...], kim_ref[...]
    yre = vre * kre + vim * kim
    nyim = vim * kre - vre * kim
    o_ref[0] += _bdot(ci_ref[...], yre.astype(BF16)) + _bdot(si_ref[...], nyim.astype(BF16))


def hyena_long_conv(vb, cmat, smat, kre, kim, knyq, *, ct=512, fk=256):
    B, L, _ = vb.shape
    return pl.pallas_call(
        _hy_conv_kernel,
        grid=(B, D // ct, L // fk),
        in_specs=[
            pl.BlockSpec((1, L, ct), lambda b, j, k: (b, 0, j)),
            pl.BlockSpec((fk, L), lambda b, j, k: (k, 0)),
            pl.BlockSpec((fk, L), lambda b, j, k: (k, 0)),
            pl.BlockSpec((L, fk), lambda b, j, k: (0, k)),
            pl.BlockSpec((L, fk), lambda b, j, k: (0, k)),
            pl.BlockSpec((fk, ct), lambda b, j, k: (k, j)),
            pl.BlockSpec((fk, ct), lambda b, j, k: (k, j)),
            pl.BlockSpec((1, ct), lambda b, j, k: (0, j)),
        ],
        out_specs=pl.BlockSpec((1, L, ct), lambda b, j, k: (b, 0, j)),
        out_shape=jax.ShapeDtypeStruct((B, L, D), F32),
        compiler_params=_cparams(("parallel", "parallel", "arbitrary"), 52),
        name="hyena_long_conv",
    )(vb, cmat, smat, cmat, smat, kre, kim, knyq)


def _hy_out_kernel(y_ref, vg_ref, x0_ref, skip_ref, x_ref, mod_ref, w_ref, o_ref):
    a = ((y_ref[0] + vg_ref[0] * skip_ref[...]) * x0_ref[0]).astype(BF16)
    o_ref[0] = x_ref[0] + mod_ref[0, 2:3, :] * _bdot(a, w_ref[...])


def hyena_out(y, vg, x0, skip, x, mod, w, *, tm=512):
    B, L, _ = x.shape
    tok = pl.BlockSpec((1, tm, D), lambda b, i: (b, i, 0))
    return pl.pallas_call(
        _hy_out_kernel,
        grid=(B, L // tm),
        in_specs=[tok, tok, tok, _resident((1, D)), tok, pl.BlockSpec((1, 6, D), lambda b, i: (b, 0, 0)),
                  _resident((D, D))],
        out_specs=tok,
        out_shape=jax.ShapeDtypeStruct((B, L, D), F32),
        compiler_params=_cparams(("parallel", "parallel"), 40),
        name="hyena_out",
    )(y, vg, x0, skip.reshape(1, D), x, mod, w)


def dft_tables(L):
    k = jnp.arange(L, dtype=jnp.int32)
    kt = (k[:, None] * k[None, :]) % (2 * L)
    ang = kt.astype(F32) * (math.pi / L)
    return jnp.cos(ang).astype(BF16), jnp.sin(ang).astype(BF16)


def hyena_layer(x, mod, nw, p):
    (w_in, conv_w, conv_b, fw1, fb1, ff1, fw2, fb2, ff2, fw3, skip, w_out) = p
    L = x.shape[1]
    cmat, smat = dft_tables(L)
    hs, hd = hyena_filter_taps(L, fw1, fb1, ff1, fw2, fb2, ff2, fw3)
    kre, kim, knyq = hyena_filter_dft(hs, hd, cmat, smat)
    u = norm_mod_matmul(x, mod, nw, w_in.astype(BF16))
    x0, vg, vb = hyena_pre(u, conv_w, conv_b)
    y = hyena_long_conv(vb, cmat, smat, kre, kim, knyq)
    return hyena_out(y, vg, x0, skip, x, mod, w_out.astype(BF16))


def _gdn_in_kernel(x_ref, mod_ref, nw_ref, w_ref, wab_ref, wabt_ref, o_ref, ab_ref, abt_ref, h_scr, *, tn):
    h = _norm_mod(x_ref[0], nw_ref[...], mod_ref, 0, 1)
    h_scr[...] = h.astype(BF16)
    for c0 in range(0, GDN_IN, tn):
        o_ref[0, :, c0:c0 + tn] = _bdot(h_scr[...], w_ref[:, c0:c0 + tn])
    ab_ref[0] = jnp.dot(h, wab_ref[...], preferred_element_type=F32, precision=HI)
    abt_ref[0] = _dot_nt(wabt_ref[...], h, precision=HI)


def gdn_in_proj(x, mod, nw, w_in, w_ab, *, tm=256, tn=512):
    B, L, _ = x.shape
    wab = jnp.pad(w_ab, ((0, 0), (0, LANE - w_ab.shape[1])))
    return pl.pallas_call(
        functools.partial(_gdn_in_kernel, tn=tn),
        grid=(B, L // tm),
        in_specs=[
            pl.BlockSpec((1, tm, D), lambda b, i: (b, i, 0)),
            pl.BlockSpec((1, 6, D), lambda b, i: (b, 0, 0)),
            _resident((1, D)),
            _resident((D, GDN_IN)),
            _resident((D, LANE)),
            _resident((LANE, D)),
        ],
        out_specs=[
            pl.BlockSpec((1, tm, GDN_IN), lambda b, i: (b, i, 0)),
            pl.BlockSpec((1, tm, LANE), lambda b, i: (b, i, 0)),
            pl.BlockSpec((1, LANE, tm), lambda b, i: (b, 0, i)),
        ],
        out_shape=[jax.ShapeDtypeStruct((B, L, GDN_IN), F32), jax.ShapeDtypeStruct((B, L, LANE), F32),
                   jax.ShapeDtypeStruct((B, LANE, L), F32)],
        scratch_shapes=[pltpu.VMEM((tm, D), BF16)],
        compiler_params=_cparams(("parallel", "parallel"), 48),
        name="gdn_in_proj",
    )(x, mod, nw.reshape(1, D), w_in.astype(BF16), wab, wab.T)


def _gdn_pre_kernel(u_ref, w_ref, b_ref, o_ref):
    L = u_ref.shape[1]
    j = pl.program_id(1)
    t = lax.broadcasted_iota(jnp.int32, (L, 1), 0)
    u = u_ref[0]
    prev = jnp.where(t == 0, 0.0, pltpu.roll(u, 1, 0))
    nxt = jnp.where(t == L - 1, 0.0, pltpu.roll(u, L - 1, 0))
    y = b_ref[...] + w_ref[0:1, :] * prev + w_ref[1:2, :] * u + w_ref[2:3, :] * nxt
    y = y * jax.nn.sigmoid(y)
    nh = y.shape[1] // GDN_DK
    nq_tiles = GDN_HK * GDN_DK // y.shape[1]

    @pl.when(j >= 2 * nq_tiles)
    def _():
        o_ref[0] = y

    @pl.when(j < 2 * nq_tiles)
    def _():
        scale = jnp.where(j < nq_tiles, GDN_DK ** -0.5, 1.0)
        for h in range(nh):
            yh = y[:, h * GDN_DK:(h + 1) * GDN_DK]
            o_ref[0, :, h * GDN_DK:(h + 1) * GDN_DK] = yh * (
                lax.rsqrt(jnp.sum(yh * yh, axis=-1, keepdims=True) + EPS) * scale)


def gdn_pre(proj, conv_w, conv_b, *, ct=256):
    B, L, _ = proj.shape
    return pl.pallas_call(
        _gdn_pre_kernel,
        grid=(B, GDN_QKV // ct),
        in_specs=[
            pl.BlockSpec((1, L, ct), lambda b, j: (b, 0, j)),
            pl.BlockSpec((3, ct), lambda b, j: (0, j)),
            pl.BlockSpec((1, ct), lambda b, j: (0, j)),
        ],
        out_specs=pl.BlockSpec((1, L, ct), lambda b, j: (b, 0, j)),
        out_shape=jax.ShapeDtypeStruct((B, L, GDN_QKV), F32),
        compiler_params=_cparams(("parallel", "parallel"), 40),
        name="gdn_pre",
    )(proj, conv_w, conv_b.reshape(1, GDN_QKV))


def _softplus(x):
    return jnp.maximum(x, 0.0) + jnp.log1p(jnp.exp(-jnp.abs(x)))


def _gdn_gates_kernel(ab_ref, abt_ref, alog_c_ref, dtb_c_ref, alog_r_ref, dtb_r_ref, col_ref, row_ref, *, tl):
    C = GDN_CHUNK
    H = GDN_HV
    i = lax.broadcasted_iota(jnp.int32, (C, C), 0)
    j = lax.broadcasted_iota(jnp.int32, (C, C), 1)
    lower = (i >= j).astype(F32)
    upper = (i <= j).astype(F32)
    ab = ab_ref[0]
    g_c = -jnp.exp(alog_c_ref[...]) * _softplus(ab[:, :2 * H] + dtb_c_ref[...])
    beta_c = jax.nn.sigmoid(ab[:, 2 * H:4 * H])
    abt = abt_ref[0]
    g_r = -jnp.exp(alog_r_ref[...]) * _softplus(abt[:2 * H, :] + dtb_r_ref[...])
    beta_r = jax.nn.sigmoid(abt[2 * H:4 * H, :])
    zc = jnp.zeros((C, LANE - 4 * H), F32)
    for c in range(tl // C):
        gch = g_c[c * C:(c + 1) * C, :]
        pre = jnp.dot(lower, gch, preferred_element_type=F32, precision=HI)
        suf = jnp.dot(upper, gch, preferred_element_type=F32, precision=HI)
        tot = jnp.sum(gch, axis=0, keepdims=True)
        grc = g_r[:, c * C:(c + 1) * C]
        pre_r = jnp.dot(grc, upper, preferred_element_type=F32, precision=HI)
        suf_r = jnp.dot(grc, lower, preferred_element_type=F32, precision=HI)
        tot_r = jnp.sum(grc, axis=1, keepdims=True)
        for d in range(2):
            hs = slice(d * H, (d + 1) * H)
            gc = (pre if d == 0 else suf)[:, hs]
            bc = beta_c[c * C:(c + 1) * C, hs]
            col_ref[0, d, c * C:(c + 1) * C, :] = jnp.concatenate(
                [gc, bc, jnp.exp(gc), jnp.exp(tot[:, hs] - gc), zc], axis=1)
            gr = (pre_r if d == 0 else suf_r)[hs, :]
            br = beta_r[hs, c * C:(c + 1) * C]
            half = jnp.concatenate([gr, br, br * jnp.exp(gr), jnp.broadcast_to(jnp.exp(tot_r[hs, :]), (H, C))],
                                   axis=0)
            row_ref[0, d, c] = jnp.concatenate([half, half], axis=1)


def gdn_gates(ab, abt, a_log, dt_bias, *, tl=512):
    B, L, _ = ab.shape
    N = L // GDN_CHUNK
    H2 = 2 * GDN_HV
    return pl.pallas_call(
        functools.partial(_gdn_gates_kernel, tl=tl),
        grid=(B, L // tl),
        in_specs=[
            pl.BlockSpec((1, tl, LANE), lambda b, i: (b, i, 0)),
            pl.BlockSpec((1, LANE, tl), lambda b, i: (b, 0, i)),
            _resident((1, H2)), _resident((1, H2)), _resident((H2, 1)), _resident((H2, 1)),
        ],
        out_specs=[
            pl.BlockSpec((1, 2, tl, LANE), lambda b, i: (b, 0, i, 0)),
            pl.BlockSpec((1, 2, tl // GDN_CHUNK, GDN_CHUNK, LANE), lambda b, i: (b, 0, i, 0, 0)),
        ],
        out_shape=[jax.ShapeDtypeStruct((B, 2, L, LANE), F32),
                   jax.ShapeDtypeStruct((B, 2, N, GDN_CHUNK, LANE), F32)],
        compiler_params=_cparams(("parallel", "parallel"), 32),
        name="gdn_gates",
    )(ab, abt, a_log.reshape(1, H2), dt_bias.reshape(1, H2), a_log.reshape(H2, 1), dt_bias.reshape(H2, 1))


def _gdn_chunk_kernel(q_ref, k_ref, v_ref, col_ref, row_ref, o_ref, s_scr):
    C = GDN_CHUNK
    H = GDN_HV
    d = pl.program_id(1)
    n = pl.program_id(2)

    @pl.when(n == 0)
    def _():
        s_scr[...] = jnp.zeros_like(s_scr)

    sgn = 1 - 2 * d
    i = lax.broadcasted_iota(jnp.int32, (C, C), 0)
    j = lax.broadcasted_iota(jnp.int32, (C, C), 1)
    order = (i - j) * sgn
    incl = order >= 0
    strict = order > 0
    eye = (i == j).astype(F32)
    col = col_ref[0, 0]
    row = row_ref[0, 0, 0]
    rep = GDN_HV // GDN_HK
    for h in range(H):
        hk = h // rep
        q = q_ref[0, :, hk * GDN_DK:(hk + 1) * GDN_DK]
        k = k_ref[0, :, hk * GDN_DK:(hk + 1) * GDN_DK]
        v = v_ref[0, :, h * GDN_DV:(h + 1) * GDN_DV]
        kb = k.astype(BF16)
        gram = _dot_nt(jnp.concatenate([k, q], axis=0).astype(BF16), kb)
        gc_i = col[:, h:h + 1]
        beta_i = col[:, H + h:H + h + 1]
        egc_i = col[:, 2 * H + h:2 * H + h + 1]
        kdec_i = col[:, 3 * H + h:3 * H + h + 1]
        gc_j = row[h:h + 1, :C]
        beta_j = row[H + h:H + h + 1, :C]
        bge_j = row[2 * H + h:2 * H + h + 1, :C]
        g_end = row[3 * H + h:3 * H + h + 1, :]
        decay = jnp.where(incl, jnp.exp(jnp.minimum(gc_i - gc_j, 0.0)), 0.0)
        a = jnp.where(strict, gram[:C] * decay * beta_i, 0.0)
        t = eye - a
        m = a
        for _ in range(5):
            m = jnp.dot(m, m, preferred_element_type=F32, precision=HI)
            t = t + jnp.dot(t, m, preferred_element_type=F32, precision=HI)
        u = _bdot((t * beta_j).astype(BF16), v.astype(BF16))
        w = _bdot((t * bge_j).astype(BF16), kb)
        s = s_scr[h]
        ws_qs = _bdot(jnp.concatenate([w, q], axis=0).astype(BF16), s.astype(BF16))
        v_new = u - ws_qs[:C]
        intra = gram[C:] * decay
        o_ref[0, 0, :, h * GDN_DV:(h + 1) * GDN_DV] = egc_i * ws_qs[C:] + _bdot(intra.astype(BF16),
                                                                               v_new.astype(BF16))
        s_scr[h] = s * g_end + _dot_tn(kb, (v_new * kdec_i).astype(BF16))


def gdn_chunk_scan(qkv, col, row):
    B, L, _ = qkv.shape
    C = GDN_CHUNK
    N = L // C
    nq = GDN_HK * GDN_DK

    def cidx(d, n):
        return n + d * (N - 1 - 2 * n)

    return pl.pallas_call(
        _gdn_chunk_kernel,
        grid=(B, 2, N),
        in_specs=[
            pl.BlockSpec((1, C, nq), lambda b, d, n: (b, cidx(d, n), 0)),
            pl.BlockSpec((1, C, nq), lambda b, d, n: (b, cidx(d, n), 1)),
            pl.BlockSpec((1, C, GDN_HV * GDN_DV), lambda b, d, n: (b, cidx(d, n), 1)),
            pl.BlockSpec((1, 1, C, LANE), lambda b, d, n: (b, d, cidx(d, n), 0)),
            pl.BlockSpec((1, 1, 1, C, LANE), lambda b, d, n: (b, d, cidx(d, n), 0, 0)),
        ],
        out_specs=pl.BlockSpec((1, 1, C, GDN_HV * GDN_DV), lambda b, d, n: (b, d, cidx(d, n), 0)),
        out_shape=jax.ShapeDtypeStruct((B, 2, L, GDN_HV * GDN_DV), F32),
        scratch_shapes=[pltpu.VMEM((GDN_HV, GDN_DK, GDN_DV), F32)],
        compiler_params=_cparams(("parallel", "parallel", "arbitrary"), 32),
        name="gdn_chunk_scan",
    )(qkv, qkv, qkv, col, row)


def _gdn_out_kernel(o_ref, z_ref, nw_ref, x_ref, mod_ref, w_ref, out_ref, a_scr):
    o = o_ref[0, 0] + o_ref[0, 1]
    z = z_ref[0]
    gate = z * jax.nn.sigmoid(z)
    for h in range(GDN_HV):
        hs = slice(h * GDN_DV, (h + 1) * GDN_DV)
        a_scr[:, hs] = (_rms(o[:, hs]) * nw_ref[...] * gate[:, hs]).astype(BF16)
    out_ref[0] = x_ref[0] + mod_ref[0, 2:3, :] * _bdot(a_scr[...], w_ref[...])


def gdn_out(o2, proj, norm_w, x, mod, w, *, tm=256):
    B, L, _ = x.shape
    hd = GDN_HV * GDN_DV
    return pl.pallas_call(
        _gdn_out_kernel,
        grid=(B, L // tm),
        in_specs=[
            pl.BlockSpec((1, 2, tm, hd), lambda b, i: (b, 0, i, 0)),
            pl.BlockSpec((1, tm, hd), lambda b, i: (b, i, GDN_QKV // hd)),
            _resident((1, GDN_DV)),
            pl.BlockSpec((1, tm, D), lambda b, i: (b, i, 0)),
            pl.BlockSpec((1, 6, D), lambda b, i: (b, 0, 0)),
            _resident((hd, D)),
        ],
        out_specs=pl.BlockSpec((1, tm, D), lambda b, i: (b, i, 0)),
        out_shape=jax.ShapeDtypeStruct((B, L, D), F32),
        scratch_shapes=[pltpu.VMEM((tm, hd), BF16)],
        compiler_params=_cparams(("parallel", "parallel"), 40),
        name="gdn_out",
    )(o2, proj, norm_w.reshape(1, GDN_DV), x, mod, w)


def gdn_layer(x, mod, nw, p):
    (w_in, conv_w, conv_b, w_ab, a_log, dt_bias, norm_w, w_out) = p
    proj, ab, abt = gdn_in_proj(x, mod, nw, w_in, w_ab)
    qkv = gdn_pre(proj, conv_w, conv_b)
    col, row = gdn_gates(ab, abt, a_log, dt_bias)
    o2 = gdn_chunk_scan(qkv, col, row)
    return gdn_out(o2, proj, norm_w, x, mod, w_out.astype(BF16))


def _swa_kernel(q_ref, kp_ref, kc_ref, kn_ref, vp_ref, vc_ref, vn_ref, sink_ref, o_ref, *, nb):
    W = SWA_BLOCK
    n = pl.program_id(1)
    qi = lax.broadcasted_iota(jnp.int32, (W, 3 * W), 0)
    kj = lax.broadcasted_iota(jnp.int32, (W, 3 * W), 1)
    rel = kj - W - qi
    dist = jnp.abs(rel)
    valid = (dist <= SWA_WINDOW) & ((kj >= W) | (n > 0)) & ((kj < 2 * W) | (n < nb - 1))
    distf = dist.astype(F32)
    scale = SWA_DH ** -0.5
    for h in range(SWA_HKV):
        hs = slice(h * SWA_DH, (h + 1) * SWA_DH)
        kb = jnp.concatenate([kp_ref[0, :, hs], kc_ref[0, :, hs], kn_ref[0, :, hs]], axis=0).astype(BF16)
        vb = jnp.concatenate([vp_ref[0, :, hs], vc_ref[0, :, hs], vn_ref[0, :, hs]], axis=0).astype(BF16)
        for g in range(SWA_G):
            hq = h * SWA_G + g
            qs = slice(hq * SWA_DH, (hq + 1) * SWA_DH)
            slope = 2.0 ** (-8.0 * (hq + 1) / SWA_HQ)
            s = _dot_nt(q_ref[0, :, qs].astype(BF16), kb) * scale - slope * distf
            s = jnp.where(valid, s, NEG)
            sink = sink_ref[0:1, hq:hq + 1]
            m = jnp.maximum(jnp.max(s, axis=-1, keepdims=True), sink)
            e = jnp.exp(s - m)
            denom = jnp.sum(e, axis=-1, keepdims=True) + jnp.exp(sink - m)
            o_ref[0, :, qs] = _bdot(e.astype(BF16), vb) / denom


def swa_attention(qkv, sink):
    B, L, _ = qkv.shape
    W = SWA_BLOCK
    nb = L // W
    nq = SWA_HQ * SWA_DH
    nk = SWA_HKV * SWA_DH
    kcol, vcol = nq // nk, nq // nk + 1

    def band(col):
        return [pl.BlockSpec((1, W, nk), lambda b, n: (b, jnp.maximum(n - 1, 0), col)),
                pl.BlockSpec((1, W, nk), lambda b, n: (b, n, col)),
                pl.BlockSpec((1, W, nk), lambda b, n: (b, jnp.minimum(n + 1, nb - 1), col))]

    return pl.pallas_call(
        functools.partial(_swa_kernel, nb=nb),
        grid=(B, nb),
        in_specs=[pl.BlockSpec((1, W, nq), lambda b, n: (b, n, 0))] + band(kcol) + band(vcol) + [
            _resident((1, SWA_HQ))],
        out_specs=pl.BlockSpec((1, W, nq), lambda b, n: (b, n, 0)),
        out_shape=jax.ShapeDtypeStruct((B, L, nq), F32),
        compiler_params=_cparams(("parallel", "parallel"), 32),
        name="swa_attention",
    )(qkv, qkv, qkv, qkv, qkv, qkv, qkv, sink.reshape(1, SWA_HQ))


def swa_layer(x, mod, nw, p):
    (w_qkv, sink, w_out) = p
    qkv = norm_mod_matmul(x, mod, nw, w_qkv.astype(BF16))
    o = swa_attention(qkv, sink)
    return matmul_gated_residual(o, x, mod, w_out.astype(BF16))


def _mla_proj_kernel(d_ref, cos_ref, sin_ref, qnw_ref, kvnw_ref, wq_ref, wqr_ref, wk_ref, wv_ref,
                     q_ref, k_ref, v_ref):
    dd = d_ref[0]
    cq = (_rms(dd[:, :MLA_QRANK]) * qnw_ref[...]).astype(BF16)
    ckv = (_rms(dd[:, MLA_QRANK:MLA_QRANK + MLA_KVRANK]) * kvnw_ref[...]).astype(BF16)
    cs, sn = cos_ref[...], sin_ref[...]
    base = MLA_QRANK + MLA_KVRANK
    k_rope = dd[:, base:base + MLA_HP] * cs + dd[:, base + MLA_HP:base + 2 * MLA_HP] * sn
    qa = _bdot(cq, wq_ref[...])
    qb = _bdot(cq, wqr_ref[...])
    kn = _bdot(ckv, wk_ref[...])
    scale = (MLA_NOPE + MLA_ROPE) ** -0.5
    for h in range(MLA_H):
        hs = slice(h * MLA_HP, (h + 1) * MLA_HP)
        q_ref[0, :, hs] = ((qa[:, hs] * cs + qb[:, hs] * sn) * scale).astype(BF16)
        k_ref[0, :, hs] = (kn[:, hs] + k_rope).astype(BF16)
    v_ref[0] = _bdot(ckv, wv_ref[...]).astype(BF16)


def mla_project(dlat, cos_t, sin_t, q_norm_w, kv_norm_w, wq, wqr, wk, wv, *, tm=256):
    B, L, N = dlat.shape
    hp = MLA_H * MLA_HP
    return pl.pallas_call(
        _mla_proj_kernel,
        grid=(B, L // tm),
        in_specs=[
            pl.BlockSpec((1, tm, N), lambda b, i: (b, i, 0)),
            pl.BlockSpec((tm, MLA_HP), lambda b, i: (i, 0)),
            pl.BlockSpec((tm, MLA_HP), lambda b, i: (i, 0)),
            _resident((1, MLA_QRANK)), _resident((1, MLA_KVRANK)),
            _resident(wq.shape), _resident(wqr.shape), _resident(wk.shape), _resident(wv.shape),
        ],
        out_specs=[
            pl.BlockSpec((1, tm, hp), lambda b, i: (b, i, 0)),
            pl.BlockSpec((1, tm, hp), lambda b, i: (b, i, 0)),
            pl.BlockSpec((1, tm, MLA_H * MLA_DV), lambda b, i: (b, i, 0)),
        ],
        out_shape=[jax.ShapeDtypeStruct((B, L, hp), BF16), jax.ShapeDtypeStruct((B, L, hp), BF16),
                   jax.ShapeDtypeStruct((B, L, MLA_H * MLA_DV), BF16)],
        compiler_params=_cparams(("parallel", "parallel"), 40),
        name="mla_project",
    )(dlat, cos_t, sin_t, q_norm_w.reshape(1, -1), kv_norm_w.reshape(1, -1), wq, wqr, wk, wv)


def _mla_attn_kernel(q_ref, k_ref, v_ref, o_ref):
    outs = []
    for h in range(2):
        q = q_ref[0, :, h * MLA_HP:(h + 1) * MLA_HP]
        k = k_ref[0, :, h * MLA_HP:(h + 1) * MLA_HP]
        v = v_ref[0, :, h * MLA_DV:(h + 1) * MLA_DV]
        s = _dot_nt(q, k)
        e = jnp.exp(s - jnp.max(s, axis=-1, keepdims=True))
        outs.append(_bdot(e.astype(BF16), v) / jnp.sum(e, axis=-1, keepdims=True))
    o_ref[0] = jnp.concatenate(outs, axis=1)


def mla_attention(q, k, v, *, tq=256):
    B, L, _ = q.shape
    return pl.pallas_call(
        _mla_attn_kernel,
        grid=(B, MLA_H // 2, L // tq),
        in_specs=[
            pl.BlockSpec((1, tq, 2 * MLA_HP), lambda b, h, i: (b, i, h)),
            pl.BlockSpec((1, L, 2 * MLA_HP), lambda b, h, i: (b, 0, h)),
            pl.BlockSpec((1, L, 2 * MLA_DV), lambda b, h, i: (b, 0, h)),
        ],
        out_specs=pl.BlockSpec((1, tq, 2 * MLA_DV), lambda b, h, i: (b, i, h)),
        out_shape=jax.ShapeDtypeStruct((B, L, MLA_H * MLA_DV), F32),
        compiler_params=_cparams(("parallel", "parallel", "parallel"), 40),
        name="mla_attention",
    )(q, k, v)


def _rot_half_cols(w):
    half = MLA_ROPE // 2
    return jnp.concatenate([-w[..., half:], w[..., :half]], axis=-1)


def mla_layer(x, mod, nw, p):
    (w_down, q_norm_w, w_uq, kv_norm_w, w_ukv, w_out) = p
    L = x.shape[1]
    pad_r = MLA_HP - MLA_NOPE - MLA_ROPE
    base = MLA_QRANK + MLA_KVRANK
    w_rope = w_down[:, base:]
    zl = jnp.zeros((D, MLA_NOPE), F32)
    zr = jnp.zeros((D, pad_r), F32)
    w_dext = jnp.concatenate([w_down[:, :base], zl, w_rope, zr, zl, _rot_half_cols(w_rope), zr], axis=1)
    wq3 = w_uq.reshape(MLA_QRANK, MLA_H, MLA_NOPE + MLA_ROPE)
    zq = jnp.zeros((MLA_QRANK, MLA_H, pad_r), F32)
    wq = jnp.concatenate([wq3, zq], axis=-1).reshape(MLA_QRANK, -1).astype(BF16)
    wqr = jnp.concatenate([jnp.zeros((MLA_QRANK, MLA_H, MLA_NOPE), F32), _rot_half_cols(wq3[..., MLA_NOPE:]), zq],
                          axis=-1).reshape(MLA_QRANK, -1).astype(BF16)
    wkv3 = w_ukv.reshape(MLA_KVRANK, MLA_H, MLA_NOPE + MLA_DV)
    wk = jnp.concatenate([wkv3[..., :MLA_NOPE], jnp.zeros((MLA_KVRANK, MLA_H, MLA_HP - MLA_NOPE), F32)],
                         axis=-1).reshape(MLA_KVRANK, -1).astype(BF16)
    wv = wkv3[..., MLA_NOPE:].reshape(MLA_KVRANK, -1).astype(BF16)
    inv = ROPE_THETA ** (-jnp.arange(0, MLA_ROPE, 2, dtype=F32) / MLA_ROPE)
    ang = jnp.arange(L, dtype=F32)[:, None] * inv[None, :]
    cos, sin = jnp.cos(ang), jnp.sin(ang)
    cos_t = jnp.concatenate([jnp.ones((L, MLA_NOPE), F32), cos, cos, jnp.zeros((L, pad_r), F32)], axis=1)
    sin_t = jnp.concatenate([jnp.zeros((L, MLA_NOPE), F32), sin, sin, jnp.zeros((L, pad_r), F32)], axis=1)

    dlat = norm_mod_matmul(x, mod, nw, w_dext.astype(BF16))
    q, k, v = mla_project(dlat, cos_t, sin_t, q_norm_w, kv_norm_w, wq, wqr, wk, wv)
    o = mla_attention(q, k, v)
    return matmul_gated_residual(o, x, mod, w_out.astype(BF16))


def encoder_trunk(x, c, ada_w, ada_b, norm_w, hy, gdn, swa, mla, ffn_w_gu, ffn_w_down, final_norm_w):
    mods = ada_modulation(c, ada_w, ada_b)
    layers = (hyena_layer, gdn_layer, swa_layer, mla_layer)
    params = (hy, gdn, swa, mla)
    for i in range(DEPTH):
        kind, j = i % 4, i // 4
        x = layers[kind](x, mods[i], norm_w[i, 0], [p[j] for p in params[kind]])
        x = ffn_block(x, mods[i], norm_w[i, 1], ffn_w_gu[i].astype(BF16), ffn_w_down[i].astype(BF16))
    return final_norm(x, final_norm_w)


def kernel(x_prompt, x_sample, c_prompt, c_sample, ada_w, ada_b, norm_w, hy_w_in, hy_conv_w, hy_conv_b, hy_filt_w1, hy_filt_b1, hy_filt_freq1, hy_filt_w2, hy_filt_b2, hy_filt_freq2, hy_filt_w3, hy_skip, hy_w_out, gdn_w_in, gdn_conv_w, gdn_conv_b, gdn_w_ab, gdn_a_log, gdn_dt_bias, gdn_norm_w, gdn_w_out, swa_w_qkv, swa_sink, swa_w_out, mla_w_down, mla_q_norm_w, mla_w_uq, mla_kv_norm_w, mla_w_ukv, mla_w_out, ffn_w_gu, ffn_w_down, final_norm_w):
    hy = (hy_w_in, hy_conv_w, hy_conv_b, hy_filt_w1, hy_filt_b1, hy_filt_freq1,
          hy_filt_w2, hy_filt_b2, hy_filt_freq2, hy_filt_w3, hy_skip, hy_w_out)
    gdn = (gdn_w_in, gdn_conv_w, gdn_conv_b, gdn_w_ab, gdn_a_log, gdn_dt_bias, gdn_norm_w, gdn_w_out)
    swa = (swa_w_qkv, swa_sink, swa_w_out)
    mla = (mla_w_down, mla_q_norm_w, mla_w_uq, mla_kv_norm_w, mla_w_ukv, mla_w_out)
    args = (ada_w, ada_b, norm_w, hy, gdn, swa, mla, ffn_w_gu, ffn_w_down, final_norm_w)
    return (encoder_trunk(x_prompt, c_prompt, *args), encoder_trunk(x_sample, c_sample, *args))
```

```python
import functools
import math

import jax
import jax.numpy as jnp
from jax import lax
from jax.experimental import pallas as pl
from jax.experimental.pallas import tpu as pltpu

F32 = jnp.float32
BF16 = jnp.bfloat16
HI = lax.Precision.HIGHEST

D = 1024
DEPTH = 4
EPS = 1e-6
D_FF = 2816

HY_BANDS = 16
HY_FILT = 64
HY_TARGET = 1e-2
HY_FAST = 0.3
HY_SLOW = 1.5

GDN_HK = 8
GDN_HV = 16
GDN_DK = 128
GDN_DV = 128
GDN_CHUNK = 64
GDN_QKV = 2 * GDN_HK * GDN_DK + GDN_HV * GDN_DV
GDN_IN = GDN_QKV + GDN_HV * GDN_DV

SWA_HQ = 16
SWA_HKV = 4
SWA_G = SWA_HQ // SWA_HKV
SWA_DH = 64
SWA_WINDOW = 128
SWA_BLOCK = 128

MLA_H = 16
MLA_NOPE = 64
MLA_ROPE = 32
MLA_DV = 64
MLA_QRANK = 256
MLA_KVRANK = 256
MLA_HP = 128
ROPE_THETA = 10000.0

LANE = 128
MIB = 2 ** 20
NEG = -1e30


def _cparams(sem, vmem_mb):
    return pltpu.CompilerParams(dimension_semantics=sem, vmem_limit_bytes=vmem_mb * MIB)


def _resident(shape):
    nd = len(shape)
    return pl.BlockSpec(shape, lambda *_: (0,) * nd, pipeline_mode=pl.Buffered(1))


def _bdot(a, b):
    return jnp.dot(a, b, preferred_element_type=F32)


def _dot_nt(a, b, precision=None):
    return lax.dot_general(a, b, (((1,), (1,)), ((), ())), preferred_element_type=F32, precision=precision)


def _dot_tn(a, b, precision=None):
    return lax.dot_general(a, b, (((0,), (0,)), ((), ())), preferred_element_type=F32, precision=precision)


def _split_bf16(x):
    hi = x.astype(BF16)
    return hi, (x - hi.astype(F32)).astype(BF16)


def _rms(x):
    return x * lax.rsqrt(jnp.mean(x * x, axis=-1, keepdims=True) + EPS)


def _norm_mod(x, nw, mod_ref, sh_row, sc_row):
    return _rms(x) * nw * (1.0 + mod_ref[0, sc_row:sc_row + 1, :]) + mod_ref[0, sh_row:sh_row + 1, :]


def _ada_kernel(c_ref, w_ref, b_ref, o_ref):
    c = c_ref[...]
    ca = c * jax.nn.sigmoid(c)
    o_ref[0] = jnp.dot(ca, w_ref[0], preferred_element_type=F32, precision=HI) + b_ref[0]


def ada_modulation(c, ada_w, ada_b):
    B = c.shape[0]
    tn = 1024
    out = pl.pallas_call(
        _ada_kernel,
        grid=(DEPTH, 6 * D // tn),
        in_specs=[
            pl.BlockSpec((B, D), lambda i, j: (0, 0)),
            pl.BlockSpec((1, D, tn), lambda i, j: (i, 0, j)),
            pl.BlockSpec((1, 1, tn), lambda i, j: (i, 0, j)),
        ],
        out_specs=pl.BlockSpec((1, B, tn), lambda i, j: (i, 0, j)),
        out_shape=jax.ShapeDtypeStruct((DEPTH, B, 6 * D), F32),
        compiler_params=_cparams(("parallel", "parallel"), 32),
        name="ada_modulation",
    )(c, ada_w, ada_b.reshape(DEPTH, 1, 6 * D))
    return out.reshape(DEPTH, B, 6, D)


def _nmm_kernel(x_ref, mod_ref, nw_ref, w_ref, o_ref, h_scr, *, sh_row, sc_row, tn):
    h_scr[...] = _norm_mod(x_ref[0], nw_ref[...], mod_ref, sh_row, sc_row).astype(BF16)
    n = w_ref.shape[1]
    for c0 in range(0, n, tn):
        c1 = min(c0 + tn, n)
        o_ref[0, :, c0:c1] = _bdot(h_scr[...], w_ref[:, c0:c1])


def norm_mod_matmul(x, mod, nw, w, *, sh_row=0, sc_row=1, tm=512, tn=512):
    B, L, _ = x.shape
    N = w.shape[1]
    return pl.pallas_call(
        functools.partial(_nmm_kernel, sh_row=sh_row, sc_row=sc_row, tn=tn),
        grid=(B, L // tm),
        in_specs=[
            pl.BlockSpec((1, tm, D), lambda b, i: (b, i, 0)),
            pl.BlockSpec((1, 6, D), lambda b, i: (b, 0, 0)),
            _resident((1, D)),
            _resident((D, N)),
        ],
        out_specs=pl.BlockSpec((1, tm, N), lambda b, i: (b, i, 0)),
        out_shape=jax.ShapeDtypeStruct((B, L, N), F32),
        scratch_shapes=[pltpu.VMEM((tm, D), BF16)],
        compiler_params=_cparams(("parallel", "parallel"), 48),
        name="norm_mod_matmul",
    )(x, mod, nw.reshape(1, D), w)


def _mmres_kernel(a_ref, x_ref, mod_ref, w_ref, o_ref, *, g_row):
    y = _bdot(a_ref[0].astype(BF16), w_ref[...])
    o_ref[0] = x_ref[0] + mod_ref[0, g_row:g_row + 1, :] * y


def matmul_gated_residual(a, x, mod, w, *, g_row=2, tm=512):
    B, L, K = a.shape
    return pl.pallas_call(
        functools.partial(_mmres_kernel, g_row=g_row),
        grid=(B, L // tm),
        in_specs=[
            pl.BlockSpec((1, tm, K), lambda b, i: (b, i, 0)),
            pl.BlockSpec((1, tm, D), lambda b, i: (b, i, 0)),
            pl.BlockSpec((1, 6, D), lambda b, i: (b, 0, 0)),
            _resident((K, D)),
        ],
        out_specs=pl.BlockSpec((1, tm, D), lambda b, i: (b, i, 0)),
        out_shape=jax.ShapeDtypeStruct((B, L, D), F32),
        compiler_params=_cparams(("parallel", "parallel"), 40),
        name="matmul_gated_residual",
    )(a, x, mod, w)


def _ffn_kernel(x_ref, mod_ref, nw_ref, wgu_ref, wd_ref, o_ref, h_scr, act_scr, *, tf):
    x = x_ref[0]
    h_scr[...] = _norm_mod(x, nw_ref[...], mod_ref, 3, 4).astype(BF16)
    for c0 in range(0, D_FF, tf):
        gate = _bdot(h_scr[...], wgu_ref[:, c0:c0 + tf])
        up = _bdot(h_scr[...], wgu_ref[:, D_FF + c0:D_FF + c0 + tf])
        act_scr[:, c0:c0 + tf] = (gate * jax.nn.sigmoid(gate) * up).astype(BF16)
    o_ref[0] = x + mod_ref[0, 5:6, :] * _bdot(act_scr[...], wd_ref[...])


def ffn_block(x, mod, nw, w_gu, w_down, *, tm=512, tf=256):
    B, L, _ = x.shape
    return pl.pallas_call(
        functools.partial(_ffn_kernel, tf=tf),
        grid=(B, L // tm),
        in_specs=[
            pl.BlockSpec((1, tm, D), lambda b, i: (b, i, 0)),
            pl.BlockSpec((1, 6, D), lambda b, i: (b, 0, 0)),
            _resident((1, D)),
            _resident((D, 2 * D_FF)),
            _resident((D_FF, D)),
        ],
        out_specs=pl.BlockSpec((1, tm, D), lambda b, i: (b, i, 0)),
        out_shape=jax.ShapeDtypeStruct((B, L, D), F32),
        scratch_shapes=[pltpu.VMEM((tm, D), BF16), pltpu.VMEM((tm, D_FF), BF16)],
        compiler_params=_cparams(("parallel", "parallel"), 48),
        name="ffn_block",
    )(x, mod, nw.reshape(1, D), w_gu, w_down)


def _final_norm_kernel(x_ref, nw_ref, o_ref):
    o_ref[0] = _rms(x_ref[0]) * nw_ref[...]


def final_norm(x, nw, *, tm=1024):
    B, L, _ = x.shape
    tm = min(tm, L)
    return pl.pallas_call(
        _final_norm_kernel,
        grid=(B, L // tm),
        in_specs=[pl.BlockSpec((1, tm, D), lambda b, i: (b, i, 0)), _resident((1, D))],
        out_specs=pl.BlockSpec((1, tm, D), lambda b, i: (b, i, 0)),
        out_shape=jax.ShapeDtypeStruct((B, L, D), F32),
        compiler_params=_cparams(("parallel", "parallel"), 32),
        name="final_norm",
    )(x, nw.reshape(1, D))


def _hy_filter_kernel(freq_ref, w1t_ref, w1c_ref, w1s_ref, b1_ref, f1_ref, w2_ref, b2_ref, f2_ref, w3_ref,
                      rate_ref, hs_ref, hd_ref, *, L, tl):
    pos = (pl.program_id(0) * tl + lax.broadcasted_iota(jnp.int32, (tl, 1), 0)).astype(F32)
    t = pos / max(L - 1, 1)
    ang = freq_ref[...] * (2.0 * math.pi / L) * pos
    z = (t * w1t_ref[...] + jnp.dot(jnp.cos(ang), w1c_ref[...], preferred_element_type=F32, precision=HI)
         - jnp.dot(jnp.sin(ang), w1s_ref[...], preferred_element_type=F32, precision=HI) + b1_ref[...])
    z = jnp.sin(f1_ref[...] * z)
    z = jnp.sin(f2_ref[...] * (jnp.dot(z, w2_ref[...], preferred_element_type=F32, precision=HI) + b2_ref[...]))
    z = jnp.dot(z, w3_ref[...], preferred_element_type=F32, precision=HI)
    window = jnp.exp(-t * rate_ref[...])
    hf = z[:, :D] * window
    hb = jnp.where(pos == 0.0, 0.0, z[:, D:] * window)
    hs_ref[...] = hf + hb
    hd_ref[...] = hb - hf


def hyena_filter_taps(L, fw1, fb1, ff1, fw2, fb2, ff2, fw3):
    tl = 256
    freqs = jnp.linspace(1e-4, HY_BANDS - 1, HY_BANDS, dtype=F32).reshape(1, HY_BANDS)
    rates = jnp.abs(jnp.linspace(math.log(HY_TARGET) / HY_SLOW, math.log(HY_TARGET) / HY_FAST, D, dtype=F32))
    small = [freqs, fw1[0:1], fw1[1:1 + HY_BANDS], fw1[1 + HY_BANDS:], fb1.reshape(1, -1), ff1.reshape(1, -1),
             fw2, fb2.reshape(1, -1), ff2.reshape(1, -1), fw3, rates.reshape(1, D)]
    return pl.pallas_call(
        functools.partial(_hy_filter_kernel, L=L, tl=tl),
        grid=(L // tl,),
        in_specs=[_resident(a.shape) for a in small],
        out_specs=[pl.BlockSpec((tl, D), lambda i: (i, 0))] * 2,
        out_shape=[jax.ShapeDtypeStruct((L, D), F32)] * 2,
        compiler_params=_cparams(("parallel",), 32),
        name="hyena_filter_taps",
    )(*small)


def _hy_filter_dft_kernel(hs_ref, hd_ref, cf_ref, sf_ref, kre_ref, kim_ref, knyq_ref, *, L, fk):
    k = pl.program_id(1)
    hs = hs_ref[...]
    inv_n = 1.0 / (2 * L)
    row = k * fk + lax.broadcasted_iota(jnp.int32, (fk, 1), 0)
    wk = jnp.where(row == 0, inv_n, 2.0 * inv_n)
    kre_ref[...] = wk * _bdot(cf_ref[...], hs.astype(BF16))
    kim_ref[...] = wk * _bdot(sf_ref[...], hd_ref[...].astype(BF16))

    @pl.when(k == 0)
    def _():
        t = lax.broadcasted_iota(jnp.int32, hs.shape, 0)
        sgn = (1 - 2 * (t & 1)).astype(F32)
        knyq_ref[...] = inv_n * jnp.sum(hs * sgn, axis=0, keepdims=True)


def hyena_filter_dft(hs, hd, cmat, smat, *, ct=256, fk=256):
    L = hs.shape[0]
    return pl.pallas_call(
        functools.partial(_hy_filter_dft_kernel, L=L, fk=fk),
        grid=(D // ct, L // fk),
        in_specs=[
            pl.BlockSpec((L, ct), lambda j, k: (0, j)),
            pl.BlockSpec((L, ct), lambda j, k: (0, j)),
            pl.BlockSpec((fk, L), lambda j, k: (k, 0)),
            pl.BlockSpec((fk, L), lambda j, k: (k, 0)),
        ],
        out_specs=[
            pl.BlockSpec((fk, ct), lambda j, k: (k, j)),
            pl.BlockSpec((fk, ct), lambda j, k: (k, j)),
            pl.BlockSpec((1, ct), lambda j, k: (0, j)),
        ],
        out_shape=[jax.ShapeDtypeStruct((L, D), F32), jax.ShapeDtypeStruct((L, D), F32),
                   jax.ShapeDtypeStruct((1, D), F32)],
        compiler_params=_cparams(("parallel", "arbitrary"), 48),
        name="hyena_filter_dft",
    )(hs, hd, cmat, smat)


def _hy_pre_kernel(u0_ref, u1_ref, u2_ref, w0_ref, w1_ref, w2_ref, b0_ref, b1_ref, b2_ref, x0_ref, vg_ref, vb_ref):
    L = u0_ref.shape[1]
    t = lax.broadcasted_iota(jnp.int32, (L, 1), 0)
    first, last = t == 0, t == L - 1

    def dwconv(u_ref, w_ref, b_ref):
        u = u_ref[0]
        prev = jnp.where(first, 0.0, pltpu.roll(u, 1, 0))
        nxt = jnp.where(last, 0.0, pltpu.roll(u, L - 1, 0))
        return b_ref[...] + w_ref[0:1, :] * prev + w_ref[1:2, :] * u + w_ref[2:3, :] * nxt

    x0_ref[0] = dwconv(u0_ref, w0_ref, b0_ref)
    vg = dwconv(u2_ref, w2_ref, b2_ref) * dwconv(u1_ref, w1_ref, b1_ref)
    vg_ref[0] = vg
    vb_ref[0] = vg.astype(BF16)


def hyena_pre(u, conv_w, conv_b, *, ct=128):
    B, L, _ = u.shape
    nj = D // ct
    conv_b = conv_b.reshape(1, 3 * D)
    ublk = [pl.BlockSpec((1, L, ct), lambda b, j, s=s: (b, 0, s * nj + j)) for s in range(3)]
    wblk = [pl.BlockSpec((3, ct), lambda b, j, s=s: (0, s * nj + j)) for s in range(3)]
    bblk = [pl.BlockSpec((1, ct), lambda b, j, s=s: (0, s * nj + j)) for s in range(3)]
    oblk = pl.BlockSpec((1, L, ct), lambda b, j: (b, 0, j))
    return pl.pallas_call(
        _hy_pre_kernel,
        grid=(B, nj),
        in_specs=ublk + wblk + bblk,
        out_specs=[oblk, oblk, oblk],
        out_shape=[jax.ShapeDtypeStruct((B, L, D), F32), jax.ShapeDtypeStruct((B, L, D), F32),
                   jax.ShapeDtypeStruct((B, L, D), BF16)],
        compiler_params=_cparams(("parallel", "parallel"), 48),
        name="hyena_pre",
    )(u, u, u, conv_w, conv_w, conv_w, conv_b, conv_b, conv_b)


def _hy_conv_kernel(vb_ref, cf_ref, sf_ref, ci_ref, si_ref, kre_ref, kim_ref, knyq_ref, o_ref):
    k = pl.program_id(2)
    vb = vb_ref[0]

    @pl.when(k == 0)
    def _():
        t = lax.broadcasted_iota(jnp.int32, vb.shape, 0)
        sgn = (1 - 2 * (t & 1)).astype(F32)
        vnyq = jnp.sum(vb.astype(F32) * sgn, axis=0, keepdims=True)
        o_ref[0] = sgn * (vnyq * knyq_ref[...])

    vre = _bdot(cf_ref[...], vb)
    vim = _bdot(sf_ref[...], vb)
    kre, kim = kre_ref[...], kim_ref[...]
    yre = vre * kre + vim * kim
    nyim = vim * kre - vre * kim
    o_ref[0] += _bdot(ci_ref[...], yre.astype(BF16)) + _bdot(si_ref[...], nyim.astype(BF16))


def hyena_long_conv(vb, cmat, smat, kre, kim, knyq, *, ct=512, fk=256):
    B, L, _ = vb.shape
    return pl.pallas_call(
        _hy_conv_kernel,
        grid=(B, D // ct, L // fk),
        in_specs=[
            pl.BlockSpec((1, L, ct), lambda b, j, k: (b, 0, j)),
            pl.BlockSpec((fk, L), lambda b, j, k: (k, 0)),
            pl.BlockSpec((fk, L), lambda b, j, k: (k, 0)),
            pl.BlockSpec((L, fk), lambda b, j, k: (0, k)),
            pl.BlockSpec((L, fk), lambda b, j, k: (0, k)),
            pl.BlockSpec((fk, ct), lambda b, j, k: (k, j)),
            pl.BlockSpec((fk, ct), lambda b, j, k: (k, j)),
            pl.BlockSpec((1, ct), lambda b, j, k: (0, j)),
        ],
        out_specs=pl.BlockSpec((1, L, ct), lambda b, j, k: (b, 0, j)),
        out_shape=jax.ShapeDtypeStruct((B, L, D), F32),
        compiler_params=_cparams(("parallel", "parallel", "arbitrary"), 52),
        name="hyena_long_conv",
    )(vb, cmat, smat, cmat, smat, kre, kim, knyq)


def _hy_out_kernel(y_ref, vg_ref, x0_ref, skip_ref, x_ref, mod_ref, w_ref, o_ref):
    a = ((y_ref[0] + vg_ref[0] * skip_ref[...]) * x0_ref[0]).astype(BF16)
    o_ref[0] = x_ref[0] + mod_ref[0, 2:3, :] * _bdot(a, w_ref[...])


def hyena_out(y, vg, x0, skip, x, mod, w, *, tm=512):
    B, L, _ = x.shape
    tok = pl.BlockSpec((1, tm, D), lambda b, i: (b, i, 0))
    return pl.pallas_call(
        _hy_out_kernel,
        grid=(B, L // tm),
        in_specs=[tok, tok, tok, _resident((1, D)), tok, pl.BlockSpec((1, 6, D), lambda b, i: (b, 0, 0)),
                  _resident((D, D))],
        out_specs=tok,
        out_shape=jax.ShapeDtypeStruct((B, L, D), F32),
        compiler_params=_cparams(("parallel", "parallel"), 40),
        name="hyena_out",
    )(y, vg, x0, skip.reshape(1, D), x, mod, w)


def dft_tables(L):
    k = jnp.arange(L, dtype=jnp.int32)
    kt = (k[:, None] * k[None, :]) % (2 * L)
    ang = kt.astype(F32) * (math.pi / L)
    return jnp.cos(ang).astype(BF16), jnp.sin(ang).astype(BF16)


def hyena_layer(x, mod, nw, p):
    (w_in, conv_w, conv_b, fw1, fb1, ff1, fw2, fb2, ff2, fw3, skip, w_out) = p
    L = x.shape[1]
    cmat, smat = dft_tables(L)
    hs, hd = hyena_filter_taps(L, fw1, fb1, ff1, fw2, fb2, ff2, fw3)
    kre, kim, knyq = hyena_filter_dft(hs, hd, cmat, smat)
    u = norm_mod_matmul(x, mod, nw, w_in.astype(BF16))
    x0, vg, vb = hyena_pre(u, conv_w, conv_b)
    y = hyena_long_conv(vb, cmat, smat, kre, kim, knyq)
    return hyena_out(y, vg, x0, skip, x, mod, w_out.astype(BF16))


def _gdn_in_kernel(x_ref, mod_ref, nw_ref, w_ref, wab_ref, wabt_ref, o_ref, ab_ref, abt_ref, h_scr, *, tn):
    h = _norm_mod(x_ref[0], nw_ref[...], mod_ref, 0, 1)
    h_scr[...] = h.astype(BF16)
    for c0 in range(0, GDN_IN, tn):
        o_ref[0, :, c0:c0 + tn] = _bdot(h_scr[...], w_ref[:, c0:c0 + tn])
    ab_ref[0] = jnp.dot(h, wab_ref[...], preferred_element_type=F32, precision=HI)
    abt_ref[0] = _dot_nt(wabt_ref[...], h, precision=HI)


def _pair_major(t):
    lead = t.shape[:-1]
    t = t.reshape(lead + (2, GDN_HK, GDN_HV // GDN_HK))
    return jnp.swapaxes(t, -1, -2).reshape(lead + (2 * GDN_HV,))


def gdn_in_proj(x, mod, nw, w_in, w_ab, *, tm=256, tn=512):
    B, L, _ = x.shape
    wab = jnp.pad(w_ab, ((0, 0), (0, LANE - w_ab.shape[1])))
    wab_rows = jnp.concatenate([_pair_major(w_ab[:, :2 * GDN_HV]), _pair_major(w_ab[:, 2 * GDN_HV:])], axis=1)
    wab_rows = jnp.pad(wab_rows, ((0, 0), (0, LANE - wab_rows.shape[1]))).T
    return pl.pallas_call(
        functools.partial(_gdn_in_kernel, tn=tn),
        grid=(B, L // tm),
        in_specs=[
            pl.BlockSpec((1, tm, D), lambda b, i: (b, i, 0)),
            pl.BlockSpec((1, 6, D), lambda b, i: (b, 0, 0)),
            _resident((1, D)),
            _resident((D, GDN_IN)),
            _resident((D, LANE)),
            _resident((LANE, D)),
        ],
        out_specs=[
            pl.BlockSpec((1, tm, GDN_IN), lambda b, i: (b, i, 0)),
            pl.BlockSpec((1, tm, LANE), lambda b, i: (b, i, 0)),
            pl.BlockSpec((1, LANE, tm), lambda b, i: (b, 0, i)),
        ],
        out_shape=[jax.ShapeDtypeStruct((B, L, GDN_IN), F32), jax.ShapeDtypeStruct((B, L, LANE), F32),
                   jax.ShapeDtypeStruct((B, LANE, L), F32)],
        scratch_shapes=[pltpu.VMEM((tm, D), BF16)],
        compiler_params=_cparams(("parallel", "parallel"), 48),
        name="gdn_in_proj",
    )(x, mod, nw.reshape(1, D), w_in.astype(BF16), wab, wab_rows)


def _gdn_pre_kernel(u_ref, w_ref, b_ref, o_ref):
    L = u_ref.shape[1]
    j = pl.program_id(1)
    t = lax.broadcasted_iota(jnp.int32, (L, 1), 0)
    u = u_ref[0]
    prev = jnp.where(t == 0, 0.0, pltpu.roll(u, 1, 0))
    nxt = jnp.where(t == L - 1, 0.0, pltpu.roll(u, L - 1, 0))
    y = b_ref[...] + w_ref[0:1, :] * prev + w_ref[1:2, :] * u + w_ref[2:3, :] * nxt
    y = y * jax.nn.sigmoid(y)
    nh = y.shape[1] // GDN_DK
    nq_tiles = GDN_HK * GDN_DK // y.shape[1]

    @pl.when(j >= 2 * nq_tiles)
    def _():
        o_ref[0] = y.astype(BF16)

    @pl.when(j < 2 * nq_tiles)
    def _():
        scale = jnp.where(j < nq_tiles, GDN_DK ** -0.5, 1.0)
        for h in range(nh):
            yh = y[:, h * GDN_DK:(h + 1) * GDN_DK]
            o_ref[0, :, h * GDN_DK:(h + 1) * GDN_DK] = (yh * (
                lax.rsqrt(jnp.sum(yh * yh, axis=-1, keepdims=True) + EPS) * scale)).astype(BF16)


def gdn_pre(proj, conv_w, conv_b, *, ct=256):
    B, L, _ = proj.shape
    return pl.pallas_call(
        _gdn_pre_kernel,
        grid=(B, GDN_QKV // ct),
        in_specs=[
            pl.BlockSpec((1, L, ct), lambda b, j: (b, 0, j)),
            pl.BlockSpec((3, ct), lambda b, j: (0, j)),
            pl.BlockSpec((1, ct), lambda b, j: (0, j)),
        ],
        out_specs=pl.BlockSpec((1, L, ct), lambda b, j: (b, 0, j)),
        out_shape=jax.ShapeDtypeStruct((B, L, GDN_QKV), BF16),
        compiler_params=_cparams(("parallel", "parallel"), 40),
        name="gdn_pre",
    )(proj, conv_w, conv_b.reshape(1, GDN_QKV))


def _softplus(x):
    return jnp.maximum(x, 0.0) + jnp.log1p(jnp.exp(-jnp.abs(x)))


def _gdn_gates_kernel(ab_ref, abt_ref, alog_c_ref, dtb_c_ref, alog_r_ref, dtb_r_ref, col_ref, row_ref, *, tl):
    C = GDN_CHUNK
    H = GDN_HV
    i = lax.broadcasted_iota(jnp.int32, (C, C), 0)
    j = lax.broadcasted_iota(jnp.int32, (C, C), 1)
    lower = (i >= j).astype(F32)
    upper = (i <= j).astype(F32)
    P = GDN_HK
    ab = ab_ref[0]
    g_c = -jnp.exp(alog_c_ref[...]) * _softplus(ab[:, :2 * H] + dtb_c_ref[...])
    logbeta_c = -_softplus(-ab[:, 2 * H:4 * H])
    abt = abt_ref[0]
    g_r = -jnp.exp(alog_r_ref[...]) * _softplus(abt[:2 * H, :] + dtb_r_ref[...])
    beta_r = jax.nn.sigmoid(abt[2 * H:4 * H, :])
    zc = jnp.zeros((C, LANE - 4 * H), F32)

    def pack(x, swap):
        return jnp.concatenate([x[P:], x[:P]] if swap else [x[:P], x[P:]], axis=1)

    for c in range(tl // C):
        gch = g_c[c * C:(c + 1) * C, :]
        pre = jnp.dot(lower, gch, preferred_element_type=F32, precision=HI)
        suf = jnp.dot(upper, gch, preferred_element_type=F32, precision=HI)
        tot = jnp.sum(gch, axis=0, keepdims=True)
        grc = g_r[:, c * C:(c + 1) * C]
        pre_r = jnp.dot(grc, upper, preferred_element_type=F32, precision=HI)
        suf_r = jnp.dot(grc, lower, preferred_element_type=F32, precision=HI)
        tot_r = jnp.sum(grc, axis=1, keepdims=True)
        for d in range(2):
            hs = slice(d * H, (d + 1) * H)
            gc = (pre if d == 0 else suf)[:, hs]
            lb = logbeta_c[c * C:(c + 1) * C, hs]
            col_ref[0, d, c * C:(c + 1) * C, :] = jnp.concatenate(
                [gc, gc + lb, jnp.exp(gc), jnp.exp(tot[:, hs] - gc), zc], axis=1)
            gr = (pre_r if d == 0 else suf_r)[hs, :]
            br = beta_r[hs, c * C:(c + 1) * C]
            row_ref[0, d, c] = jnp.concatenate(
                [pack(gr, False), pack(br, True), pack(br * jnp.exp(gr), True),
                 jnp.broadcast_to(jnp.exp(tot_r[hs, :]), (H, LANE))], axis=0)


GDN_ROWS = 3 * GDN_HK + GDN_HV


def gdn_gates(ab, abt, a_log, dt_bias, *, tl=512):
    B, L, _ = ab.shape
    N = L // GDN_CHUNK
    H2 = 2 * GDN_HV
    return pl.pallas_call(
        functools.partial(_gdn_gates_kernel, tl=tl),
        grid=(B, L // tl),
        in_specs=[
            pl.BlockSpec((1, tl, LANE), lambda b, i: (b, i, 0)),
            pl.BlockSpec((1, LANE, tl), lambda b, i: (b, 0, i)),
            _resident((1, H2)), _resident((1, H2)), _resident((H2, 1)), _resident((H2, 1)),
        ],
        out_specs=[
            pl.BlockSpec((1, 2, tl, LANE), lambda b, i: (b, 0, i, 0)),
            pl.BlockSpec((1, 2, tl // GDN_CHUNK, GDN_ROWS, LANE), lambda b, i: (b, 0, i, 0, 0)),
        ],
        out_shape=[jax.ShapeDtypeStruct((B, 2, L, LANE), F32),
                   jax.ShapeDtypeStruct((B, 2, N, GDN_ROWS, LANE), F32)],
        compiler_params=_cparams(("parallel", "parallel"), 32),
        name="gdn_gates",
    )(ab, abt, a_log.reshape(1, H2), dt_bias.reshape(1, H2),
      _pair_major(a_log.reshape(H2)).reshape(H2, 1), _pair_major(dt_bias.reshape(H2)).reshape(H2, 1))


def _gdn_chunk_kernel(q_ref, k_ref, v_ref, col_ref, row_ref, o_ref, s_scr):
    C = GDN_CHUNK
    H = GDN_HV
    d = pl.program_id(1)
    n = pl.program_id(2)

    @pl.when(n == 0)
    def _():
        s_scr[...] = jnp.zeros_like(s_scr)

    P = GDN_HK
    W = 2 * GDN_DV
    sgn = 1 - 2 * d
    i = lax.broadcasted_iota(jnp.int32, (C, LANE), 0)
    lane = lax.broadcasted_iota(jnp.int32, (C, LANE), 1)
    order = (i - (lane & (C - 1))) * sgn
    incl = order >= 0
    strict = order > 0
    left = lane < C
    eye2 = (order == 0).astype(F32)
    col = col_ref[0, 0]
    row = row_ref[0, 0, 0]
    zf = jnp.zeros((C, LANE), F32)
    zb = jnp.zeros((C, LANE), BF16)
    zs = jnp.zeros((GDN_DK, GDN_DV), BF16)

    def col_pair(base, p):
        return jnp.where(left, col[:, base + 2 * p:base + 2 * p + 1], col[:, base + 2 * p + 1:base + 2 * p + 2])

    def col_wide(base, p):
        return jnp.concatenate(
            [jnp.broadcast_to(col[:, base + 2 * p + e:base + 2 * p + e + 1], (C, GDN_DV)) for e in range(2)], axis=1)

    def block_diag(a, b, z):
        return jnp.concatenate([jnp.concatenate([a, z], axis=1), jnp.concatenate([z, b], axis=1)], axis=0)

    def anti_diag(a, b, z):
        return jnp.concatenate([jnp.concatenate([z, a], axis=1), jnp.concatenate([b, z], axis=1)], axis=0)

    ks =[k_ref[0, :, p * GDN_DK:(p + 1) * GDN_DK] for p in range(P)]
    qs = [q_ref[0, :, p * GDN_DK:(p + 1) * GDN_DK] for p in range(P)]
    grams = [_dot_nt(jnp.concatenate([ks[p], qs[p]], axis=0), jnp.concatenate([ks[p], ks[p]], axis=0))
             for p in range(P)]
    ms, intras = [], []
    for p in range(P):
        gc_j = row[p:p + 1, :]
        decay = jnp.exp(jnp.where(incl, col_pair(0, p) - gc_j, NEG))
        a_coef = jnp.exp(jnp.where(strict, col_pair(H, p) - gc_j, NEG))
        ms.append(-(grams[p][:C] * a_coef))
        intras.append((grams[p][C:] * decay).astype(BF16))
    tops = [jnp.where(left, m, eye2) for m in ms]
    bots = [jnp.where(left, eye2, m) for m in ms]
    for _ in range(6):
        for p in range(P):
            m_hi, m_lo = _split_bf16(ms[p])
            rhs_hi, rhs_lo = _split_bf16(block_diag(tops[p], bots[p], zf))
            r2 = _bdot(jnp.concatenate([m_hi, m_lo], axis=0), rhs_hi)
            r = r2[:C] + r2[C:] + _bdot(m_hi, rhs_lo)
            r0, r1 = r[:, :LANE], r[:, LANE:]
            tops[p] = jnp.where(left, r0, tops[p] + r0)
            bots[p] = jnp.where(left, bots[p] + r1, r1)
            ms[p] = jnp.where(left, r0, r1)
    us, ws = [], []
    for p in range(P):
        t = jnp.where(left, bots[p], tops[p])
        v0 = v_ref[0, :, 2 * p * GDN_DV:(2 * p + 1) * GDN_DV]
        v1 = v_ref[0, :, (2 * p + 1) * GDN_DV:(2 * p + 2) * GDN_DV]
        us.append(_bdot((t * row[P + p:P + p + 1, :]).astype(BF16), anti_diag(v1, v0, zb)))
        ws.append(_bdot((t * row[2 * P + p:2 * P + p + 1, :]).astype(BF16), anti_diag(ks[p], ks[p], zb)))
    for p in range(P):
        s = s_scr[p]
        sb = s.astype(BF16)
        lhs = jnp.concatenate([ws[p].astype(BF16), jnp.concatenate([qs[p], qs[p]], axis=1)], axis=0)
        ws_qs = _bdot(lhs, block_diag(sb[:, :GDN_DV], sb[:, GDN_DV:], zs))
        v_new = us[p] - ws_qs[:C]
        vb = v_new.astype(BF16)
        o_ref[0, 0, :, p * W:(p + 1) * W] = col_wide(2 * H, p) * ws_qs[C:] + _bdot(
            intras[p], block_diag(vb[:, :GDN_DV], vb[:, GDN_DV:], zb))
        g_end = jnp.concatenate([row[3 * P + p:3 * P + p + 1, :], row[4 * P + p:4 * P + p + 1, :]], axis=1)
        s_scr[p] = s * g_end + _dot_tn(ks[p], (v_new * col_wide(3 * H, p)).astype(BF16))


def gdn_chunk_scan(qkv, col, row):
    B, L, _ = qkv.shape
    C = GDN_CHUNK
    N = L // C
    nq = GDN_HK * GDN_DK

    def cidx(d, n):
        return n + d * (N - 1 - 2 * n)

    return pl.pallas_call(
        _gdn_chunk_kernel,
        grid=(B, 2, N),
        in_specs=[
            pl.BlockSpec((1, C, nq), lambda b, d, n: (b, cidx(d, n), 0)),
            pl.BlockSpec((1, C, nq), lambda b, d, n: (b, cidx(d, n), 1)),
            pl.BlockSpec((1, C, GDN_HV * GDN_DV), lambda b, d, n: (b, cidx(d, n), 1)),
            pl.BlockSpec((1, 1, C, LANE), lambda b, d, n: (b, d, cidx(d, n), 0)),
            pl.BlockSpec((1, 1, 1, GDN_ROWS, LANE), lambda b, d, n: (b, d, cidx(d, n), 0, 0)),
        ],
        out_specs=pl.BlockSpec((1, 1, C, GDN_HV * GDN_DV), lambda b, d, n: (b, d, cidx(d, n), 0)),
        out_shape=jax.ShapeDtypeStruct((B, 2, L, GDN_HV * GDN_DV), F32),
        scratch_shapes=[pltpu.VMEM((GDN_HK, GDN_DK, 2 * GDN_DV), F32)],
        compiler_params=_cparams(("parallel", "parallel", "arbitrary"), 32),
        name="gdn_chunk_scan",
    )(qkv, qkv, qkv, col, row)


def _gdn_out_kernel(o_ref, z_ref, nw_ref, x_ref, mod_ref, w_ref, out_ref, a_scr):
    o = o_ref[0, 0] + o_ref[0, 1]
    z = z_ref[0]
    gate = z * jax.nn.sigmoid(z)
    for h in range(GDN_HV):
        hs = slice(h * GDN_DV, (h + 1) * GDN_DV)
        a_scr[:, hs] = (_rms(o[:, hs]) * nw_ref[...] * gate[:, hs]).astype(BF16)
    out_ref[0] = x_ref[0] + mod_ref[0, 2:3, :] * _bdot(a_scr[...], w_ref[...])


def gdn_out(o2, proj, norm_w, x, mod, w, *, tm=256):
    B, L, _ = x.shape
    hd = GDN_HV * GDN_DV
    return pl.pallas_call(
        _gdn_out_kernel,
        grid=(B, L // tm),
        in_specs=[
            pl.BlockSpec((1, 2, tm, hd), lambda b, i: (b, 0, i, 0)),
            pl.BlockSpec((1, tm, hd), lambda b, i: (b, i, GDN_QKV // hd)),
            _resident((1, GDN_DV)),
            pl.BlockSpec((1, tm, D), lambda b, i: (b, i, 0)),
            pl.BlockSpec((1, 6, D), lambda b, i: (b, 0, 0)),
            _resident((hd, D)),
        ],
        out_specs=pl.BlockSpec((1, tm, D), lambda b, i: (b, i, 0)),
        out_shape=jax.ShapeDtypeStruct((B, L, D), F32),
        scratch_shapes=[pltpu.VMEM((tm, hd), BF16)],
        compiler_params=_cparams(("parallel", "parallel"), 40),
        name="gdn_out",
    )(o2, proj, norm_w.reshape(1, GDN_DV), x, mod, w)


def gdn_layer(x, mod, nw, p):
    (w_in, conv_w, conv_b, w_ab, a_log, dt_bias, norm_w, w_out) = p
    proj, ab, abt = gdn_in_proj(x, mod, nw, w_in, w_ab)
    qkv = gdn_pre(proj, conv_w, conv_b)
    col, row = gdn_gates(ab, abt, a_log, dt_bias)
    o2 = gdn_chunk_scan(qkv, col, row)
    return gdn_out(o2, proj, norm_w, x, mod, w_out.astype(BF16))


def _swa_kernel(q_ref, kp_ref, kc_ref, kn_ref, vp_ref, vc_ref, vn_ref, sink_ref, o_ref, s_scr, p_scr, *, nb):
    W = SWA_BLOCK
    KW = 3 * W
    nk = SWA_HKV * SWA_DH
    n = pl.program_id(1)
    qi = lax.broadcasted_iota(jnp.int32, (W, KW), 0)
    kj = lax.broadcasted_iota(jnp.int32, (W, KW), 1)
    dist = jnp.abs(kj - W - qi)
    valid = (dist <= SWA_WINDOW) & ((kj >= W) | (n > 0)) & ((kj < 2 * W) | (n < nb - 1))
    distm = jnp.where(valid, dist.astype(F32), -NEG)
    head_of_lane = lax.broadcasted_iota(jnp.int32, (KW, nk), 1) // SWA_DH
    kb = jnp.concatenate([kp_ref[0], kc_ref[0], kn_ref[0]], axis=0).astype(BF16)
    vb = jnp.concatenate([vp_ref[0], vc_ref[0], vn_ref[0]], axis=0).astype(BF16)
    kbd = jnp.concatenate([jnp.where(head_of_lane == h, kb, 0) for h in range(SWA_HKV)], axis=0)
    vbd = jnp.concatenate([jnp.where(head_of_lane == h, vb, 0) for h in range(SWA_HKV)], axis=0)
    q = jnp.concatenate([q_ref[0, :, g * nk:(g + 1) * nk] for g in range(SWA_G)], axis=0)
    s_all = _dot_nt((q * SWA_DH ** -0.5).astype(BF16), kbd)
    group_of_row = lax.broadcasted_iota(jnp.int32, (SWA_G * W, 1), 0) // W
    distm4 = jnp.concatenate([distm] * SWA_G, axis=0)
    sinks, ms, rs = [], [], []
    for h in range(SWA_HKV):
        slope = jnp.zeros((SWA_G * W, 1), F32)
        sink = jnp.zeros((SWA_G * W, 1), F32)
        for g in range(SWA_G):
            hq = h * SWA_G + g
            slope = jnp.where(group_of_row == g, 2.0 ** (-8.0 * (hq + 1) / SWA_HQ), slope)
            sink = jnp.where(group_of_row == g, sink_ref[0:1, hq:hq + 1], sink)
        s = s_all[:, h * KW:(h + 1) * KW] - slope * distm4
        s_scr[:, h * KW:(h + 1) * KW] = s
        sinks.append(sink)
        ms.append(jnp.maximum(jnp.max(s, axis=-1, keepdims=True), sink))
    for h in range(SWA_HKV):
        e = jnp.exp(s_scr[:, h * KW:(h + 1) * KW] - ms[h])
        p_scr[:, h * KW:(h + 1) * KW] = e.astype(BF16)
        rs.append(1.0 / (jnp.sum(e, axis=-1, keepdims=True) + jnp.exp(sinks[h] - ms[h])))
    o = _bdot(p_scr[...], vbd)
    head_of_out = lax.broadcasted_iota(jnp.int32, (SWA_G * W, nk), 1) // SWA_DH
    r_all = jnp.broadcast_to(rs[0], (SWA_G * W, nk))
    for h in range(1, SWA_HKV):
        r_all = jnp.where(head_of_out == h, rs[h], r_all)
    o = o * r_all
    for g in range(SWA_G):
        o_ref[0, :, g * nk:(g + 1) * nk] = o[g * W:(g + 1) * W]


def swa_attention(qkv, sink):
    B, L, _ = qkv.shape
    W = SWA_BLOCK
    nb = L // W
    nq = SWA_HQ * SWA_DH
    nk = SWA_HKV * SWA_DH
    kcol, vcol = nq // nk, nq // nk + 1

    def band(col):
        return [pl.BlockSpec((1, W, nk), lambda b, n: (b, jnp.maximum(n - 1, 0), col)),
                pl.BlockSpec((1, W, nk), lambda b, n: (b, n, col)),
                pl.BlockSpec((1, W, nk), lambda b, n: (b, jnp.minimum(n + 1, nb - 1), col))]

    return pl.pallas_call(
        functools.partial(_swa_kernel, nb=nb),
        grid=(B, nb),
        in_specs=[pl.BlockSpec((1, W, nq), lambda b, n: (b, n, 0))] + band(kcol) + band(vcol) + [
            _resident((1, SWA_HQ))],
        out_specs=pl.BlockSpec((1, W, nq), lambda b, n: (b, n, 0)),
        out_shape=jax.ShapeDtypeStruct((B, L, nq), F32),
        scratch_shapes=[pltpu.VMEM((SWA_G * W, SWA_HKV * 3 * W), F32),
                        pltpu.VMEM((SWA_G * W, SWA_HKV * 3 * W), BF16)],
        compiler_params=_cparams(("parallel", "parallel"), 32),
        name="swa_attention",
    )(qkv, qkv, qkv, qkv, qkv, qkv, qkv, sink.reshape(1, SWA_HQ))


def swa_layer(x, mod, nw, p):
    (w_qkv, sink, w_out) = p
    nq = SWA_HQ * SWA_DH
    wq = w_qkv[:, :nq].reshape(D, SWA_HKV, SWA_G, SWA_DH).swapaxes(1, 2).reshape(D, nq)
    wo = w_out.reshape(SWA_HKV, SWA_G, SWA_DH, D).swapaxes(0, 1).reshape(nq, D)
    qkv = norm_mod_matmul(x, mod, nw, jnp.concatenate([wq, w_qkv[:, nq:]], axis=1).astype(BF16))
    o = swa_attention(qkv, sink)
    return matmul_gated_residual(o, x, mod, wo.astype(BF16))


def _mla_proj_kernel(d_ref, cos_ref, sin_ref, qnw_ref, kvnw_ref, wq_ref, wqr_ref, wk_ref, wv_ref,
                     q_ref, k_ref, v_ref):
    dd = d_ref[0]
    cq = (_rms(dd[:, :MLA_QRANK]) * qnw_ref[...]).astype(BF16)
    ckv = (_rms(dd[:, MLA_QRANK:MLA_QRANK + MLA_KVRANK]) * kvnw_ref[...]).astype(BF16)
    cs, sn = cos_ref[...], sin_ref[...]
    base = MLA_QRANK + MLA_KVRANK
    k_rope = dd[:, base:base + MLA_HP] * cs + dd[:, base + MLA_HP:base + 2 * MLA_HP] * sn
    qa = _bdot(cq, wq_ref[...])
    qb = _bdot(cq, wqr_ref[...])
    kn = _bdot(ckv, wk_ref[...])
    scale = (MLA_NOPE + MLA_ROPE) ** -0.5
    for h in range(MLA_H):
        hs = slice(h * MLA_HP, (h + 1) * MLA_HP)
        q_ref[0, :, hs] = ((qa[:, hs] * cs + qb[:, hs] * sn) * scale).astype(BF16)
        k_ref[0, :, hs] = (kn[:, hs] + k_rope).astype(BF16)
    v_ref[0] = _bdot(ckv, wv_ref[...]).astype(BF16)


def mla_project(dlat, cos_t, sin_t, q_norm_w, kv_norm_w, wq, wqr, wk, wv, *, tm=256):
    B, L, N = dlat.shape
    hp = MLA_H * MLA_HP
    return pl.pallas_call(
        _mla_proj_kernel,
        grid=(B, L // tm),
        in_specs=[
            pl.BlockSpec((1, tm, N), lambda b, i: (b, i, 0)),
            pl.BlockSpec((tm, MLA_HP), lambda b, i: (i, 0)),
            pl.BlockSpec((tm, MLA_HP), lambda b, i: (i, 0)),
            _resident((1, MLA_QRANK)), _resident((1, MLA_KVRANK)),
            _resident(wq.shape), _resident(wqr.shape), _resident(wk.shape), _resident(wv.shape),
        ],
        out_specs=[
            pl.BlockSpec((1, tm, hp), lambda b, i: (b, i, 0)),
            pl.BlockSpec((1, tm, hp), lambda b, i: (b, i, 0)),
            pl.BlockSpec((1, tm, MLA_H * MLA_DV), lambda b, i: (b, i, 0)),
        ],
        out_shape=[jax.ShapeDtypeStruct((B, L, hp), BF16), jax.ShapeDtypeStruct((B, L, hp), BF16),
                   jax.ShapeDtypeStruct((B, L, MLA_H * MLA_DV), BF16)],
        compiler_params=_cparams(("parallel", "parallel"), 40),
        name="mla_project",
    )(dlat, cos_t, sin_t, q_norm_w.reshape(1, -1), kv_norm_w.reshape(1, -1), wq, wqr, wk, wv)


def _mla_attn_kernel(q_ref, k_ref, v_ref, o_ref):
    outs = []
    for h in range(2):
        q = q_ref[0, :, h * MLA_HP:(h + 1) * MLA_HP]
        k = k_ref[0, :, h * MLA_HP:(h + 1) * MLA_HP]
        v = v_ref[0, :, h * MLA_DV:(h + 1) * MLA_DV]
        s = _dot_nt(q, k)
        e = jnp.exp(s - jnp.max(s, axis=-1, keepdims=True))
        outs.append(_bdot(e.astype(BF16), v) / jnp.sum(e, axis=-1, keepdims=True))
    o_ref[0] = jnp.concatenate(outs, axis=1)


def mla_attention(q, k, v, *, tq=256):
    B, L, _ = q.shape
    return pl.pallas_call(
        _mla_attn_kernel,
        grid=(B, MLA_H // 2, L // tq),
        in_specs=[
            pl.BlockSpec((1, tq, 2 * MLA_HP), lambda b, h, i: (b, i, h)),
            pl.BlockSpec((1, L, 2 * MLA_HP), lambda b, h, i: (b, 0, h)),
            pl.BlockSpec((1, L, 2 * MLA_DV), lambda b, h, i: (b, 0, h)),
        ],
        out_specs=pl.BlockSpec((1, tq, 2 * MLA_DV), lambda b, h, i: (b, i, h)),
        out_shape=jax.ShapeDtypeStruct((B, L, MLA_H * MLA_DV), F32),
        compiler_params=_cparams(("parallel", "parallel", "parallel"), 40),
        name="mla_attention",
    )(q, k, v)


def _rot_half_cols(w):
    half = MLA_ROPE // 2
    return jnp.concatenate([-w[..., half:], w[..., :half]], axis=-1)


def mla_layer(x, mod, nw, p):
    (w_down, q_norm_w, w_uq, kv_norm_w, w_ukv, w_out) = p
    L = x.shape[1]
    pad_r = MLA_HP - MLA_NOPE - MLA_ROPE
    base = MLA_QRANK + MLA_KVRANK
    w_rope = w_down[:, base:]
    zl = jnp.zeros((D, MLA_NOPE), F32)
    zr = jnp.zeros((D, pad_r), F32)
    w_dext = jnp.concatenate([w_down[:, :base], zl, w_rope, zr, zl, _rot_half_cols(w_rope), zr], axis=1)
    wq3 = w_uq.reshape(MLA_QRANK, MLA_H, MLA_NOPE + MLA_ROPE)
    zq = jnp.zeros((MLA_QRANK, MLA_H, pad_r), F32)
    wq = jnp.concatenate([wq3, zq], axis=-1).reshape(MLA_QRANK, -1).astype(BF16)
    wqr = jnp.concatenate([jnp.zeros((MLA_QRANK, MLA_H, MLA_NOPE), F32), _rot_half_cols(wq3[..., MLA_NOPE:]), zq],
                          axis=-1).reshape(MLA_QRANK, -1).astype(BF16)
    wkv3 = w_ukv.reshape(MLA_KVRANK, MLA_H, MLA_NOPE + MLA_DV)
    wk = jnp.concatenate([wkv3[..., :MLA_NOPE], jnp.zeros((MLA_KVRANK, MLA_H, MLA_HP - MLA_NOPE), F32)],
                         axis=-1).reshape(MLA_KVRANK, -1).astype(BF16)
    wv = wkv3[..., MLA_NOPE:].reshape(MLA_KVRANK, -1).astype(BF16)
    inv = ROPE_THETA ** (-jnp.arange(0, MLA_ROPE, 2, dtype=F32) / MLA_ROPE)
    ang = jnp.arange(L, dtype=F32)[:, None] * inv[None, :]
    cos, sin = jnp.cos(ang), jnp.sin(ang)
    cos_t = jnp.concatenate([jnp.ones((L, MLA_NOPE), F32), cos, cos, jnp.zeros((L, pad_r), F32)], axis=1)
    sin_t = jnp.concatenate([jnp.zeros((L, MLA_NOPE), F32), sin, sin, jnp.zeros((L, pad_r), F32)], axis=1)

    dlat = norm_mod_matmul(x, mod, nw, w_dext.astype(BF16))
    q, k, v = mla_project(dlat, cos_t, sin_t, q_norm_w, kv_norm_w, wq, wqr, wk, wv)
    o = mla_attention(q, k, v)
    return matmul_gated_residual(o, x, mod, w_out.astype(BF16))


def encoder_trunk(x, c, ada_w, ada_b, norm_w, hy, gdn, swa, mla, ffn_w_gu, ffn_w_down, final_norm_w):
    mods = ada_modulation(c, ada_w, ada_b)
    layers = (hyena_layer, gdn_layer, swa_layer, mla_layer)
    params = (hy, gdn, swa, mla)
    for i in range(DEPTH):
        kind, j = i % 4, i // 4
        x = layers[kind](x, mods[i], norm_w[i, 0], [p[j] for p in params[kind]])
        x = ffn_block(x, mods[i], norm_w[i, 1], ffn_w_gu[i].astype(BF16), ffn_w_down[i].astype(BF16))
    return final_norm(x, final_norm_w)


def kernel(x_prompt, x_sample, c_prompt, c_sample, ada_w, ada_b, norm_w, hy_w_in, hy_conv_w, hy_conv_b, hy_filt_w1, hy_filt_b1, hy_filt_freq1, hy_filt_w2, hy_filt_b2, hy_filt_freq2, hy_filt_w3, hy_skip, hy_w_out, gdn_w_in, gdn_conv_w, gdn_conv_b, gdn_w_ab, gdn_a_log, gdn_dt_bias, gdn_norm_w, gdn_w_out, swa_w_qkv, swa_sink, swa_w_out, mla_w_down, mla_q_norm_w, mla_w_uq, mla_kv_norm_w, mla_w_ukv, mla_w_out, ffn_w_gu, ffn_w_down, final_norm_w):
    hy = (hy_w_in, hy_conv_w, hy_conv_b, hy_filt_w1, hy_filt_b1, hy_filt_freq1,
          hy_filt_w2, hy_filt_b2, hy_filt_freq2, hy_filt_w3, hy_skip, hy_w_out)
    gdn = (gdn_w_in, gdn_conv_w, gdn_conv_b, gdn_w_ab, gdn_a_log, gdn_dt_bias, gdn_norm_w, gdn_w_out)
    swa = (swa_w_qkv, swa_sink, swa_w_out)
    mla = (mla_w_down, mla_q_norm_w, mla_w_uq, mla_kv_norm_w, mla_w_ukv, mla_w_out)
    args = (ada_w, ada_b, norm_w, hy, gdn, swa, mla, ffn_w_gu, ffn_w_down, final_norm_w)
    return (encoder_trunk(x_prompt, c_prompt, *args), encoder_trunk(x_sample, c_sample, *args))
```

```python
import functools
import math

import jax
import jax.numpy as jnp
from jax import lax
from jax.experimental import pallas as pl
from jax.experimental.pallas import tpu as pltpu

F32 = jnp.float32
BF16 = jnp.bfloat16
HI = lax.Precision.HIGHEST

D = 1024
DEPTH = 4
EPS = 1e-6
D_FF = 2816

HY_BANDS = 16
HY_FILT = 64
HY_TARGET = 1e-2
HY_FAST = 0.3
HY_SLOW = 1.5

GDN_HK = 8
GDN_HV = 16
GDN_DK = 128
GDN_DV = 128
GDN_CHUNK = 64
GDN_QKV = 2 * GDN_HK * GDN_DK + GDN_HV * GDN_DV
GDN_IN = GDN_QKV + GDN_HV * GDN_DV

SWA_HQ = 16
SWA_HKV = 4
SWA_G = SWA_HQ // SWA_HKV
SWA_DH = 64
SWA_WINDOW = 128
SWA_BLOCK = 128

MLA_H = 16
MLA_NOPE = 64
MLA_ROPE = 32
MLA_DV = 64
MLA_QRANK = 256
MLA_KVRANK = 256
MLA_HP = 128
ROPE_THETA = 10000.0

LANE = 128
LOG2E = math.log2(math.e)
MIB = 2 ** 20
NEG = -1e30


def _cparams(sem, vmem_mb):
    return pltpu.CompilerParams(dimension_semantics=sem, vmem_limit_bytes=vmem_mb * MIB)


def _resident(shape):
    nd = len(shape)
    return pl.BlockSpec(shape, lambda *_: (0,) * nd, pipeline_mode=pl.Buffered(1))


def _bdot(a, b):
    return jnp.dot(a, b, preferred_element_type=F32)


def _dot_nt(a, b, precision=None):
    return lax.dot_general(a, b, (((1,), (1,)), ((), ())), preferred_element_type=F32, precision=precision)


def _dot_tn(a, b, precision=None):
    return lax.dot_general(a, b, (((0,), (0,)), ((), ())), preferred_element_type=F32, precision=precision)


def _split_bf16(x):
    hi = x.astype(BF16)
    return hi, (x - hi.astype(F32)).astype(BF16)


def _rms(x):
    return x * lax.rsqrt(jnp.mean(x * x, axis=-1, keepdims=True) + EPS)


def _norm_mod(x, nw, mod_ref, sh_row, sc_row):
    return _rms(x) * nw * (1.0 + mod_ref[0, sc_row:sc_row + 1, :]) + mod_ref[0, sh_row:sh_row + 1, :]


def _ada_kernel(c_ref, w_ref, b_ref, o_ref):
    c = c_ref[...]
    ca = c * jax.nn.sigmoid(c)
    o_ref[0] = jnp.dot(ca, w_ref[0], preferred_element_type=F32, precision=HI) + b_ref[0]


def ada_modulation(c, ada_w, ada_b):
    B = c.shape[0]
    tn = 1024
    out = pl.pallas_call(
        _ada_kernel,
        grid=(DEPTH, 6 * D // tn),
        in_specs=[
            pl.BlockSpec((B, D), lambda i, j: (0, 0)),
            pl.BlockSpec((1, D, tn), lambda i, j: (i, 0, j)),
            pl.BlockSpec((1, 1, tn), lambda i, j: (i, 0, j)),
        ],
        out_specs=pl.BlockSpec((1, B, tn), lambda i, j: (i, 0, j)),
        out_shape=jax.ShapeDtypeStruct((DEPTH, B, 6 * D), F32),
        compiler_params=_cparams(("parallel", "parallel"), 32),
        name="ada_modulation",
    )(c, ada_w, ada_b.reshape(DEPTH, 1, 6 * D))
    return out.reshape(DEPTH, B, 6, D)


def _nmm_kernel(x_ref, mod_ref, nw_ref, w_ref, o_ref, h_scr, *, sh_row, sc_row, tn):
    h_scr[...] = _norm_mod(x_ref[0], nw_ref[...], mod_ref, sh_row, sc_row).astype(BF16)
    n = w_ref.shape[1]
    for c0 in range(0, n, tn):
        c1 = min(c0 + tn, n)
        o_ref[0, :, c0:c1] = _bdot(h_scr[...], w_ref[:, c0:c1]).astype(o_ref.dtype)


def norm_mod_matmul(x, mod, nw, w, *, sh_row=0, sc_row=1, tm=512, tn=512, out_dtype=F32):
    B, L, _ = x.shape
    N = w.shape[1]
    return pl.pallas_call(
        functools.partial(_nmm_kernel, sh_row=sh_row, sc_row=sc_row, tn=tn),
        grid=(B, L // tm),
        in_specs=[
            pl.BlockSpec((1, tm, D), lambda b, i: (b, i, 0)),
            pl.BlockSpec((1, 6, D), lambda b, i: (b, 0, 0)),
            _resident((1, D)),
            _resident((D, N)),
        ],
        out_specs=pl.BlockSpec((1, tm, N), lambda b, i: (b, i, 0)),
        out_shape=jax.ShapeDtypeStruct((B, L, N), out_dtype),
        scratch_shapes=[pltpu.VMEM((tm, D), BF16)],
        compiler_params=_cparams(("parallel", "parallel"), 48),
        name="norm_mod_matmul",
    )(x, mod, nw.reshape(1, D), w)


def _mmres_kernel(a_ref, x_ref, mod_ref, w_ref, o_ref, *, g_row):
    y = _bdot(a_ref[0].astype(BF16), w_ref[...])
    o_ref[0] = x_ref[0] + mod_ref[0, g_row:g_row + 1, :] * y


def matmul_gated_residual(a, x, mod, w, *, g_row=2, tm=512):
    B, L, K = a.shape
    return pl.pallas_call(
        functools.partial(_mmres_kernel, g_row=g_row),
        grid=(B, L // tm),
        in_specs=[
            pl.BlockSpec((1, tm, K), lambda b, i: (b, i, 0)),
            pl.BlockSpec((1, tm, D), lambda b, i: (b, i, 0)),
            pl.BlockSpec((1, 6, D), lambda b, i: (b, 0, 0)),
            _resident((K, D)),
        ],
        out_specs=pl.BlockSpec((1, tm, D), lambda b, i: (b, i, 0)),
        out_shape=jax.ShapeDtypeStruct((B, L, D), F32),
        compiler_params=_cparams(("parallel", "parallel"), 40),
        name="matmul_gated_residual",
    )(a, x, mod, w)


def _ffn_kernel(x_ref, mod_ref, nw_ref, wgu_ref, wd_ref, o_ref, h_scr, act_scr, *, tf):
    x = x_ref[0]
    h_scr[...] = _norm_mod(x, nw_ref[...], mod_ref, 3, 4).astype(BF16)
    for c0 in range(0, D_FF, tf):
        gate = _bdot(h_scr[...], wgu_ref[:, c0:c0 + tf])
        up = _bdot(h_scr[...], wgu_ref[:, D_FF + c0:D_FF + c0 + tf])
        act_scr[:, c0:c0 + tf] = (gate * jax.nn.sigmoid(gate) * up).astype(BF16)
    o_ref[0] = x + mod_ref[0, 5:6, :] * _bdot(act_scr[...], wd_ref[...])


def ffn_block(x, mod, nw, w_gu, w_down, *, tm=512, tf=256):
    B, L, _ = x.shape
    return pl.pallas_call(
        functools.partial(_ffn_kernel, tf=tf),
        grid=(B, L // tm),
        in_specs=[
            pl.BlockSpec((1, tm, D), lambda b, i: (b, i, 0)),
            pl.BlockSpec((1, 6, D), lambda b, i: (b, 0, 0)),
            _resident((1, D)),
            _resident((D, 2 * D_FF)),
            _resident((D_FF, D)),
        ],
        out_specs=pl.BlockSpec((1, tm, D), lambda b, i: (b, i, 0)),
        out_shape=jax.ShapeDtypeStruct((B, L, D), F32),
        scratch_shapes=[pltpu.VMEM((tm, D), BF16), pltpu.VMEM((tm, D_FF), BF16)],
        compiler_params=_cparams(("parallel", "parallel"), 48),
        name="ffn_block",
    )(x, mod, nw.reshape(1, D), w_gu, w_down)


def _final_norm_kernel(x_ref, nw_ref, o_ref):
    o_ref[0] = _rms(x_ref[0]) * nw_ref[...]


def final_norm(x, nw, *, tm=1024):
    B, L, _ = x.shape
    tm = min(tm, L)
    return pl.pallas_call(
        _final_norm_kernel,
        grid=(B, L // tm),
        in_specs=[pl.BlockSpec((1, tm, D), lambda b, i: (b, i, 0)), _resident((1, D))],
        out_specs=pl.BlockSpec((1, tm, D), lambda b, i: (b, i, 0)),
        out_shape=jax.ShapeDtypeStruct((B, L, D), F32),
        compiler_params=_cparams(("parallel", "parallel"), 32),
        name="final_norm",
    )(x, nw.reshape(1, D))


def _hy_filter_kernel(freq_ref, w1t_ref, w1c_ref, w1s_ref, b1_ref, f1_ref, w2_ref, b2_ref, f2_ref, w3_ref,
                      rate_ref, hs_ref, hd_ref, *, L, tl):
    pos = (pl.program_id(0) * tl + lax.broadcasted_iota(jnp.int32, (tl, 1), 0)).astype(F32)
    t = pos / max(L - 1, 1)
    ang = freq_ref[...] * (2.0 * math.pi / L) * pos
    z = (t * w1t_ref[...] + jnp.dot(jnp.cos(ang), w1c_ref[...], preferred_element_type=F32, precision=HI)
         - jnp.dot(jnp.sin(ang), w1s_ref[...], preferred_element_type=F32, precision=HI) + b1_ref[...])
    z = jnp.sin(f1_ref[...] * z)
    z = jnp.sin(f2_ref[...] * (jnp.dot(z, w2_ref[...], preferred_element_type=F32, precision=HI) + b2_ref[...]))
    z = jnp.dot(z, w3_ref[...], preferred_element_type=F32, precision=HI)
    window = jnp.exp(-t * rate_ref[...])
    hf = z[:, :D] * window
    hb = jnp.where(pos == 0.0, 0.0, z[:, D:] * window)
    hs_ref[...] = hf + hb
    hd_ref[...] = hb - hf


def hyena_filter_taps(L, fw1, fb1, ff1, fw2, fb2, ff2, fw3):
    tl = 256
    freqs = jnp.linspace(1e-4, HY_BANDS - 1, HY_BANDS, dtype=F32).reshape(1, HY_BANDS)
    rates = jnp.abs(jnp.linspace(math.log(HY_TARGET) / HY_SLOW, math.log(HY_TARGET) / HY_FAST, D, dtype=F32))
    small = [freqs, fw1[0:1], fw1[1:1 + HY_BANDS], fw1[1 + HY_BANDS:], fb1.reshape(1, -1), ff1.reshape(1, -1),
             fw2, fb2.reshape(1, -1), ff2.reshape(1, -1), fw3, rates.reshape(1, D)]
    return pl.pallas_call(
        functools.partial(_hy_filter_kernel, L=L, tl=tl),
        grid=(L // tl,),
        in_specs=[_resident(a.shape) for a in small],
        out_specs=[pl.BlockSpec((tl, D), lambda i: (i, 0))] * 2,
        out_shape=[jax.ShapeDtypeStruct((L, D), F32)] * 2,
        compiler_params=_cparams(("parallel",), 32),
        name="hyena_filter_taps",
    )(*small)


def _hy_filter_dft_kernel(hs_ref, hd_ref, cf_ref, sf_ref, kre_ref, kim_ref, knyq_ref, *, L, fk):
    k = pl.program_id(1)
    hs = hs_ref[...]
    inv_n = 1.0 / (2 * L)
    row = k * fk + lax.broadcasted_iota(jnp.int32, (fk, 1), 0)
    wk = jnp.where(row == 0, inv_n, 2.0 * inv_n)
    kre_ref[...] = wk * _bdot(cf_ref[...], hs.astype(BF16))
    kim_ref[...] = wk * _bdot(sf_ref[...], hd_ref[...].astype(BF16))

    @pl.when(k == 0)
    def _():
        t = lax.broadcasted_iota(jnp.int32, hs.shape, 0)
        sgn = (1 - 2 * (t & 1)).astype(F32)
        knyq_ref[...] = inv_n * jnp.sum(hs * sgn, axis=0, keepdims=True)


def hyena_filter_dft(hs, hd, cmat, smat, *, ct=256, fk=256):
    L = hs.shape[0]
    return pl.pallas_call(
        functools.partial(_hy_filter_dft_kernel, L=L, fk=fk),
        grid=(D // ct, L // fk),
        in_specs=[
            pl.BlockSpec((L, ct), lambda j, k: (0, j)),
            pl.BlockSpec((L, ct), lambda j, k: (0, j)),
            pl.BlockSpec((fk, L), lambda j, k: (k, 0)),
            pl.BlockSpec((fk, L), lambda j, k: (k, 0)),
        ],
        out_specs=[
            pl.BlockSpec((fk, ct), lambda j, k: (k, j)),
            pl.BlockSpec((fk, ct), lambda j, k: (k, j)),
            pl.BlockSpec((1, ct), lambda j, k: (0, j)),
        ],
        out_shape=[jax.ShapeDtypeStruct((L, D), F32), jax.ShapeDtypeStruct((L, D), F32),
                   jax.ShapeDtypeStruct((1, D), F32)],
        compiler_params=_cparams(("parallel", "arbitrary"), 48),
        name="hyena_filter_dft",
    )(hs, hd, cmat, smat)


def _hy_pre_kernel(u0_ref, u1_ref, u2_ref, w0_ref, w1_ref, w2_ref, b0_ref, b1_ref, b2_ref, x0_ref, vg_ref):
    L = u0_ref.shape[1]
    t = lax.broadcasted_iota(jnp.int32, (L, 1), 0)
    first, last = t == 0, t == L - 1

    def dwconv(u_ref, w_ref, b_ref):
        u = u_ref[0].astype(F32)
        prev = jnp.where(first, 0.0, pltpu.roll(u, 1, 0))
        nxt = jnp.where(last, 0.0, pltpu.roll(u, L - 1, 0))
        return b_ref[...] + w_ref[0:1, :] * prev + w_ref[1:2, :] * u + w_ref[2:3, :] * nxt

    x0_ref[0] = dwconv(u0_ref, w0_ref, b0_ref).astype(BF16)
    vg_ref[0] = (dwconv(u2_ref, w2_ref, b2_ref) * dwconv(u1_ref, w1_ref, b1_ref)).astype(BF16)


def hyena_pre(u, conv_w, conv_b, *, ct=128):
    B, L, _ = u.shape
    nj = D // ct
    conv_b = conv_b.reshape(1, 3 * D)
    ublk = [pl.BlockSpec((1, L, ct), lambda b, j, s=s: (b, 0, s * nj + j)) for s in range(3)]
    wblk = [pl.BlockSpec((3, ct), lambda b, j, s=s: (0, s * nj + j)) for s in range(3)]
    bblk = [pl.BlockSpec((1, ct), lambda b, j, s=s: (0, s * nj + j)) for s in range(3)]
    oblk = pl.BlockSpec((1, L, ct), lambda b, j: (b, 0, j))
    return pl.pallas_call(
        _hy_pre_kernel,
        grid=(B, nj),
        in_specs=ublk + wblk + bblk,
        out_specs=[oblk, oblk],
        out_shape=[jax.ShapeDtypeStruct((B, L, D), BF16)] * 2,
        compiler_params=_cparams(("parallel", "parallel"), 48),
        name="hyena_pre",
    )(u, u, u, conv_w, conv_w, conv_w, conv_b, conv_b, conv_b)


def _hy_conv_kernel(vb_ref, cf_ref, sf_ref, ci_ref, si_ref, kre_ref, kim_ref, knyq_ref, o_ref):
    k = pl.program_id(2)
    vb = vb_ref[0]

    @pl.when(k == 0)
    def _():
        t = lax.broadcasted_iota(jnp.int32, vb.shape, 0)
        sgn = (1 - 2 * (t & 1)).astype(F32)
        vnyq = jnp.sum(vb.astype(F32) * sgn, axis=0, keepdims=True)
        o_ref[0] = sgn * (vnyq * knyq_ref[...])

    vre = _bdot(cf_ref[...], vb)
    vim = _bdot(sf_ref[...], vb)
    kre, kim = kre_ref[...], kim_ref[...]
    yre = vre * kre + vim * kim
    nyim = vim * kre - vre * kim
    o_ref[0] += _bdot(ci_ref[...], yre.astype(BF16)) + _bdot(si_ref[...], nyim.astype(BF16))


def hyena_long_conv(vb, cmat, smat, kre, kim, knyq, *, ct=512, fk=256):
    B, L, _ = vb.shape
    return pl.pallas_call(
        _hy_conv_kernel,
        grid=(B, D // ct, L // fk),
        in_specs=[
            pl.BlockSpec((1, L, ct), lambda b, j, k: (b, 0, j)),
            pl.BlockSpec((fk, L), lambda b, j, k: (k, 0)),
            pl.BlockSpec((fk, L), lambda b, j, k: (k, 0)),
            pl.BlockSpec((L, fk), lambda b, j, k: (0, k)),
            pl.BlockSpec((L, fk), lambda b, j, k: (0, k)),
            pl.BlockSpec((fk, ct), lambda b, j, k: (k, j)),
            pl.BlockSpec((fk, ct), lambda b, j, k: (k, j)),
            pl.BlockSpec((1, ct), lambda b, j, k: (0, j)),
        ],
        out_specs=pl.BlockSpec((1, L, ct), lambda b, j, k: (b, 0, j)),
        out_shape=jax.ShapeDtypeStruct((B, L, D), F32),
        compiler_params=_cparams(("parallel", "parallel", "arbitrary"), 52),
        name="hyena_long_conv",
    )(vb, cmat, smat, cmat, smat, kre, kim, knyq)


def _hy_out_kernel(y_ref, vg_ref, x0_ref, skip_ref, x_ref, mod_ref, w_ref, o_ref):
    a = ((y_ref[0] + vg_ref[0] * skip_ref[...]) * x0_ref[0]).astype(BF16)
    o_ref[0] = x_ref[0] + mod_ref[0, 2:3, :] * _bdot(a, w_ref[...])


def hyena_out(y, vg, x0, skip, x, mod, w, *, tm=512):
    B, L, _ = x.shape
    tok = pl.BlockSpec((1, tm, D), lambda b, i: (b, i, 0))
    return pl.pallas_call(
        _hy_out_kernel,
        grid=(B, L // tm),
        in_specs=[tok, tok, tok, _resident((1, D)), tok, pl.BlockSpec((1, 6, D), lambda b, i: (b, 0, 0)),
                  _resident((D, D))],
        out_specs=tok,
        out_shape=jax.ShapeDtypeStruct((B, L, D), F32),
        compiler_params=_cparams(("parallel", "parallel"), 40),
        name="hyena_out",
    )(y, vg, x0, skip.reshape(1, D), x, mod, w)


def dft_tables(L):
    k = jnp.arange(L, dtype=jnp.int32)
    kt = (k[:, None] * k[None, :]) % (2 * L)
    ang = kt.astype(F32) * (math.pi / L)
    return jnp.cos(ang).astype(BF16), jnp.sin(ang).astype(BF16)


def hyena_layer(x, mod, nw, p):
    (w_in, conv_w, conv_b, fw1, fb1, ff1, fw2, fb2, ff2, fw3, skip, w_out) = p
    L = x.shape[1]
    cmat, smat = dft_tables(L)
    hs, hd = hyena_filter_taps(L, fw1, fb1, ff1, fw2, fb2, ff2, fw3)
    kre, kim, knyq = hyena_filter_dft(hs, hd, cmat, smat)
    u = norm_mod_matmul(x, mod, nw, w_in.astype(BF16), out_dtype=BF16)
    x0, vg = hyena_pre(u, conv_w, conv_b)
    y = hyena_long_conv(vg, cmat, smat, kre, kim, knyq)
    return hyena_out(y, vg, x0, skip, x, mod, w_out.astype(BF16))


def _gdn_in_kernel(x_ref, mod_ref, nw_ref, w_ref, wab_hi_ref, wab_lo_ref, o_ref, ab_ref, abt_ref, h_scr, *, tn):
    h = _norm_mod(x_ref[0], nw_ref[...], mod_ref, 0, 1)
    h_hi, h_lo = _split_bf16(h)
    h_scr[...] = h_hi
    for c0 in range(0, GDN_IN, tn):
        o_ref[0, :, c0:c0 + tn] = _bdot(h_scr[...], w_ref[:, c0:c0 + tn]).astype(BF16)
    ab = _bdot(h_hi, wab_hi_ref[...]) + _bdot(h_lo, wab_hi_ref[...]) + _bdot(h_hi, wab_lo_ref[...])
    ab_ref[0] = ab
    abt_ref[0] = ab.T


def _pair_major(t):
    lead = t.shape[:-1]
    t = t.reshape(lead + (2, GDN_HK, GDN_HV // GDN_HK))
    return jnp.swapaxes(t, -1, -2).reshape(lead + (2 * GDN_HV,))


def gdn_in_proj(x, mod, nw, w_in, w_ab, *, tm=256, tn=512):
    B, L, _ = x.shape
    wab = jnp.concatenate([_pair_major(w_ab[:, :2 * GDN_HV]), _pair_major(w_ab[:, 2 * GDN_HV:])], axis=1)
    wab_hi, wab_lo = _split_bf16(jnp.pad(wab, ((0, 0), (0, LANE - wab.shape[1]))))
    return pl.pallas_call(
        functools.partial(_gdn_in_kernel, tn=tn),
        grid=(B, L // tm),
        in_specs=[
            pl.BlockSpec((1, tm, D), lambda b, i: (b, i, 0)),
            pl.BlockSpec((1, 6, D), lambda b, i: (b, 0, 0)),
            _resident((1, D)),
            _resident((D, GDN_IN)),
            _resident((D, LANE)),
            _resident((D, LANE)),
        ],
        out_specs=[
            pl.BlockSpec((1, tm, GDN_IN), lambda b, i: (b, i, 0)),
            pl.BlockSpec((1, tm, LANE), lambda b, i: (b, i, 0)),
            pl.BlockSpec((1, LANE, tm), lambda b, i: (b, 0, i)),
        ],
        out_shape=[jax.ShapeDtypeStruct((B, L, GDN_IN), BF16), jax.ShapeDtypeStruct((B, L, LANE), F32),
                   jax.ShapeDtypeStruct((B, LANE, L), F32)],
        scratch_shapes=[pltpu.VMEM((tm, D), BF16)],
        compiler_params=_cparams(("parallel", "parallel"), 48),
        name="gdn_in_proj",
    )(x, mod, nw.reshape(1, D), w_in.astype(BF16), wab_hi, wab_lo)


CONV_ROWS = 64
CONV_HALO = 16


def _dwconv_rows(u_ref, w_ref, b_ref, i):
    L = u_ref.shape[1]
    R, G = CONV_ROWS, CONV_HALO
    r0 = pl.multiple_of(i * R, R)
    u = u_ref[0, pl.ds(r0, R), :].astype(F32)
    lo = pl.multiple_of(jnp.maximum(r0 - G, 0), G)
    hi = pl.multiple_of(jnp.minimum(r0 + R, L - G), G)
    before = jnp.where(i == 0, 0.0, u_ref[0, pl.ds(lo, G), :].astype(F32)[G - 1:G])
    after = jnp.where(i == L // R - 1, 0.0, u_ref[0, pl.ds(hi, G), :].astype(F32)[0:1])
    t = lax.broadcasted_iota(jnp.int32, (R, 1), 0)
    prev = jnp.where(t == 0, before, pltpu.roll(u, 1, 0))
    nxt = jnp.where(t == R - 1, after, pltpu.roll(u, R - 1, 0))
    return b_ref[...] + w_ref[0:1, :] * prev + w_ref[1:2, :] * u + w_ref[2:3, :] * nxt


def _gdn_pre_kernel(u_ref, w_ref, b_ref, o_ref):
    L, ct = u_ref.shape[1], u_ref.shape[2]
    j = pl.program_id(1)
    nq_tiles = GDN_HK * GDN_DK // ct

    def run(normalise):
        scale = jnp.where(j < nq_tiles, GDN_DK ** -0.5, 1.0)

        def body(i, carry):
            y = _dwconv_rows(u_ref, w_ref, b_ref, i)
            y = y * jax.nn.sigmoid(y)
            rows = pl.ds(pl.multiple_of(i * CONV_ROWS, CONV_ROWS), CONV_ROWS)
            if normalise:
                for h in range(ct // GDN_DK):
                    yh = y[:, h * GDN_DK:(h + 1) * GDN_DK]
                    o_ref[0, rows, h * GDN_DK:(h + 1) * GDN_DK] = (yh * (
                        lax.rsqrt(jnp.sum(yh * yh, axis=-1, keepdims=True) + EPS) * scale)).astype(BF16)
            else:
                o_ref[0, rows, :] = y.astype(BF16)
            return carry

        lax.fori_loop(0, L // CONV_ROWS, body, 0, unroll=4)

    @pl.when(j >= 2 * nq_tiles)
    def _():
        run(False)

    @pl.when(j < 2 * nq_tiles)
    def _():
        run(True)


def gdn_pre(proj, conv_w, conv_b, *, ct=256):
    B, L, _ = proj.shape
    return pl.pallas_call(
        _gdn_pre_kernel,
        grid=(B, GDN_QKV // ct),
        in_specs=[
            pl.BlockSpec((1, L, ct), lambda b, j: (b, 0, j)),
            pl.BlockSpec((3, ct), lambda b, j: (0, j)),
            pl.BlockSpec((1, ct), lambda b, j: (0, j)),
        ],
        out_specs=pl.BlockSpec((1, L, ct), lambda b, j: (b, 0, j)),
        out_shape=jax.ShapeDtypeStruct((B, L, GDN_QKV), BF16),
        compiler_params=_cparams(("parallel", "parallel"), 40),
        name="gdn_pre",
    )(proj, conv_w, conv_b.reshape(1, GDN_QKV))


def _softplus(x):
    return jnp.maximum(x, 0.0) + jnp.log1p(jnp.exp(-jnp.abs(x)))


def _gdn_gates_kernel(ab_ref, abt_ref, alog_c_ref, dtb_c_ref, alog_r_ref, dtb_r_ref, col_ref, row_ref, *, tl):
    C = GDN_CHUNK
    H = GDN_HV
    i = lax.broadcasted_iota(jnp.int32, (C, C), 0)
    j = lax.broadcasted_iota(jnp.int32, (C, C), 1)
    lower = (i >= j).astype(F32)
    upper = (i <= j).astype(F32)
    P = GDN_HK
    ab = ab_ref[0]
    g_c = -jnp.exp(alog_c_ref[...]) * _softplus(ab[:, :2 * H] + dtb_c_ref[...])
    logbeta_c = -_softplus(-ab[:, 2 * H:4 * H])
    abt = abt_ref[0]
    g_r = -jnp.exp(alog_r_ref[...]) * _softplus(abt[:2 * H, :] + dtb_r_ref[...])
    beta_r = jax.nn.sigmoid(abt[2 * H:4 * H, :])
    zc = jnp.zeros((C, LANE - 4 * H), F32)

    def pack(x, swap):
        return jnp.concatenate([x[P:], x[:P]] if swap else [x[:P], x[P:]], axis=1)

    for c in range(tl // C):
        gch = g_c[c * C:(c + 1) * C, :]
        pre = jnp.dot(lower, gch, preferred_element_type=F32, precision=HI)
        suf = jnp.dot(upper, gch, preferred_element_type=F32, precision=HI)
        tot = jnp.sum(gch, axis=0, keepdims=True)
        grc = g_r[:, c * C:(c + 1) * C]
        pre_r = jnp.dot(grc, upper, preferred_element_type=F32, precision=HI)
        suf_r = jnp.dot(grc, lower, preferred_element_type=F32, precision=HI)
        tot_r = jnp.sum(grc, axis=1, keepdims=True)
        for d in range(2):
            hs = slice(d * H, (d + 1) * H)
            gc = (pre if d == 0 else suf)[:, hs]
            lb = logbeta_c[c * C:(c + 1) * C, hs]
            col_ref[0, d, c * C:(c + 1) * C, :] = jnp.concatenate(
                [gc, gc + lb, jnp.exp(gc), jnp.exp(tot[:, hs] - gc), zc], axis=1)
            gr = (pre_r if d == 0 else suf_r)[hs, :]
            br = beta_r[hs, c * C:(c + 1) * C]
            row_ref[0, d, c] = jnp.concatenate(
                [pack(gr, False), pack(br, True), pack(br * jnp.exp(gr), True),
                 jnp.broadcast_to(jnp.exp(tot_r[hs, :]), (H, LANE))], axis=0)


GDN_ROWS = 3 * GDN_HK + GDN_HV


def gdn_gates(ab, abt, a_log, dt_bias, *, tl=512):
    B, L, _ = ab.shape
    N = L // GDN_CHUNK
    H2 = 2 * GDN_HV
    return pl.pallas_call(
        functools.partial(_gdn_gates_kernel, tl=tl),
        grid=(B, L // tl),
        in_specs=[
            pl.BlockSpec((1, tl, LANE), lambda b, i: (b, i, 0)),
            pl.BlockSpec((1, LANE, tl), lambda b, i: (b, 0, i)),
            _resident((1, H2)), _resident((1, H2)), _resident((H2, 1)), _resident((H2, 1)),
        ],
        out_specs=[
            pl.BlockSpec((1, 2, tl, LANE), lambda b, i: (b, 0, i, 0)),
            pl.BlockSpec((1, 2, tl // GDN_CHUNK, GDN_ROWS, LANE), lambda b, i: (b, 0, i, 0, 0)),
        ],
        out_shape=[jax.ShapeDtypeStruct((B, 2, L, LANE), F32),
                   jax.ShapeDtypeStruct((B, 2, N, GDN_ROWS, LANE), F32)],
        compiler_params=_cparams(("parallel", "parallel"), 32),
        name="gdn_gates",
    )(ab, abt, _pair_major(a_log.reshape(1, H2)), _pair_major(dt_bias.reshape(1, H2)),
      _pair_major(a_log.reshape(H2)).reshape(H2, 1), _pair_major(dt_bias.reshape(H2)).reshape(H2, 1))


def _gdn_chunk_kernel(q_ref, k_ref, v_ref, col_ref, row_ref, o_ref, s_scr):
    C = GDN_CHUNK
    H = GDN_HV
    d = pl.program_id(1)
    n = pl.program_id(2)

    @pl.when(n == 0)
    def _():
        s_scr[...] = jnp.zeros_like(s_scr)

    P = GDN_HK
    W = 2 * GDN_DV
    sgn = 1 - 2 * d
    i = lax.broadcasted_iota(jnp.int32, (C, LANE), 0)
    lane = lax.broadcasted_iota(jnp.int32, (C, LANE), 1)
    order = (i - (lane & (C - 1))) * sgn
    incl = order >= 0
    strict = order > 0
    left = lane < C
    eye2 = (order == 0).astype(F32)
    col = col_ref[0, 0]
    row = row_ref[0, 0, 0]
    zf = jnp.zeros((C, LANE), F32)
    zb = jnp.zeros((C, LANE), BF16)
    zs = jnp.zeros((GDN_DK, GDN_DV), BF16)

    def col_pair(base, p):
        return jnp.where(left, col[:, base + p:base + p + 1], col[:, base + P + p:base + P + p + 1])

    def col_wide(base, p):
        return jnp.concatenate(
            [jnp.broadcast_to(col[:, base + e * P + p:base + e * P + p + 1], (C, GDN_DV)) for e in range(2)], axis=1)

    def block_diag(a, b, z):
        return jnp.concatenate([jnp.concatenate([a, z], axis=1), jnp.concatenate([z, b], axis=1)], axis=0)

    def anti_diag(a, b, z):
        return jnp.concatenate([jnp.concatenate([z, a], axis=1), jnp.concatenate([b, z], axis=1)], axis=0)

    ks =[k_ref[0, :, p * GDN_DK:(p + 1) * GDN_DK] for p in range(P)]
    qs = [q_ref[0, :, p * GDN_DK:(p + 1) * GDN_DK] for p in range(P)]
    grams = [_dot_nt(jnp.concatenate([ks[p], qs[p]], axis=0), jnp.concatenate([ks[p], ks[p]], axis=0))
             for p in range(P)]
    ms, intras = [], []
    for p in range(P):
        gc_j = row[p:p + 1, :]
        decay = jnp.exp(jnp.where(incl, col_pair(0, p) - gc_j, NEG))
        a_coef = jnp.exp(jnp.where(strict, col_pair(H, p) - gc_j, NEG))
        ms.append(-(grams[p][:C] * a_coef))
        intras.append((grams[p][C:] * decay).astype(BF16))
    tops = [jnp.where(left, m, eye2) for m in ms]
    bots = [jnp.where(left, eye2, m) for m in ms]
    for _ in range(6):
        for p in range(P):
            m_hi, m_lo = _split_bf16(ms[p])
            rhs_hi, rhs_lo = _split_bf16(block_diag(tops[p], bots[p], zf))
            r2 = _bdot(jnp.concatenate([m_hi, m_lo], axis=0), rhs_hi)
            r = r2[:C] + r2[C:] + _bdot(m_hi, rhs_lo)
            r0, r1 = r[:, :LANE], r[:, LANE:]
            tops[p] = jnp.where(left, r0, tops[p] + r0)
            bots[p] = jnp.where(left, bots[p] + r1, r1)
            ms[p] = jnp.where(left, r0, r1)
    us, ws = [], []
    for p in range(P):
        t = jnp.where(left, bots[p], tops[p])
        v0 = v_ref[0, :, 2 * p * GDN_DV:(2 * p + 1) * GDN_DV]
        v1 = v_ref[0, :, (2 * p + 1) * GDN_DV:(2 * p + 2) * GDN_DV]
        us.append(_bdot((t * row[P + p:P + p + 1, :]).astype(BF16), anti_diag(v1, v0, zb)))
        ws.append(_bdot((t * row[2 * P + p:2 * P + p + 1, :]).astype(BF16), anti_diag(ks[p], ks[p], zb)))
    for p in range(P):
        s = s_scr[p]
        sb = s.astype(BF16)
        lhs = jnp.concatenate([ws[p].astype(BF16), jnp.concatenate([qs[p], qs[p]], axis=1)], axis=0)
        ws_qs = _bdot(lhs, block_diag(sb[:, :GDN_DV], sb[:, GDN_DV:], zs))
        v_new = us[p] - ws_qs[:C]
        vb = v_new.astype(BF16)
        o_ref[0, 0, :, p * W:(p + 1) * W] = (col_wide(2 * H, p) * ws_qs[C:] + _bdot(
            intras[p], block_diag(vb[:, :GDN_DV], vb[:, GDN_DV:], zb))).astype(BF16)
        g_end = jnp.concatenate([row[3 * P + p:3 * P + p + 1, :], row[4 * P + p:4 * P + p + 1, :]], axis=1)
        s_scr[p] = s * g_end + _dot_tn(ks[p], (v_new * col_wide(3 * H, p)).astype(BF16))


def gdn_chunk_scan(qkv, col, row):
    B, L, _ = qkv.shape
    C = GDN_CHUNK
    N = L // C
    nq = GDN_HK * GDN_DK

    def cidx(d, n):
        return n + d * (N - 1 - 2 * n)

    return pl.pallas_call(
        _gdn_chunk_kernel,
        grid=(B, 2, N),
        in_specs=[
            pl.BlockSpec((1, C, nq), lambda b, d, n: (b, cidx(d, n), 0)),
            pl.BlockSpec((1, C, nq), lambda b, d, n: (b, cidx(d, n), 1)),
            pl.BlockSpec((1, C, GDN_HV * GDN_DV), lambda b, d, n: (b, cidx(d, n), 1)),
            pl.BlockSpec((1, 1, C, LANE), lambda b, d, n: (b, d, cidx(d, n), 0)),
            pl.BlockSpec((1, 1, 1, GDN_ROWS, LANE), lambda b, d, n: (b, d, cidx(d, n), 0, 0)),
        ],
        out_specs=pl.BlockSpec((1, 1, C, GDN_HV * GDN_DV), lambda b, d, n: (b, d, cidx(d, n), 0)),
        out_shape=jax.ShapeDtypeStruct((B, 2, L, GDN_HV * GDN_DV), BF16),
        scratch_shapes=[pltpu.VMEM((GDN_HK, GDN_DK, 2 * GDN_DV), F32)],
        compiler_params=_cparams(("parallel", "parallel", "arbitrary"), 32),
        name="gdn_chunk_scan",
    )(qkv, qkv, qkv, col, row)


def _gdn_out_kernel(o_ref, z_ref, nw_ref, x_ref, mod_ref, w_ref, out_ref, a_scr):
    o = o_ref[0, 0].astype(F32) + o_ref[0, 1].astype(F32)
    z = z_ref[0].astype(F32)
    gate = z * jax.nn.sigmoid(z)
    for h in range(GDN_HV):
        hs = slice(h * GDN_DV, (h + 1) * GDN_DV)
        a_scr[:, hs] = (_rms(o[:, hs]) * nw_ref[...] * gate[:, hs]).astype(BF16)
    out_ref[0] = x_ref[0] + mod_ref[0, 2:3, :] * _bdot(a_scr[...], w_ref[...])


def gdn_out(o2, proj, norm_w, x, mod, w, *, tm=256):
    B, L, _ = x.shape
    hd = GDN_HV * GDN_DV
    return pl.pallas_call(
        _gdn_out_kernel,
        grid=(B, L // tm),
        in_specs=[
            pl.BlockSpec((1, 2, tm, hd), lambda b, i: (b, 0, i, 0)),
            pl.BlockSpec((1, tm, hd), lambda b, i: (b, i, GDN_QKV // hd)),
            _resident((1, GDN_DV)),
            pl.BlockSpec((1, tm, D), lambda b, i: (b, i, 0)),
            pl.BlockSpec((1, 6, D), lambda b, i: (b, 0, 0)),
            _resident((hd, D)),
        ],
        out_specs=pl.BlockSpec((1, tm, D), lambda b, i: (b, i, 0)),
        out_shape=jax.ShapeDtypeStruct((B, L, D), F32),
        scratch_shapes=[pltpu.VMEM((tm, hd), BF16)],
        compiler_params=_cparams(("parallel", "parallel"), 40),
        name="gdn_out",
    )(o2, proj, norm_w.reshape(1, GDN_DV), x, mod, w)


def gdn_layer(x, mod, nw, p):
    (w_in, conv_w, conv_b, w_ab, a_log, dt_bias, norm_w, w_out) = p
    proj, ab, abt = gdn_in_proj(x, mod, nw, w_in, w_ab)
    qkv = gdn_pre(proj, conv_w, conv_b)
    col, row = gdn_gates(ab, abt, a_log, dt_bias)
    o2 = gdn_chunk_scan(qkv, col, row)
    return gdn_out(o2, proj, norm_w, x, mod, w_out.astype(BF16))


def _swa_kernel(q_ref, kp_ref, kc_ref, kn_ref, vp_ref, vc_ref, vn_ref, sink_ref, o_ref, s_scr, p_scr, *, nb):
    W = SWA_BLOCK
    KW = 3 * W
    nk = SWA_HKV * SWA_DH
    n = pl.program_id(1)
    qi = lax.broadcasted_iota(jnp.int32, (W, KW), 0)
    kj = lax.broadcasted_iota(jnp.int32, (W, KW), 1)
    dist = jnp.abs(kj - W - qi)
    valid = (dist <= SWA_WINDOW) & ((kj >= W) | (n > 0)) & ((kj < 2 * W) | (n < nb - 1))
    distm = jnp.where(valid, dist.astype(F32), -NEG)
    head_of_lane = lax.broadcasted_iota(jnp.int32, (KW, nk), 1) // SWA_DH
    kb = jnp.concatenate([kp_ref[0], kc_ref[0], kn_ref[0]], axis=0).astype(BF16)
    vb = jnp.concatenate([vp_ref[0], vc_ref[0], vn_ref[0]], axis=0).astype(BF16)
    kbd = jnp.concatenate([jnp.where(head_of_lane == h, kb, 0) for h in range(SWA_HKV)], axis=0)
    vbd = jnp.concatenate([jnp.where(head_of_lane == h, vb, 0) for h in range(SWA_HKV)], axis=0)
    q = jnp.concatenate([q_ref[0, :, g * nk:(g + 1) * nk] for g in range(SWA_G)], axis=0)
    s_all = _dot_nt((q * (SWA_DH ** -0.5 * LOG2E)).astype(BF16), kbd)
    group_of_row = lax.broadcasted_iota(jnp.int32, (SWA_G * W, 1), 0) // W
    distm4 = jnp.concatenate([distm] * SWA_G, axis=0)
    sinks, ms, rs = [], [], []
    for h in range(SWA_HKV):
        slope = jnp.zeros((SWA_G * W, 1), F32)
        sink = jnp.zeros((SWA_G * W, 1), F32)
        for g in range(SWA_G):
            hq = h * SWA_G + g
            slope = jnp.where(group_of_row == g, 2.0 ** (-8.0 * (hq + 1) / SWA_HQ) * LOG2E, slope)
            sink = jnp.where(group_of_row == g, sink_ref[0:1, hq:hq + 1] * LOG2E, sink)
        s = s_all[:, h * KW:(h + 1) * KW] - slope * distm4
        s_scr[:, h * KW:(h + 1) * KW] = s
        sinks.append(sink)
        ms.append(jnp.maximum(jnp.max(s, axis=-1, keepdims=True), sink))
    for h in range(SWA_HKV):
        e = jnp.exp2(s_scr[:, h * KW:(h + 1) * KW] - ms[h])
        p_scr[:, h * KW:(h + 1) * KW] = e.astype(BF16)
        rs.append(1.0 / (jnp.sum(e, axis=-1, keepdims=True) + jnp.exp2(sinks[h] - ms[h])))
    o = _bdot(p_scr[...], vbd)
    head_of_out = lax.broadcasted_iota(jnp.int32, (SWA_G * W, nk), 1) // SWA_DH
    r_all = jnp.broadcast_to(rs[0], (SWA_G * W, nk))
    for h in range(1, SWA_HKV):
        r_all = jnp.where(head_of_out == h, rs[h], r_all)
    o = (o * r_all).astype(BF16)
    for g in range(SWA_G):
        o_ref[0, :, g * nk:(g + 1) * nk] = o[g * W:(g + 1) * W]


def swa_attention(qkv, sink):
    B, L, _ = qkv.shape
    W = SWA_BLOCK
    nb = L // W
    nq = SWA_HQ * SWA_DH
    nk = SWA_HKV * SWA_DH
    kcol, vcol = nq // nk, nq // nk + 1

    def band(col):
        return [pl.BlockSpec((1, W, nk), lambda b, n: (b, jnp.maximum(n - 1, 0), col)),
                pl.BlockSpec((1, W, nk), lambda b, n: (b, n, col)),
                pl.BlockSpec((1, W, nk), lambda b, n: (b, jnp.minimum(n + 1, nb - 1), col))]

    return pl.pallas_call(
        functools.partial(_swa_kernel, nb=nb),
        grid=(B, nb),
        in_specs=[pl.BlockSpec((1, W, nq), lambda b, n: (b, n, 0))] + band(kcol) + band(vcol) + [
            _resident((1, SWA_HQ))],
        out_specs=pl.BlockSpec((1, W, nq), lambda b, n: (b, n, 0)),
        out_shape=jax.ShapeDtypeStruct((B, L, nq), BF16),
        scratch_shapes=[pltpu.VMEM((SWA_G * W, SWA_HKV * 3 * W), F32),
                        pltpu.VMEM((SWA_G * W, SWA_HKV * 3 * W), BF16)],
        compiler_params=_cparams(("parallel", "parallel"), 32),
        name="swa_attention",
    )(qkv, qkv, qkv, qkv, qkv, qkv, qkv, sink.reshape(1, SWA_HQ))


def swa_layer(x, mod, nw, p):
    (w_qkv, sink, w_out) = p
    nq = SWA_HQ * SWA_DH
    wq = w_qkv[:, :nq].reshape(D, SWA_HKV, SWA_G, SWA_DH).swapaxes(1, 2).reshape(D, nq)
    wo = w_out.reshape(SWA_HKV, SWA_G, SWA_DH, D).swapaxes(0, 1).reshape(nq, D)
    qkv = norm_mod_matmul(x, mod, nw, jnp.concatenate([wq, w_qkv[:, nq:]], axis=1).astype(BF16))
    o = swa_attention(qkv, sink)
    return matmul_gated_residual(o, x, mod, wo.astype(BF16))


def _mla_proj_kernel(d_ref, cos_ref, sin_ref, qnw_ref, kvnw_ref, wq_ref, wqr_ref, wk_ref, wv_ref,
                     q_ref, k_ref, v_ref):
    dd = d_ref[0]
    cq = (_rms(dd[:, :MLA_QRANK]) * qnw_ref[...]).astype(BF16)
    ckv = (_rms(dd[:, MLA_QRANK:MLA_QRANK + MLA_KVRANK]) * kvnw_ref[...]).astype(BF16)
    cs, sn = cos_ref[...], sin_ref[...]
    base = MLA_QRANK + MLA_KVRANK
    k_rope = dd[:, base:base + MLA_HP] * cs + dd[:, base + MLA_HP:base + 2 * MLA_HP] * sn
    qa = _bdot(cq, wq_ref[...])
    qb = _bdot(cq, wqr_ref[...])
    kn = _bdot(ckv, wk_ref[...])
    scale = (MLA_NOPE + MLA_ROPE) ** -0.5 * LOG2E
    for h in range(MLA_H):
        hs = slice(h * MLA_HP, (h + 1) * MLA_HP)
        q_ref[0, :, hs] = ((qa[:, hs] * cs + qb[:, hs] * sn) * scale).astype(BF16)
        k_ref[0, :, hs] = (kn[:, hs] + k_rope).astype(BF16)
    v = _bdot(ckv, wv_ref[...])
    lane = lax.broadcasted_iota(jnp.int32, v.shape, 1)
    v_ref[0] = jnp.where(lane % MLA_HP == MLA_DV, 1.0, v).astype(BF16)


def mla_project(dlat, cos_t, sin_t, q_norm_w, kv_norm_w, wq, wqr, wk, wv, *, tm=256):
    B, L, N = dlat.shape
    hp = MLA_H * MLA_HP
    return pl.pallas_call(
        _mla_proj_kernel,
        grid=(B, L // tm),
        in_specs=[
            pl.BlockSpec((1, tm, N), lambda b, i: (b, i, 0)),
            pl.BlockSpec((tm, MLA_HP), lambda b, i: (i, 0)),
            pl.BlockSpec((tm, MLA_HP), lambda b, i: (i, 0)),
            _resident((1, MLA_QRANK)), _resident((1, MLA_KVRANK)),
            _resident(wq.shape), _resident(wqr.shape), _resident(wk.shape), _resident(wv.shape),
        ],
        out_specs=[
            pl.BlockSpec((1, tm, hp), lambda b, i: (b, i, 0)),
            pl.BlockSpec((1, tm, hp), lambda b, i: (b, i, 0)),
            pl.BlockSpec((1, tm, hp), lambda b, i: (b, i, 0)),
        ],
        out_shape=[jax.ShapeDtypeStruct((B, L, hp), BF16)] * 3,
        compiler_params=_cparams(("parallel", "parallel"), 40),
        name="mla_project",
    )(dlat, cos_t, sin_t, q_norm_w.reshape(1, -1), kv_norm_w.reshape(1, -1), wq, wqr, wk, wv)


def _mla_attn_kernel(q_ref, k_ref, v_ref, o_ref):
    outs = []
    for h in range(2):
        hs = slice(h * MLA_HP, (h + 1) * MLA_HP)
        s = _dot_nt(q_ref[0, :, hs], k_ref[0, :, hs])
        e = jnp.exp2(s - jnp.max(s, axis=-1, keepdims=True)).astype(BF16)
        ov = _bdot(e, v_ref[0, :, hs])
        outs.append(ov[:, :MLA_DV] / ov[:, MLA_DV:MLA_DV + 1])
    o_ref[0] = jnp.concatenate(outs, axis=1).astype(BF16)


def mla_attention(q, k, v, *, tq=256):
    B, L, _ = q.shape
    return pl.pallas_call(
        _mla_attn_kernel,
        grid=(B, MLA_H // 2, L // tq),
        in_specs=[
            pl.BlockSpec((1, tq, 2 * MLA_HP), lambda b, h, i: (b, i, h)),
            pl.BlockSpec((1, L, 2 * MLA_HP), lambda b, h, i: (b, 0, h)),
            pl.BlockSpec((1, L, 2 * MLA_HP), lambda b, h, i: (b, 0, h)),
        ],
        out_specs=pl.BlockSpec((1, tq, 2 * MLA_DV), lambda b, h, i: (b, i, h)),
        out_shape=jax.ShapeDtypeStruct((B, L, MLA_H * MLA_DV), BF16),
        compiler_params=_cparams(("parallel", "parallel", "parallel"), 40),
        name="mla_attention",
    )(q, k, v)


def _rot_half_cols(w):
    half = MLA_ROPE // 2
    return jnp.concatenate([-w[..., half:], w[..., :half]], axis=-1)


def mla_layer(x, mod, nw, p):
    (w_down, q_norm_w, w_uq, kv_norm_w, w_ukv, w_out) = p
    L = x.shape[1]
    pad_r = MLA_HP - MLA_NOPE - MLA_ROPE
    base = MLA_QRANK + MLA_KVRANK
    w_rope = w_down[:, base:]
    zl = jnp.zeros((D, MLA_NOPE), F32)
    zr = jnp.zeros((D, pad_r), F32)
    w_dext = jnp.concatenate([w_down[:, :base], zl, w_rope, zr, zl, _rot_half_cols(w_rope), zr], axis=1)
    wq3 = w_uq.reshape(MLA_QRANK, MLA_H, MLA_NOPE + MLA_ROPE)
    zq = jnp.zeros((MLA_QRANK, MLA_H, pad_r), F32)
    wq = jnp.concatenate([wq3, zq], axis=-1).reshape(MLA_QRANK, -1).astype(BF16)
    wqr = jnp.concatenate([jnp.zeros((MLA_QRANK, MLA_H, MLA_NOPE), F32), _rot_half_cols(wq3[..., MLA_NOPE:]), zq],
                          axis=-1).reshape(MLA_QRANK, -1).astype(BF16)
    wkv3 = w_ukv.reshape(MLA_KVRANK, MLA_H, MLA_NOPE + MLA_DV)
    wk = jnp.concatenate([wkv3[..., :MLA_NOPE], jnp.zeros((MLA_KVRANK, MLA_H, MLA_HP - MLA_NOPE), F32)],
                         axis=-1).reshape(MLA_KVRANK, -1).astype(BF16)
    wv = jnp.concatenate([wkv3[..., MLA_NOPE:], jnp.zeros((MLA_KVRANK, MLA_H, MLA_HP - MLA_DV), F32)],
                         axis=-1).reshape(MLA_KVRANK, -1).astype(BF16)
    inv = ROPE_THETA ** (-jnp.arange(0, MLA_ROPE, 2, dtype=F32) / MLA_ROPE)
    ang = jnp.arange(L, dtype=F32)[:, None] * inv[None, :]
    cos, sin = jnp.cos(ang), jnp.sin(ang)
    cos_t = jnp.concatenate([jnp.ones((L, MLA_NOPE), F32), cos, cos, jnp.zeros((L, pad_r), F32)], axis=1)
    sin_t = jnp.concatenate([jnp.zeros((L, MLA_NOPE), F32), sin, sin, jnp.zeros((L, pad_r), F32)], axis=1)

    dlat = norm_mod_matmul(x, mod, nw, w_dext.astype(BF16))
    q, k, v = mla_project(dlat, cos_t, sin_t, q_norm_w, kv_norm_w, wq, wqr, wk, wv)
    o = mla_attention(q, k, v)
    return matmul_gated_residual(o, x, mod, w_out.astype(BF16))


def encoder_trunk(x, c, ada_w, ada_b, norm_w, hy, gdn, swa, mla, ffn_w_gu, ffn_w_down, final_norm_w):
    mods = ada_modulation(c, ada_w, ada_b)
    layers = (hyena_layer, gdn_layer, swa_layer, mla_layer)
    params = (hy, gdn, swa, mla)
    for i in range(DEPTH):
        kind, j = i % 4, i // 4
        x = layers[kind](x, mods[i], norm_w[i, 0], [p[j] for p in params[kind]])
        x = ffn_block(x, mods[i], norm_w[i, 1], ffn_w_gu[i].astype(BF16), ffn_w_down[i].astype(BF16))
    return final_norm(x, final_norm_w)


def kernel(x_prompt, x_sample, c_prompt, c_sample, ada_w, ada_b, norm_w, hy_w_in, hy_conv_w, hy_conv_b, hy_filt_w1, hy_filt_b1, hy_filt_freq1, hy_filt_w2, hy_filt_b2, hy_filt_freq2, hy_filt_w3, hy_skip, hy_w_out, gdn_w_in, gdn_conv_w, gdn_conv_b, gdn_w_ab, gdn_a_log, gdn_dt_bias, gdn_norm_w, gdn_w_out, swa_w_qkv, swa_sink, swa_w_out, mla_w_down, mla_q_norm_w, mla_w_uq, mla_kv_norm_w, mla_w_ukv, mla_w_out, ffn_w_gu, ffn_w_down, final_norm_w):
    hy = (hy_w_in, hy_conv_w, hy_conv_b, hy_filt_w1, hy_filt_b1, hy_filt_freq1,
          hy_filt_w2, hy_filt_b2, hy_filt_freq2, hy_filt_w3, hy_skip, hy_w_out)
    gdn = (gdn_w_in, gdn_conv_w, gdn_conv_b, gdn_w_ab, gdn_a_log, gdn_dt_bias, gdn_norm_w, gdn_w_out)
    swa = (swa_w_qkv, swa_sink, swa_w_out)
    mla = (mla_w_down, mla_q_norm_w, mla_w_uq, mla_kv_norm_w, mla_w_ukv, mla_w_out)
    args = (ada_w, ada_b, norm_w, hy, gdn, swa, mla, ffn_w_gu, ffn_w_down, final_norm_w)
    return (encoder_trunk(x_prompt, c_prompt, *args), encoder_trunk(x_sample, c_sample, *args))
```

```python
import functools
import math

import jax
import jax.numpy as jnp
from jax import lax
from jax.experimental import pallas as pl
from jax.experimental.pallas import tpu as pltpu

F32 = jnp.float32
BF16 = jnp.bfloat16
HI = lax.Precision.HIGHEST

D = 1024
DEPTH = 4
EPS = 1e-6
D_FF = 2816

HY_BANDS = 16
HY_FILT = 64
HY_TARGET = 1e-2
HY_FAST = 0.3
HY_SLOW = 1.5

GDN_HK = 8
GDN_HV = 16
GDN_DK = 128
GDN_DV = 128
GDN_CHUNK = 64
GDN_QKV = 2 * GDN_HK * GDN_DK + GDN_HV * GDN_DV
GDN_IN = GDN_QKV + GDN_HV * GDN_DV

SWA_HQ = 16
SWA_HKV = 4
SWA_G = SWA_HQ // SWA_HKV
SWA_DH = 64
SWA_WINDOW = 128
SWA_BLOCK = 128

MLA_H = 16
MLA_NOPE = 64
MLA_ROPE = 32
MLA_DV = 64
MLA_QRANK = 256
MLA_KVRANK = 256
MLA_HP = 128
ROPE_THETA = 10000.0

LANE = 128
LOG2E = math.log2(math.e)
MIB = 2 ** 20
NEG = -1e30


def _cparams(sem, vmem_mb):
    return pltpu.CompilerParams(dimension_semantics=sem, vmem_limit_bytes=vmem_mb * MIB)


def _resident(shape):
    nd = len(shape)
    return pl.BlockSpec(shape, lambda *_: (0,) * nd, pipeline_mode=pl.Buffered(1))


def _bdot(a, b):
    return jnp.dot(a, b, preferred_element_type=F32)


def _dot_nt(a, b, precision=None):
    return lax.dot_general(a, b, (((1,), (1,)), ((), ())), preferred_element_type=F32, precision=precision)


def _dot_tn(a, b, precision=None):
    return lax.dot_general(a, b, (((0,), (0,)), ((), ())), preferred_element_type=F32, precision=precision)


def _split_bf16(x):
    hi = x.astype(BF16)
    return hi, (x - hi.astype(F32)).astype(BF16)


def _rms(x):
    return x * lax.rsqrt(jnp.mean(x * x, axis=-1, keepdims=True) + EPS)


def _norm_mod(x, nw, mod_ref, sh_row, sc_row):
    return _rms(x) * nw * (1.0 + mod_ref[0, sc_row:sc_row + 1, :]) + mod_ref[0, sh_row:sh_row + 1, :]


def _ada_kernel(c_ref, w_ref, b_ref, o_ref):
    c = c_ref[...]
    ca = c * jax.nn.sigmoid(c)
    o_ref[0] = jnp.dot(ca, w_ref[0], preferred_element_type=F32, precision=HI) + b_ref[0]


def ada_modulation(c, ada_w, ada_b):
    B = c.shape[0]
    tn = 1024
    out = pl.pallas_call(
        _ada_kernel,
        grid=(DEPTH, 6 * D // tn),
        in_specs=[
            pl.BlockSpec((B, D), lambda i, j: (0, 0)),
            pl.BlockSpec((1, D, tn), lambda i, j: (i, 0, j)),
            pl.BlockSpec((1, 1, tn), lambda i, j: (i, 0, j)),
        ],
        out_specs=pl.BlockSpec((1, B, tn), lambda i, j: (i, 0, j)),
        out_shape=jax.ShapeDtypeStruct((DEPTH, B, 6 * D), F32),
        compiler_params=_cparams(("parallel", "parallel"), 32),
        name="ada_modulation",
    )(c, ada_w, ada_b.reshape(DEPTH, 1, 6 * D))
    return out.reshape(DEPTH, B, 6, D)


def _nmm_kernel(x_ref, mod_ref, nw_ref, w_ref, o_ref, h_scr, *, sh_row, sc_row, tn):
    h_scr[...] = _norm_mod(x_ref[0], nw_ref[...], mod_ref, sh_row, sc_row).astype(BF16)
    n = w_ref.shape[1]
    for c0 in range(0, n, tn):
        c1 = min(c0 + tn, n)
        o_ref[0, :, c0:c1] = _bdot(h_scr[...], w_ref[:, c0:c1]).astype(o_ref.dtype)


def norm_mod_matmul(x, mod, nw, w, *, sh_row=0, sc_row=1, tm=512, tn=512, out_dtype=F32):
    B, L, _ = x.shape
    N = w.shape[1]
    return pl.pallas_call(
        functools.partial(_nmm_kernel, sh_row=sh_row, sc_row=sc_row, tn=tn),
        grid=(B, L // tm),
        in_specs=[
            pl.BlockSpec((1, tm, D), lambda b, i: (b, i, 0)),
            pl.BlockSpec((1, 6, D), lambda b, i: (b, 0, 0)),
            _resident((1, D)),
            _resident((D, N)),
        ],
        out_specs=pl.BlockSpec((1, tm, N), lambda b, i: (b, i, 0)),
        out_shape=jax.ShapeDtypeStruct((B, L, N), out_dtype),
        scratch_shapes=[pltpu.VMEM((tm, D), BF16)],
        compiler_params=_cparams(("parallel", "parallel"), 48),
        name="norm_mod_matmul",
    )(x, mod, nw.reshape(1, D), w)


def _mmres_kernel(a_ref, x_ref, mod_ref, w_ref, o_ref, *, g_row):
    y = _bdot(a_ref[0].astype(BF16), w_ref[...])
    o_ref[0] = x_ref[0] + mod_ref[0, g_row:g_row + 1, :] * y


def matmul_gated_residual(a, x, mod, w, *, g_row=2, tm=512):
    B, L, K = a.shape
    return pl.pallas_call(
        functools.partial(_mmres_kernel, g_row=g_row),
        grid=(B, L // tm),
        in_specs=[
            pl.BlockSpec((1, tm, K), lambda b, i: (b, i, 0)),
            pl.BlockSpec((1, tm, D), lambda b, i: (b, i, 0)),
            pl.BlockSpec((1, 6, D), lambda b, i: (b, 0, 0)),
            _resident((K, D)),
        ],
        out_specs=pl.BlockSpec((1, tm, D), lambda b, i: (b, i, 0)),
        out_shape=jax.ShapeDtypeStruct((B, L, D), F32),
        compiler_params=_cparams(("parallel", "parallel"), 40),
        name="matmul_gated_residual",
    )(a, x, mod, w)


def _ffn_kernel(x_ref, mod_ref, nw_ref, wgu_ref, wd_ref, *rest, tf, final):
    fnw_ref = rest[0] if final else None
    o_ref, h_scr, act_scr = rest[-3:]
    x = x_ref[0]
    h_scr[...] = _norm_mod(x, nw_ref[...], mod_ref, 3, 4).astype(BF16)
    for c0 in range(0, D_FF, tf):
        gate = _bdot(h_scr[...], wgu_ref[:, c0:c0 + tf])
        up = _bdot(h_scr[...], wgu_ref[:, D_FF + c0:D_FF + c0 + tf])
        act_scr[:, c0:c0 + tf] = (gate * jax.nn.sigmoid(gate) * up).astype(BF16)
    y = x + mod_ref[0, 5:6, :] * _bdot(act_scr[...], wd_ref[...])
    o_ref[0] = _rms(y) * fnw_ref[...] if final else y


def ffn_block(x, mod, nw, w_gu, w_down, final_nw=None, *, tm=512, tf=256):
    B, L, _ = x.shape
    final = final_nw is not None
    return pl.pallas_call(
        functools.partial(_ffn_kernel, tf=tf, final=final),
        grid=(B, L // tm),
        in_specs=[
            pl.BlockSpec((1, tm, D), lambda b, i: (b, i, 0)),
            pl.BlockSpec((1, 6, D), lambda b, i: (b, 0, 0)),
            _resident((1, D)),
            _resident((D, 2 * D_FF)),
            _resident((D_FF, D)),
        ] + ([_resident((1, D))] if final else []),
        out_specs=pl.BlockSpec((1, tm, D), lambda b, i: (b, i, 0)),
        out_shape=jax.ShapeDtypeStruct((B, L, D), F32),
        scratch_shapes=[pltpu.VMEM((tm, D), BF16), pltpu.VMEM((tm, D_FF), BF16)],
        compiler_params=_cparams(("parallel", "parallel"), 48),
        name="ffn_block",
    )(x, mod, nw.reshape(1, D), w_gu, w_down, *([final_nw.reshape(1, D)] if final else []))


def _hy_filter_kernel(freq_ref, w1t_ref, w1c_ref, w1s_ref, b1_ref, f1_ref, w2_ref, b2_ref, f2_ref, w3_ref,
                      rate_ref, hs_ref, hd_ref, *, L, tl):
    pos = (pl.program_id(0) * tl + lax.broadcasted_iota(jnp.int32, (tl, 1), 0)).astype(F32)
    t = pos / max(L - 1, 1)
    ang = freq_ref[...] * (2.0 * math.pi / L) * pos
    z = (t * w1t_ref[...] + jnp.dot(jnp.cos(ang), w1c_ref[...], preferred_element_type=F32, precision=HI)
         - jnp.dot(jnp.sin(ang), w1s_ref[...], preferred_element_type=F32, precision=HI) + b1_ref[...])
    z = jnp.sin(f1_ref[...] * z)
    z = jnp.sin(f2_ref[...] * (jnp.dot(z, w2_ref[...], preferred_element_type=F32, precision=HI) + b2_ref[...]))
    z = jnp.dot(z, w3_ref[...], preferred_element_type=F32, precision=HI)
    window = jnp.exp(-t * rate_ref[...])
    hf = z[:, :D] * window
    hb = jnp.where(pos == 0.0, 0.0, z[:, D:] * window)
    hs_ref[...] = hf + hb
    hd_ref[...] = hb - hf


def hyena_filter_taps(L, fw1, fb1, ff1, fw2, fb2, ff2, fw3):
    tl = 256
    freqs = jnp.linspace(1e-4, HY_BANDS - 1, HY_BANDS, dtype=F32).reshape(1, HY_BANDS)
    rates = jnp.abs(jnp.linspace(math.log(HY_TARGET) / HY_SLOW, math.log(HY_TARGET) / HY_FAST, D, dtype=F32))
    small = [freqs, fw1[0:1], fw1[1:1 + HY_BANDS], fw1[1 + HY_BANDS:], fb1.reshape(1, -1), ff1.reshape(1, -1),
             fw2, fb2.reshape(1, -1), ff2.reshape(1, -1), fw3, rates.reshape(1, D)]
    return pl.pallas_call(
        functools.partial(_hy_filter_kernel, L=L, tl=tl),
        grid=(L // tl,),
        in_specs=[_resident(a.shape) for a in small],
        out_specs=[pl.BlockSpec((tl, D), lambda i: (i, 0))] * 2,
        out_shape=[jax.ShapeDtypeStruct((L, D), F32)] * 2,
        compiler_params=_cparams(("parallel",), 32),
        name="hyena_filter_taps",
    )(*small)


def _hy_filter_dft_kernel(hs_ref, hd_ref, cf_ref, sf_ref, kre_ref, kim_ref, knyq_ref, *, L, fk):
    k = pl.program_id(1)
    hs = hs_ref[...]
    inv_n = 1.0 / (2 * L)
    row = k * fk + lax.broadcasted_iota(jnp.int32, (fk, 1), 0)
    wk = jnp.where(row == 0, inv_n, 2.0 * inv_n)
    kre_ref[...] = wk * _bdot(cf_ref[...], hs.astype(BF16))
    kim_ref[...] = wk * _bdot(sf_ref[...], hd_ref[...].astype(BF16))

    @pl.when(k == 0)
    def _():
        t = lax.broadcasted_iota(jnp.int32, hs.shape, 0)
        sgn = (1 - 2 * (t & 1)).astype(F32)
        knyq_ref[...] = inv_n * jnp.sum(hs * sgn, axis=0, keepdims=True)


def hyena_filter_dft(hs, hd, cmat, smat, *, ct=256, fk=256):
    L = hs.shape[0]
    return pl.pallas_call(
        functools.partial(_hy_filter_dft_kernel, L=L, fk=fk),
        grid=(D // ct, L // fk),
        in_specs=[
            pl.BlockSpec((L, ct), lambda j, k: (0, j)),
            pl.BlockSpec((L, ct), lambda j, k: (0, j)),
            pl.BlockSpec((fk, L), lambda j, k: (k, 0)),
            pl.BlockSpec((fk, L), lambda j, k: (k, 0)),
        ],
        out_specs=[
            pl.BlockSpec((fk, ct), lambda j, k: (k, j)),
            pl.BlockSpec((fk, ct), lambda j, k: (k, j)),
            pl.BlockSpec((1, ct), lambda j, k: (0, j)),
        ],
        out_shape=[jax.ShapeDtypeStruct((L, D), F32), jax.ShapeDtypeStruct((L, D), F32),
                   jax.ShapeDtypeStruct((1, D), F32)],
        compiler_params=_cparams(("parallel", "arbitrary"), 48),
        name="hyena_filter_dft",
    )(hs, hd, cmat, smat)


def _hy_pre_kernel(u0_ref, u1_ref, u2_ref, w0_ref, w1_ref, w2_ref, b0_ref, b1_ref, b2_ref, x0_ref, vg_ref):
    L = u0_ref.shape[1]
    t = lax.broadcasted_iota(jnp.int32, (L, 1), 0)
    first, last = t == 0, t == L - 1

    def dwconv(u_ref, w_ref, b_ref):
        u = u_ref[0].astype(F32)
        prev = jnp.where(first, 0.0, pltpu.roll(u, 1, 0))
        nxt = jnp.where(last, 0.0, pltpu.roll(u, L - 1, 0))
        return b_ref[...] + w_ref[0:1, :] * prev + w_ref[1:2, :] * u + w_ref[2:3, :] * nxt

    x0_ref[0] = dwconv(u0_ref, w0_ref, b0_ref).astype(BF16)
    vg_ref[0] = (dwconv(u2_ref, w2_ref, b2_ref) * dwconv(u1_ref, w1_ref, b1_ref)).astype(BF16)


def hyena_pre(u, conv_w, conv_b, *, ct=128):
    B, L, _ = u.shape
    nj = D // ct
    conv_b = conv_b.reshape(1, 3 * D)
    ublk = [pl.BlockSpec((1, L, ct), lambda b, j, s=s: (b, 0, s * nj + j)) for s in range(3)]
    wblk = [pl.BlockSpec((3, ct), lambda b, j, s=s: (0, s * nj + j)) for s in range(3)]
    bblk = [pl.BlockSpec((1, ct), lambda b, j, s=s: (0, s * nj + j)) for s in range(3)]
    oblk = pl.BlockSpec((1, L, ct), lambda b, j: (b, 0, j))
    return pl.pallas_call(
        _hy_pre_kernel,
        grid=(B, nj),
        in_specs=ublk + wblk + bblk,
        out_specs=[oblk, oblk],
        out_shape=[jax.ShapeDtypeStruct((B, L, D), BF16)] * 2,
        compiler_params=_cparams(("parallel", "parallel"), 48),
        name="hyena_pre",
    )(u, u, u, conv_w, conv_w, conv_w, conv_b, conv_b, conv_b)


def _hy_conv_kernel(vb_ref, cf_ref, sf_ref, ci_ref, si_ref, kre_ref, kim_ref, knyq_ref, o_ref):
    k = pl.program_id(2)
    vb = vb_ref[0]

    @pl.when(k == 0)
    def _():
        t = lax.broadcasted_iota(jnp.int32, vb.shape, 0)
        sgn = (1 - 2 * (t & 1)).astype(F32)
        vnyq = jnp.sum(vb.astype(F32) * sgn, axis=0, keepdims=True)
        o_ref[0] = sgn * (vnyq * knyq_ref[...])

    vre = _bdot(cf_ref[...], vb)
    vim = _bdot(sf_ref[...], vb)
    kre, kim = kre_ref[...], kim_ref[...]
    yre = vre * kre + vim * kim
    nyim = vim * kre - vre * kim
    o_ref[0] += _bdot(ci_ref[...], yre.astype(BF16)) + _bdot(si_ref[...], nyim.astype(BF16))


def hyena_long_conv(vb, cmat, smat, kre, kim, knyq, *, ct=512, fk=256):
    B, L, _ = vb.shape
    return pl.pallas_call(
        _hy_conv_kernel,
        grid=(B, D // ct, L // fk),
        in_specs=[
            pl.BlockSpec((1, L, ct), lambda b, j, k: (b, 0, j)),
            pl.BlockSpec((fk, L), lambda b, j, k: (k, 0)),
            pl.BlockSpec((fk, L), lambda b, j, k: (k, 0)),
            pl.BlockSpec((L, fk), lambda b, j, k: (0, k)),
            pl.BlockSpec((L, fk), lambda b, j, k: (0, k)),
            pl.BlockSpec((fk, ct), lambda b, j, k: (k, j)),
            pl.BlockSpec((fk, ct), lambda b, j, k: (k, j)),
            pl.BlockSpec((1, ct), lambda b, j, k: (0, j)),
        ],
        out_specs=pl.BlockSpec((1, L, ct), lambda b, j, k: (b, 0, j)),
        out_shape=jax.ShapeDtypeStruct((B, L, D), F32),
        compiler_params=_cparams(("parallel", "parallel", "arbitrary"), 58),
        name="hyena_long_conv",
    )(vb, cmat, smat, cmat, smat, kre, kim, knyq)


def _hy_out_kernel(y_ref, vg_ref, x0_ref, skip_ref, x_ref, mod_ref, w_ref, o_ref):
    a = ((y_ref[0] + vg_ref[0] * skip_ref[...]) * x0_ref[0]).astype(BF16)
    o_ref[0] = x_ref[0] + mod_ref[0, 2:3, :] * _bdot(a, w_ref[...])


def hyena_out(y, vg, x0, skip, x, mod, w, *, tm=512):
    B, L, _ = x.shape
    tok = pl.BlockSpec((1, tm, D), lambda b, i: (b, i, 0))
    return pl.pallas_call(
        _hy_out_kernel,
        grid=(B, L // tm),
        in_specs=[tok, tok, tok, _resident((1, D)), tok, pl.BlockSpec((1, 6, D), lambda b, i: (b, 0, 0)),
                  _resident((D, D))],
        out_specs=tok,
        out_shape=jax.ShapeDtypeStruct((B, L, D), F32),
        compiler_params=_cparams(("parallel", "parallel"), 40),
        name="hyena_out",
    )(y, vg, x0, skip.reshape(1, D), x, mod, w)


def dft_tables(L):
    k = jnp.arange(L, dtype=jnp.int32)
    kt = (k[:, None] * k[None, :]) % (2 * L)
    ang = kt.astype(F32) * (math.pi / L)
    return jnp.cos(ang).astype(BF16), jnp.sin(ang).astype(BF16)


def hyena_layer(x, mod, nw, p):
    (w_in, conv_w, conv_b, fw1, fb1, ff1, fw2, fb2, ff2, fw3, skip, w_out) = p
    L = x.shape[1]
    cmat, smat = dft_tables(L)
    hs, hd = hyena_filter_taps(L, fw1, fb1, ff1, fw2, fb2, ff2, fw3)
    kre, kim, knyq = hyena_filter_dft(hs, hd, cmat, smat)
    u = norm_mod_matmul(x, mod, nw, w_in.astype(BF16), out_dtype=BF16)
    x0, vg = hyena_pre(u, conv_w, conv_b)
    y = hyena_long_conv(vg, cmat, smat, kre, kim, knyq, ct=512 if L <= 2048 else 256, fk=512)
    return hyena_out(y, vg, x0, skip, x, mod, w_out.astype(BF16))


def _gdn_in_kernel(x_ref, mod_ref, nw_ref, w_ref, wab_hi_ref, wab_lo_ref, o_ref, ab_ref, abt_ref, h_scr, *, tn):
    h = _norm_mod(x_ref[0], nw_ref[...], mod_ref, 0, 1)
    h_hi, h_lo = _split_bf16(h)
    h_scr[...] = h_hi
    for c0 in range(0, GDN_IN, tn):
        o_ref[0, :, c0:c0 + tn] = _bdot(h_scr[...], w_ref[:, c0:c0 + tn]).astype(BF16)
    ab = _bdot(h_hi, wab_hi_ref[...]) + _bdot(h_lo, wab_hi_ref[...]) + _bdot(h_hi, wab_lo_ref[...])
    ab_ref[0] = ab
    abt_ref[0] = ab.T


def _pair_major(t):
    lead = t.shape[:-1]
    t = t.reshape(lead + (2, GDN_HK, GDN_HV // GDN_HK))
    return jnp.swapaxes(t, -1, -2).reshape(lead + (2 * GDN_HV,))


def gdn_in_proj(x, mod, nw, w_in, w_ab, *, tm=256, tn=512):
    B, L, _ = x.shape
    wab = jnp.concatenate([_pair_major(w_ab[:, :2 * GDN_HV]), _pair_major(w_ab[:, 2 * GDN_HV:])], axis=1)
    wab_hi, wab_lo = _split_bf16(jnp.pad(wab, ((0, 0), (0, LANE - wab.shape[1]))))
    return pl.pallas_call(
        functools.partial(_gdn_in_kernel, tn=tn),
        grid=(B, L // tm),
        in_specs=[
            pl.BlockSpec((1, tm, D), lambda b, i: (b, i, 0)),
            pl.BlockSpec((1, 6, D), lambda b, i: (b, 0, 0)),
            _resident((1, D)),
            _resident((D, GDN_IN)),
            _resident((D, LANE)),
            _resident((D, LANE)),
        ],
        out_specs=[
            pl.BlockSpec((1, tm, GDN_IN), lambda b, i: (b, i, 0)),
            pl.BlockSpec((1, tm, LANE), lambda b, i: (b, i, 0)),
            pl.BlockSpec((1, LANE, tm), lambda b, i: (b, 0, i)),
        ],
        out_shape=[jax.ShapeDtypeStruct((B, L, GDN_IN), BF16), jax.ShapeDtypeStruct((B, L, LANE), F32),
                   jax.ShapeDtypeStruct((B, LANE, L), F32)],
        scratch_shapes=[pltpu.VMEM((tm, D), BF16)],
        compiler_params=_cparams(("parallel", "parallel"), 48),
        name="gdn_in_proj",
    )(x, mod, nw.reshape(1, D), w_in.astype(BF16), wab_hi, wab_lo)


CONV_ROWS = 64
CONV_HALO = 16


def _dwconv_rows(u_ref, w_ref, b_ref, i):
    L = u_ref.shape[1]
    R, G = CONV_ROWS, CONV_HALO
    r0 = pl.multiple_of(i * R, R)
    u = u_ref[0, pl.ds(r0, R), :].astype(F32)
    lo = pl.multiple_of(jnp.maximum(r0 - G, 0), G)
    hi = pl.multiple_of(jnp.minimum(r0 + R, L - G), G)
    before = jnp.where(i == 0, 0.0, u_ref[0, pl.ds(lo, G), :].astype(F32)[G - 1:G])
    after = jnp.where(i == L // R - 1, 0.0, u_ref[0, pl.ds(hi, G), :].astype(F32)[0:1])
    t = lax.broadcasted_iota(jnp.int32, (R, 1), 0)
    prev = jnp.where(t == 0, before, pltpu.roll(u, 1, 0))
    nxt = jnp.where(t == R - 1, after, pltpu.roll(u, R - 1, 0))
    return b_ref[...] + w_ref[0:1, :] * prev + w_ref[1:2, :] * u + w_ref[2:3, :] * nxt


def _gdn_pre_kernel(u_ref, w_ref, b_ref, o_ref):
    L, ct = u_ref.shape[1], u_ref.shape[2]
    j = pl.program_id(1)
    nq_tiles = GDN_HK * GDN_DK // ct

    def run(normalise):
        scale = jnp.where(j < nq_tiles, GDN_DK ** -0.5, 1.0)

        def body(i, carry):
            y = _dwconv_rows(u_ref, w_ref, b_ref, i)
            y = y * jax.nn.sigmoid(y)
            rows = pl.ds(pl.multiple_of(i * CONV_ROWS, CONV_ROWS), CONV_ROWS)
            if normalise:
                for h in range(ct // GDN_DK):
                    yh = y[:, h * GDN_DK:(h + 1) * GDN_DK]
                    o_ref[0, rows, h * GDN_DK:(h + 1) * GDN_DK] = (yh * (
                        lax.rsqrt(jnp.sum(yh * yh, axis=-1, keepdims=True) + EPS) * scale)).astype(BF16)
            else:
                o_ref[0, rows, :] = y.astype(BF16)
            return carry

        lax.fori_loop(0, L // CONV_ROWS, body, 0, unroll=4)

    @pl.when(j >= 2 * nq_tiles)
    def _():
        run(False)

    @pl.when(j < 2 * nq_tiles)
    def _():
        run(True)


def gdn_pre(proj, conv_w, conv_b, *, ct=256):
    B, L, _ = proj.shape
    return pl.pallas_call(
        _gdn_pre_kernel,
        grid=(B, GDN_QKV // ct),
        in_specs=[
            pl.BlockSpec((1, L, ct), lambda b, j: (b, 0, j)),
            pl.BlockSpec((3, ct), lambda b, j: (0, j)),
            pl.BlockSpec((1, ct), lambda b, j: (0, j)),
        ],
        out_specs=pl.BlockSpec((1, L, ct), lambda b, j: (b, 0, j)),
        out_shape=jax.ShapeDtypeStruct((B, L, GDN_QKV), BF16),
        compiler_params=_cparams(("parallel", "parallel"), 40),
        name="gdn_pre",
    )(proj, conv_w, conv_b.reshape(1, GDN_QKV))


def _softplus(x):
    return jnp.maximum(x, 0.0) + jnp.log1p(jnp.exp(-jnp.abs(x)))


def _gdn_gates_kernel(ab_ref, abt_ref, alog_c_ref, dtb_c_ref, alog_r_ref, dtb_r_ref, col_ref, row_ref, *, tl):
    C = GDN_CHUNK
    H = GDN_HV
    i = lax.broadcasted_iota(jnp.int32, (C, C), 0)
    j = lax.broadcasted_iota(jnp.int32, (C, C), 1)
    lower = (i >= j).astype(F32)
    upper = (i <= j).astype(F32)
    P = GDN_HK
    ab = ab_ref[0]
    g_c = -jnp.exp(alog_c_ref[...]) * _softplus(ab[:, :2 * H] + dtb_c_ref[...])
    logbeta_c = -_softplus(-ab[:, 2 * H:4 * H])
    abt = abt_ref[0]
    g_r = -jnp.exp(alog_r_ref[...]) * _softplus(abt[:2 * H, :] + dtb_r_ref[...])
    beta_r = jax.nn.sigmoid(abt[2 * H:4 * H, :])
    zc = jnp.zeros((C, LANE - 4 * H), F32)

    def pack(x, swap):
        return jnp.concatenate([x[P:], x[:P]] if swap else [x[:P], x[P:]], axis=1)

    for c in range(tl // C):
        gch = g_c[c * C:(c + 1) * C, :]
        pre = jnp.dot(lower, gch, preferred_element_type=F32, precision=HI)
        suf = jnp.dot(upper, gch, preferred_element_type=F32, precision=HI)
        tot = jnp.sum(gch, axis=0, keepdims=True)
        grc = g_r[:, c * C:(c + 1) * C]
        pre_r = jnp.dot(grc, upper, preferred_element_type=F32, precision=HI)
        suf_r = jnp.dot(grc, lower, preferred_element_type=F32, precision=HI)
        tot_r = jnp.sum(grc, axis=1, keepdims=True)
        for d in range(2):
            hs = slice(d * H, (d + 1) * H)
            gc = (pre if d == 0 else suf)[:, hs]
            lb = logbeta_c[c * C:(c + 1) * C, hs]
            col_ref[0, d, c * C:(c + 1) * C, :] = jnp.concatenate(
                [gc, gc + lb, jnp.exp(gc), jnp.exp(tot[:, hs] - gc), zc], axis=1)
            gr = (pre_r if d == 0 else suf_r)[hs, :]
            br = beta_r[hs, c * C:(c + 1) * C]
            row_ref[0, d, c] = jnp.concatenate(
                [pack(gr, False), pack(br, True), pack(br * jnp.exp(gr), True),
                 jnp.broadcast_to(jnp.exp(tot_r[hs, :]), (H, LANE))], axis=0)


GDN_ROWS = 3 * GDN_HK + GDN_HV


def gdn_gates(ab, abt, a_log, dt_bias, *, tl=512):
    B, L, _ = ab.shape
    N = L // GDN_CHUNK
    H2 = 2 * GDN_HV
    return pl.pallas_call(
        functools.partial(_gdn_gates_kernel, tl=tl),
        grid=(B, L // tl),
        in_specs=[
            pl.BlockSpec((1, tl, LANE), lambda b, i: (b, i, 0)),
            pl.BlockSpec((1, LANE, tl), lambda b, i: (b, 0, i)),
            _resident((1, H2)), _resident((1, H2)), _resident((H2, 1)), _resident((H2, 1)),
        ],
        out_specs=[
            pl.BlockSpec((1, 2, tl, LANE), lambda b, i: (b, 0, i, 0)),
            pl.BlockSpec((1, 2, tl // GDN_CHUNK, GDN_ROWS, LANE), lambda b, i: (b, 0, i, 0, 0)),
        ],
        out_shape=[jax.ShapeDtypeStruct((B, 2, L, LANE), F32),
                   jax.ShapeDtypeStruct((B, 2, N, GDN_ROWS, LANE), F32)],
        compiler_params=_cparams(("parallel", "parallel"), 32),
        name="gdn_gates",
    )(ab, abt, _pair_major(a_log.reshape(1, H2)), _pair_major(dt_bias.reshape(1, H2)),
      _pair_major(a_log.reshape(H2)).reshape(H2, 1), _pair_major(dt_bias.reshape(H2)).reshape(H2, 1))


def _gdn_chunk_kernel(q_ref, k_ref, v_ref, col_ref, row_ref, o_ref, s_scr):
    C = GDN_CHUNK
    H = GDN_HV
    d = pl.program_id(1)
    n = pl.program_id(2)

    @pl.when(n == 0)
    def _():
        s_scr[...] = jnp.zeros_like(s_scr)

    P = GDN_HK
    W = 2 * GDN_DV
    sgn = 1 - 2 * d
    i = lax.broadcasted_iota(jnp.int32, (C, LANE), 0)
    lane = lax.broadcasted_iota(jnp.int32, (C, LANE), 1)
    order = (i - (lane & (C - 1))) * sgn
    incl = order >= 0
    strict = order > 0
    left = lane < C
    eye2 = (order == 0).astype(F32)
    col = col_ref[0, 0]
    row = row_ref[0, 0, 0]
    zb = jnp.zeros((C, LANE), BF16)
    zs = jnp.zeros((GDN_DK, GDN_DV), BF16)

    def col_pair(base, p):
        return jnp.where(left, col[:, base + p:base + p + 1], col[:, base + P + p:base + P + p + 1])

    def col_wide(base, p):
        return jnp.concatenate(
            [jnp.broadcast_to(col[:, base + e * P + p:base + e * P + p + 1], (C, GDN_DV)) for e in range(2)], axis=1)

    def block_diag(a, b, z):
        return jnp.concatenate([jnp.concatenate([a, z], axis=1), jnp.concatenate([z, b], axis=1)], axis=0)

    def anti_diag(a, b, z):
        return jnp.concatenate([jnp.concatenate([z, a], axis=1), jnp.concatenate([b, z], axis=1)], axis=0)

    ks =[k_ref[0, :, p * GDN_DK:(p + 1) * GDN_DK] for p in range(P)]
    qs = [q_ref[0, :, p * GDN_DK:(p + 1) * GDN_DK] for p in range(P)]
    grams = [_dot_nt(jnp.concatenate([ks[p], qs[p]], axis=0), jnp.concatenate([ks[p], ks[p]], axis=0))
             for p in range(P)]
    ms, intras = [], []
    for p in range(P):
        gc_j = row[p:p + 1, :]
        decay = jnp.exp(jnp.where(incl, col_pair(0, p) - gc_j, NEG))
        a_coef = jnp.exp(jnp.where(strict, col_pair(H, p) - gc_j, NEG))
        ms.append(-(grams[p][:C] * a_coef))
        intras.append((grams[p][C:] * decay).astype(BF16))
    tops = [jnp.where(left, m, eye2) for m in ms]
    bots = [jnp.where(left, eye2, m) for m in ms]
    for _ in range(6):
        for p in range(P):
            m_hi, m_lo = _split_bf16(ms[p])
            top_hi, top_lo = _split_bf16(tops[p])
            bot_hi, bot_lo = _split_bf16(bots[p])
            r2 = _bdot(jnp.concatenate([m_hi, m_lo], axis=0), block_diag(top_hi, bot_hi, zb))
            r = r2[:C] + r2[C:] + _bdot(m_hi, block_diag(top_lo, bot_lo, zb))
            r0, r1 = r[:, :LANE], r[:, LANE:]
            tops[p] = jnp.where(left, r0, tops[p] + r0)
            bots[p] = jnp.where(left, bots[p] + r1, r1)
            ms[p] = jnp.where(left, r0, r1)
    us, ws = [], []
    for p in range(P):
        t = jnp.where(left, bots[p], tops[p])
        v0 = v_ref[0, :, 2 * p * GDN_DV:(2 * p + 1) * GDN_DV]
        v1 = v_ref[0, :, (2 * p + 1) * GDN_DV:(2 * p + 2) * GDN_DV]
        us.append(_bdot((t * row[P + p:P + p + 1, :]).astype(BF16), anti_diag(v1, v0, zb)))
        ws.append(_bdot((t * row[2 * P + p:2 * P + p + 1, :]).astype(BF16), anti_diag(ks[p], ks[p], zb)))
    for p in range(P):
        s = s_scr[p]
        sb = s.astype(BF16)
        lhs = jnp.concatenate([ws[p].astype(BF16), jnp.concatenate([qs[p], qs[p]], axis=1)], axis=0)
        ws_qs = _bdot(lhs, block_diag(sb[:, :GDN_DV], sb[:, GDN_DV:], zs))
        v_new = us[p] - ws_qs[:C]
        vb = v_new.astype(BF16)
        o_ref[0, 0, :, p * W:(p + 1) * W] = (col_wide(2 * H, p) * ws_qs[C:] + _bdot(
            intras[p], block_diag(vb[:, :GDN_DV], vb[:, GDN_DV:], zb))).astype(BF16)
        g_end = jnp.concatenate([row[3 * P + p:3 * P + p + 1, :], row[4 * P + p:4 * P + p + 1, :]], axis=1)
        s_scr[p] = s * g_end + _dot_tn(ks[p], (v_new * col_wide(3 * H, p)).astype(BF16))


def gdn_chunk_scan(qkv, col, row):
    B, L, _ = qkv.shape
    C = GDN_CHUNK
    N = L // C
    nq = GDN_HK * GDN_DK

    def cidx(d, n):
        return n + d * (N - 1 - 2 * n)

    return pl.pallas_call(
        _gdn_chunk_kernel,
        grid=(B, 2, N),
        in_specs=[
            pl.BlockSpec((1, C, nq), lambda b, d, n: (b, cidx(d, n), 0)),
            pl.BlockSpec((1, C, nq), lambda b, d, n: (b, cidx(d, n), 1)),
            pl.BlockSpec((1, C, GDN_HV * GDN_DV), lambda b, d, n: (b, cidx(d, n), 1)),
            pl.BlockSpec((1, 1, C, LANE), lambda b, d, n: (b, d, cidx(d, n), 0)),
            pl.BlockSpec((1, 1, 1, GDN_ROWS, LANE), lambda b, d, n: (b, d, cidx(d, n), 0, 0)),
        ],
        out_specs=pl.BlockSpec((1, 1, C, GDN_HV * GDN_DV), lambda b, d, n: (b, d, cidx(d, n), 0)),
        out_shape=jax.ShapeDtypeStruct((B, 2, L, GDN_HV * GDN_DV), BF16),
        scratch_shapes=[pltpu.VMEM((GDN_HK, GDN_DK, 2 * GDN_DV), F32)],
        compiler_params=_cparams(("parallel", "parallel", "arbitrary"), 32),
        name="gdn_chunk_scan",
    )(qkv, qkv, qkv, col, row)


def _gdn_out_kernel(o_ref, z_ref, nw_ref, x_ref, mod_ref, w_ref, out_ref, a_scr):
    o = o_ref[0, 0].astype(F32) + o_ref[0, 1].astype(F32)
    z = z_ref[0].astype(F32)
    gate = z * jax.nn.sigmoid(z)
    for h in range(GDN_HV):
        hs = slice(h * GDN_DV, (h + 1) * GDN_DV)
        a_scr[:, hs] = (_rms(o[:, hs]) * nw_ref[...] * gate[:, hs]).astype(BF16)
    out_ref[0] = x_ref[0] + mod_ref[0, 2:3, :] * _bdot(a_scr[...], w_ref[...])


def gdn_out(o2, proj, norm_w, x, mod, w, *, tm=256):
    B, L, _ = x.shape
    hd = GDN_HV * GDN_DV
    return pl.pallas_call(
        _gdn_out_kernel,
        grid=(B, L // tm),
        in_specs=[
            pl.BlockSpec((1, 2, tm, hd), lambda b, i: (b, 0, i, 0)),
            pl.BlockSpec((1, tm, hd), lambda b, i: (b, i, GDN_QKV // hd)),
            _resident((1, GDN_DV)),
            pl.BlockSpec((1, tm, D), lambda b, i: (b, i, 0)),
            pl.BlockSpec((1, 6, D), lambda b, i: (b, 0, 0)),
            _resident((hd, D)),
        ],
        out_specs=pl.BlockSpec((1, tm, D), lambda b, i: (b, i, 0)),
        out_shape=jax.ShapeDtypeStruct((B, L, D), F32),
        scratch_shapes=[pltpu.VMEM((tm, hd), BF16)],
        compiler_params=_cparams(("parallel", "parallel"), 40),
        name="gdn_out",
    )(o2, proj, norm_w.reshape(1, GDN_DV), x, mod, w)


def gdn_layer(x, mod, nw, p):
    (w_in, conv_w, conv_b, w_ab, a_log, dt_bias, norm_w, w_out) = p
    proj, ab, abt = gdn_in_proj(x, mod, nw, w_in, w_ab)
    qkv = gdn_pre(proj, conv_w, conv_b)
    col, row = gdn_gates(ab, abt, a_log, dt_bias)
    o2 = gdn_chunk_scan(qkv, col, row)
    return gdn_out(o2, proj, norm_w, x, mod, w_out.astype(BF16))


def _swa_kernel(q_ref, kp_ref, kc_ref, kn_ref, vp_ref, vc_ref, vn_ref, sink_ref, o_ref, s_scr, p_scr, *, nb):
    W = SWA_BLOCK
    KW = 3 * W
    nk = SWA_HKV * SWA_DH
    n = pl.program_id(1)
    qi = lax.broadcasted_iota(jnp.int32, (W, KW), 0)
    kj = lax.broadcasted_iota(jnp.int32, (W, KW), 1)
    dist = jnp.abs(kj - W - qi)
    valid = (dist <= SWA_WINDOW) & ((kj >= W) | (n > 0)) & ((kj < 2 * W) | (n < nb - 1))
    distm = jnp.where(valid, dist.astype(F32), -NEG)
    head_of_lane = lax.broadcasted_iota(jnp.int32, (KW, nk), 1) // SWA_DH
    kb = jnp.concatenate([kp_ref[0], kc_ref[0], kn_ref[0]], axis=0).astype(BF16)
    vb = jnp.concatenate([vp_ref[0], vc_ref[0], vn_ref[0]], axis=0).astype(BF16)
    kbd = jnp.concatenate([jnp.where(head_of_lane == h, kb, 0) for h in range(SWA_HKV)], axis=0)
    vbd = jnp.concatenate([jnp.where(head_of_lane == h, vb, 0) for h in range(SWA_HKV)], axis=0)
    q = jnp.concatenate([q_ref[0, :, g * nk:(g + 1) * nk] for g in range(SWA_G)], axis=0)
    s_all = _dot_nt((q * (SWA_DH ** -0.5 * LOG2E)).astype(BF16), kbd)
    group_of_row = lax.broadcasted_iota(jnp.int32, (SWA_G * W, 1), 0) // W
    distm4 = jnp.concatenate([distm] * SWA_G, axis=0)
    sinks, ms, rs = [], [], []
    for h in range(SWA_HKV):
        slope = jnp.zeros((SWA_G * W, 1), F32)
        sink = jnp.zeros((SWA_G * W, 1), F32)
        for g in range(SWA_G):
            hq = h * SWA_G + g
            slope = jnp.where(group_of_row == g, 2.0 ** (-8.0 * (hq + 1) / SWA_HQ) * LOG2E, slope)
            sink = jnp.where(group_of_row == g, sink_ref[0:1, hq:hq + 1] * LOG2E, sink)
        s = s_all[:, h * KW:(h + 1) * KW] - slope * distm4
        s_scr[:, h * KW:(h + 1) * KW] = s
        sinks.append(sink)
        ms.append(jnp.maximum(jnp.max(s, axis=-1, keepdims=True), sink))
    for h in range(SWA_HKV):
        e = jnp.exp2(s_scr[:, h * KW:(h + 1) * KW] - ms[h])
        p_scr[:, h * KW:(h + 1) * KW] = e.astype(BF16)
        rs.append(1.0 / (jnp.sum(e, axis=-1, keepdims=True) + jnp.exp2(sinks[h] - ms[h])))
    o = _bdot(p_scr[...], vbd)
    head_of_out = lax.broadcasted_iota(jnp.int32, (SWA_G * W, nk), 1) // SWA_DH
    r_all = jnp.broadcast_to(rs[0], (SWA_G * W, nk))
    for h in range(1, SWA_HKV):
        r_all = jnp.where(head_of_out == h, rs[h], r_all)
    o = (o * r_all).astype(BF16)
    for g in range(SWA_G):
        o_ref[0, :, g * nk:(g + 1) * nk] = o[g * W:(g + 1) * W]


def swa_attention(qkv, sink):
    B, L, _ = qkv.shape
    W = SWA_BLOCK
    nb = L // W
    nq = SWA_HQ * SWA_DH
    nk = SWA_HKV * SWA_DH
    kcol, vcol = nq // nk, nq // nk + 1

    def band(col):
        return [pl.BlockSpec((1, W, nk), lambda b, n: (b, jnp.maximum(n - 1, 0), col)),
                pl.BlockSpec((1, W, nk), lambda b, n: (b, n, col)),
                pl.BlockSpec((1, W, nk), lambda b, n: (b, jnp.minimum(n + 1, nb - 1), col))]

    return pl.pallas_call(
        functools.partial(_swa_kernel, nb=nb),
        grid=(B, nb),
        in_specs=[pl.BlockSpec((1, W, nq), lambda b, n: (b, n, 0))] + band(kcol) + band(vcol) + [
            _resident((1, SWA_HQ))],
        out_specs=pl.BlockSpec((1, W, nq), lambda b, n: (b, n, 0)),
        out_shape=jax.ShapeDtypeStruct((B, L, nq), BF16),
        scratch_shapes=[pltpu.VMEM((SWA_G * W, SWA_HKV * 3 * W), F32),
                        pltpu.VMEM((SWA_G * W, SWA_HKV * 3 * W), BF16)],
        compiler_params=_cparams(("parallel", "parallel"), 32),
        name="swa_attention",
    )(qkv, qkv, qkv, qkv, qkv, qkv, qkv, sink.reshape(1, SWA_HQ))


def swa_layer(x, mod, nw, p):
    (w_qkv, sink, w_out) = p
    nq = SWA_HQ * SWA_DH
    wq = w_qkv[:, :nq].reshape(D, SWA_HKV, SWA_G, SWA_DH).swapaxes(1, 2).reshape(D, nq)
    wo = w_out.reshape(SWA_HKV, SWA_G, SWA_DH, D).swapaxes(0, 1).reshape(nq, D)
    qkv = norm_mod_matmul(x, mod, nw, jnp.concatenate([wq, w_qkv[:, nq:]], axis=1).astype(BF16))
    o = swa_attention(qkv, sink)
    return matmul_gated_residual(o, x, mod, wo.astype(BF16))


def _mla_proj_kernel(d_ref, cos_ref, sin_ref, qnw_ref, kvnw_ref, wq_ref, wqr_ref, wk_ref, wv_ref,
                     q_ref, k_ref, v_ref):
    dd = d_ref[0]
    cq = (_rms(dd[:, :MLA_QRANK]) * qnw_ref[...]).astype(BF16)
    ckv = (_rms(dd[:, MLA_QRANK:MLA_QRANK + MLA_KVRANK]) * kvnw_ref[...]).astype(BF16)
    cs, sn = cos_ref[...], sin_ref[...]
    base = MLA_QRANK + MLA_KVRANK
    k_rope = dd[:, base:base + MLA_HP] * cs + dd[:, base + MLA_HP:base + 2 * MLA_HP] * sn
    qa = _bdot(cq, wq_ref[...])
    qb = _bdot(cq, wqr_ref[...])
    kn = _bdot(ckv, wk_ref[...])
    scale = (MLA_NOPE + MLA_ROPE) ** -0.5 * LOG2E
    for h in range(MLA_H):
        hs = slice(h * MLA_HP, (h + 1) * MLA_HP)
        q_ref[0, :, hs] = ((qa[:, hs] * cs + qb[:, hs] * sn) * scale).astype(BF16)
        k_ref[0, :, hs] = (kn[:, hs] + k_rope).astype(BF16)
    v = _bdot(ckv, wv_ref[...])
    lane = lax.broadcasted_iota(jnp.int32, v.shape, 1)
    v_ref[0] = jnp.where(lane % MLA_HP == MLA_DV, 1.0, v).astype(BF16)


def mla_project(dlat, cos_t, sin_t, q_norm_w, kv_norm_w, wq, wqr, wk, wv, *, tm=256):
    B, L, N = dlat.shape
    hp = MLA_H * MLA_HP
    return pl.pallas_call(
        _mla_proj_kernel,
        grid=(B, L // tm),
        in_specs=[
            pl.BlockSpec((1, tm, N), lambda b, i: (b, i, 0)),
            pl.BlockSpec((tm, MLA_HP), lambda b, i: (i, 0)),
            pl.BlockSpec((tm, MLA_HP), lambda b, i: (i, 0)),
            _resident((1, MLA_QRANK)), _resident((1, MLA_KVRANK)),
            _resident(wq.shape), _resident(wqr.shape), _resident(wk.shape), _resident(wv.shape),
        ],
        out_specs=[
            pl.BlockSpec((1, tm, hp), lambda b, i: (b, i, 0)),
            pl.BlockSpec((1, tm, hp), lambda b, i: (b, i, 0)),
            pl.BlockSpec((1, tm, hp), lambda b, i: (b, i, 0)),
        ],
        out_shape=[jax.ShapeDtypeStruct((B, L, hp), BF16)] * 3,
        compiler_params=_cparams(("parallel", "parallel"), 40),
        name="mla_project",
    )(dlat, cos_t, sin_t, q_norm_w.reshape(1, -1), kv_norm_w.reshape(1, -1), wq, wqr, wk, wv)


def _mla_attn_kernel(q_ref, k_ref, v_ref, o_ref):
    hss = [slice(h * MLA_HP, (h + 1) * MLA_HP) for h in range(q_ref.shape[2] // MLA_HP)]
    ss = [_dot_nt(q_ref[0, :, hs], k_ref[0, :, hs]) for hs in hss]
    es = [jnp.exp2(s - jnp.max(s, axis=-1, keepdims=True)).astype(BF16) for s in ss]
    ovs = [_bdot(e, v_ref[0, :, hs]) for e, hs in zip(es, hss)]
    outs = [ov[:, :MLA_DV] / ov[:, MLA_DV:MLA_DV + 1] for ov in ovs]
    o_ref[0] = jnp.concatenate(outs, axis=1).astype(BF16)


def mla_attention(q, k, v, *, tq=256):
    B, L, _ = q.shape
    group = 4
    return pl.pallas_call(
        _mla_attn_kernel,
        grid=(B, MLA_H // group, L // tq),
        in_specs=[
            pl.BlockSpec((1, tq, group * MLA_HP), lambda b, h, i: (b, i, h)),
            pl.BlockSpec((1, L, group * MLA_HP), lambda b, h, i: (b, 0, h)),
            pl.BlockSpec((1, L, group * MLA_HP), lambda b, h, i: (b, 0, h)),
        ],
        out_specs=pl.BlockSpec((1, tq, group * MLA_DV), lambda b, h, i: (b, i, h)),
        out_shape=jax.ShapeDtypeStruct((B, L, MLA_H * MLA_DV), BF16),
        compiler_params=_cparams(("parallel", "parallel", "parallel"), 56),
        name="mla_attention",
    )(q, k, v)


def _rot_half_cols(w):
    half = MLA_ROPE // 2
    return jnp.concatenate([-w[..., half:], w[..., :half]], axis=-1)


def mla_layer(x, mod, nw, p):
    (w_down, q_norm_w, w_uq, kv_norm_w, w_ukv, w_out) = p
    L = x.shape[1]
    pad_r = MLA_HP - MLA_NOPE - MLA_ROPE
    base = MLA_QRANK + MLA_KVRANK
    w_rope = w_down[:, base:]
    zl = jnp.zeros((D, MLA_NOPE), F32)
    zr = jnp.zeros((D, pad_r), F32)
    w_dext = jnp.concatenate([w_down[:, :base], zl, w_rope, zr, zl, _rot_half_cols(w_rope), zr], axis=1)
    wq3 = w_uq.reshape(MLA_QRANK, MLA_H, MLA_NOPE + MLA_ROPE)
    zq = jnp.zeros((MLA_QRANK, MLA_H, pad_r), F32)
    wq = jnp.concatenate([wq3, zq], axis=-1).reshape(MLA_QRANK, -1).astype(BF16)
    wqr = jnp.concatenate([jnp.zeros((MLA_QRANK, MLA_H, MLA_NOPE), F32), _rot_half_cols(wq3[..., MLA_NOPE:]), zq],
                          axis=-1).reshape(MLA_QRANK, -1).astype(BF16)
    wkv3 = w_ukv.reshape(MLA_KVRANK, MLA_H, MLA_NOPE + MLA_DV)
    wk = jnp.concatenate([wkv3[..., :MLA_NOPE], jnp.zeros((MLA_KVRANK, MLA_H, MLA_HP - MLA_NOPE), F32)],
                         axis=-1).reshape(MLA_KVRANK, -1).astype(BF16)
    wv = jnp.concatenate([wkv3[..., MLA_NOPE:], jnp.zeros((MLA_KVRANK, MLA_H, MLA_HP - MLA_DV), F32)],
                         axis=-1).reshape(MLA_KVRANK, -1).astype(BF16)
    inv = ROPE_THETA ** (-jnp.arange(0, MLA_ROPE, 2, dtype=F32) / MLA_ROPE)
    ang = jnp.arange(L, dtype=F32)[:, None] * inv[None, :]
    cos, sin = jnp.cos(ang), jnp.sin(ang)
    cos_t = jnp.concatenate([jnp.ones((L, MLA_NOPE), F32), cos, cos, jnp.zeros((L, pad_r), F32)], axis=1)
    sin_t = jnp.concatenate([jnp.zeros((L, MLA_NOPE), F32), sin, sin, jnp.zeros((L, pad_r), F32)], axis=1)

    dlat = norm_mod_matmul(x, mod, nw, w_dext.astype(BF16))
    q, k, v = mla_project(dlat, cos_t, sin_t, q_norm_w, kv_norm_w, wq, wqr, wk, wv)
    o = mla_attention(q, k, v)
    return matmul_gated_residual(o, x, mod, w_out.astype(BF16))


def encoder_trunk(x, c, ada_w, ada_b, norm_w, hy, gdn, swa, mla, ffn_w_gu, ffn_w_down, final_norm_w):
    mods = ada_modulation(c, ada_w, ada_b)
    layers = (hyena_layer, gdn_layer, swa_layer, mla_layer)
    params = (hy, gdn, swa, mla)
    for i in range(DEPTH):
        kind, j = i % 4, i // 4
        x = layers[kind](x, mods[i], norm_w[i, 0], [p[j] for p in params[kind]])
        x = ffn_block(x, mods[i], norm_w[i, 1], ffn_w_gu[i].astype(BF16), ffn_w_down[i].astype(BF16),
                      final_norm_w if i == DEPTH - 1 else None)
    return x


def kernel(x_prompt, x_sample, c_prompt, c_sample, ada_w, ada_b, norm_w, hy_w_in, hy_conv_w, hy_conv_b, hy_filt_w1, hy_filt_b1, hy_filt_freq1, hy_filt_w2, hy_filt_b2, hy_filt_freq2, hy_filt_w3, hy_skip, hy_w_out, gdn_w_in, gdn_conv_w, gdn_conv_b, gdn_w_ab, gdn_a_log, gdn_dt_bias, gdn_norm_w, gdn_w_out, swa_w_qkv, swa_sink, swa_w_out, mla_w_down, mla_q_norm_w, mla_w_uq, mla_kv_norm_w, mla_w_ukv, mla_w_out, ffn_w_gu, ffn_w_down, final_norm_w):
    hy = (hy_w_in, hy_conv_w, hy_conv_b, hy_filt_w1, hy_filt_b1, hy_filt_freq1,
          hy_filt_w2, hy_filt_b2, hy_filt_freq2, hy_filt_w3, hy_skip, hy_w_out)
    gdn = (gdn_w_in, gdn_conv_w, gdn_conv_b, gdn_w_ab, gdn_a_log, gdn_dt_bias, gdn_norm_w, gdn_w_out)
    swa = (swa_w_qkv, swa_sink, swa_w_out)
    mla = (mla_w_down, mla_q_norm_w, mla_w_uq, mla_kv_norm_w, mla_w_ukv, mla_w_out)
    args = (ada_w, ada_b, norm_w, hy, gdn, swa, mla, ffn_w_gu, ffn_w_down, final_norm_w)
    return (encoder_trunk(x_prompt, c_prompt, *args), encoder_trunk(x_sample, c_sample, *args))
```

```python
import functools
import math

import jax
import jax.numpy as jnp
from jax import lax
from jax.experimental import pallas as pl
from jax.experimental.pallas import tpu as pltpu

F32 = jnp.float32
BF16 = jnp.bfloat16
HI = lax.Precision.HIGHEST

D = 1024
DEPTH = 4
EPS = 1e-6
D_FF = 2816

HY_BANDS = 16
HY_FILT = 64
HY_TARGET = 1e-2
HY_FAST = 0.3
HY_SLOW = 1.5

GDN_HK = 8
GDN_HV = 16
GDN_DK = 128
GDN_DV = 128
GDN_CHUNK = 64
GDN_QKV = 2 * GDN_HK * GDN_DK + GDN_HV * GDN_DV
GDN_IN = GDN_QKV + GDN_HV * GDN_DV

SWA_HQ = 16
SWA_HKV = 4
SWA_G = SWA_HQ // SWA_HKV
SWA_DH = 64
SWA_WINDOW = 128
SWA_BLOCK = 128
SWA_STEP_BLOCKS = 4

MLA_H = 16
MLA_NOPE = 64
MLA_ROPE = 32
MLA_DV = 64
MLA_QRANK = 256
MLA_KVRANK = 256
MLA_HP = 128
ROPE_THETA = 10000.0

LANE = 128
LOG2E = math.log2(math.e)
MIB = 2 ** 20
NEG = -1e30


def _cparams(sem, vmem_mb):
    return pltpu.CompilerParams(dimension_semantics=sem, vmem_limit_bytes=vmem_mb * MIB)


def _resident(shape):
    nd = len(shape)
    return pl.BlockSpec(shape, lambda *_: (0,) * nd, pipeline_mode=pl.Buffered(1))


def _bdot(a, b):
    return jnp.dot(a, b, preferred_element_type=F32)


def _dot_nt(a, b, precision=None):
    return lax.dot_general(a, b, (((1,), (1,)), ((), ())), preferred_element_type=F32, precision=precision)


def _dot_tn(a, b, precision=None):
    return lax.dot_general(a, b, (((0,), (0,)), ((), ())), preferred_element_type=F32, precision=precision)


def _split_bf16(x):
    hi = x.astype(BF16)
    return hi, (x - hi.astype(F32)).astype(BF16)


def _rms(x):
    return x * lax.rsqrt(jnp.mean(x * x, axis=-1, keepdims=True) + EPS)


def _norm_mod(x, nw, mod_ref, sh_row, sc_row):
    return _rms(x) * nw * (1.0 + mod_ref[0, sc_row:sc_row + 1, :]) + mod_ref[0, sh_row:sh_row + 1, :]


def _ada_kernel(c_ref, w_ref, b_ref, o_ref):
    c = c_ref[...]
    ca = c * jax.nn.sigmoid(c)
    o_ref[0] = jnp.dot(ca, w_ref[0], preferred_element_type=F32, precision=HI) + b_ref[0]


def ada_modulation(c, ada_w, ada_b):
    B = c.shape[0]
    tn = 1024
    out = pl.pallas_call(
        _ada_kernel,
        grid=(DEPTH, 6 * D // tn),
        in_specs=[
            pl.BlockSpec((B, D), lambda i, j: (0, 0)),
            pl.BlockSpec((1, D, tn), lambda i, j: (i, 0, j)),
            pl.BlockSpec((1, 1, tn), lambda i, j: (i, 0, j)),
        ],
        out_specs=pl.BlockSpec((1, B, tn), lambda i, j: (i, 0, j)),
        out_shape=jax.ShapeDtypeStruct((DEPTH, B, 6 * D), F32),
        compiler_params=_cparams(("parallel", "parallel"), 32),
        name="ada_modulation",
    )(c, ada_w, ada_b.reshape(DEPTH, 1, 6 * D))
    return out.reshape(DEPTH, B, 6, D)


def _nmm_kernel(x_ref, mod_ref, nw_ref, w_ref, o_ref, h_scr, *, sh_row, sc_row, tn):
    h_scr[...] = _norm_mod(x_ref[0], nw_ref[...], mod_ref, sh_row, sc_row).astype(BF16)
    n = w_ref.shape[1]
    for c0 in range(0, n, tn):
        c1 = min(c0 + tn, n)
        o_ref[0, :, c0:c1] = _bdot(h_scr[...], w_ref[:, c0:c1]).astype(o_ref.dtype)


def norm_mod_matmul(x, mod, nw, w, *, sh_row=0, sc_row=1, tm=512, tn=512, out_dtype=F32):
    B, L, _ = x.shape
    N = w.shape[1]
    return pl.pallas_call(
        functools.partial(_nmm_kernel, sh_row=sh_row, sc_row=sc_row, tn=tn),
        grid=(B, L // tm),
        in_specs=[
            pl.BlockSpec((1, tm, D), lambda b, i: (b, i, 0)),
            pl.BlockSpec((1, 6, D), lambda b, i: (b, 0, 0)),
            _resident((1, D)),
            _resident((D, N)),
        ],
        out_specs=pl.BlockSpec((1, tm, N), lambda b, i: (b, i, 0)),
        out_shape=jax.ShapeDtypeStruct((B, L, N), out_dtype),
        scratch_shapes=[pltpu.VMEM((tm, D), BF16)],
        compiler_params=_cparams(("parallel", "parallel"), 48),
        name="norm_mod_matmul",
    )(x, mod, nw.reshape(1, D), w)


def _mmres_kernel(a_ref, x_ref, mod_ref, w_ref, o_ref, *, g_row):
    y = _bdot(a_ref[0].astype(BF16), w_ref[...])
    o_ref[0] = x_ref[0] + mod_ref[0, g_row:g_row + 1, :] * y


def matmul_gated_residual(a, x, mod, w, *, g_row=2, tm=512):
    B, L, K = a.shape
    return pl.pallas_call(
        functools.partial(_mmres_kernel, g_row=g_row),
        grid=(B, L // tm),
        in_specs=[
            pl.BlockSpec((1, tm, K), lambda b, i: (b, i, 0)),
            pl.BlockSpec((1, tm, D), lambda b, i: (b, i, 0)),
            pl.BlockSpec((1, 6, D), lambda b, i: (b, 0, 0)),
            _resident((K, D)),
        ],
        out_specs=pl.BlockSpec((1, tm, D), lambda b, i: (b, i, 0)),
        out_shape=jax.ShapeDtypeStruct((B, L, D), F32),
        compiler_params=_cparams(("parallel", "parallel"), 40),
        name="matmul_gated_residual",
    )(a, x, mod, w)


def _ffn_kernel(x_ref, mod_ref, nw_ref, wgu_ref, wd_ref, *rest, tf, final):
    fnw_ref = rest[0] if final else None
    o_ref, h_scr, act_scr = rest[-3:]
    x = x_ref[0]
    h_scr[...] = _norm_mod(x, nw_ref[...], mod_ref, 3, 4).astype(BF16)
    for c0 in range(0, D_FF, tf):
        gate = _bdot(h_scr[...], wgu_ref[:, c0:c0 + tf])
        up = _bdot(h_scr[...], wgu_ref[:, D_FF + c0:D_FF + c0 + tf])
        act_scr[:, c0:c0 + tf] = (gate * jax.nn.sigmoid(gate) * up).astype(BF16)
    y = x + mod_ref[0, 5:6, :] * _bdot(act_scr[...], wd_ref[...])
    o_ref[0] = _rms(y) * fnw_ref[...] if final else y


def ffn_block(x, mod, nw, w_gu, w_down, final_nw=None, *, tm=512, tf=256):
    B, L, _ = x.shape
    final = final_nw is not None
    return pl.pallas_call(
        functools.partial(_ffn_kernel, tf=tf, final=final),
        grid=(B, L // tm),
        in_specs=[
            pl.BlockSpec((1, tm, D), lambda b, i: (b, i, 0)),
            pl.BlockSpec((1, 6, D), lambda b, i: (b, 0, 0)),
            _resident((1, D)),
            _resident((D, 2 * D_FF)),
            _resident((D_FF, D)),
        ] + ([_resident((1, D))] if final else []),
        out_specs=pl.BlockSpec((1, tm, D), lambda b, i: (b, i, 0)),
        out_shape=jax.ShapeDtypeStruct((B, L, D), F32),
        scratch_shapes=[pltpu.VMEM((tm, D), BF16), pltpu.VMEM((tm, D_FF), BF16)],
        compiler_params=_cparams(("parallel", "parallel"), 48),
        name="ffn_block",
    )(x, mod, nw.reshape(1, D), w_gu, w_down, *([final_nw.reshape(1, D)] if final else []))


def _hy_filter_kernel(freq_ref, w1t_ref, w1c_ref, w1s_ref, b1_ref, f1_ref, w2_ref, b2_ref, f2_ref, w3_ref,
                      rate_ref, hs_ref, hd_ref, *, L, tl):
    pos = (pl.program_id(0) * tl + lax.broadcasted_iota(jnp.int32, (tl, 1), 0)).astype(F32)
    t = pos / max(L - 1, 1)
    ang = freq_ref[...] * (2.0 * math.pi / L) * pos
    z = (t * w1t_ref[...] + jnp.dot(jnp.cos(ang), w1c_ref[...], preferred_element_type=F32, precision=HI)
         - jnp.dot(jnp.sin(ang), w1s_ref[...], preferred_element_type=F32, precision=HI) + b1_ref[...])
    z = jnp.sin(f1_ref[...] * z)
    z = jnp.sin(f2_ref[...] * (jnp.dot(z, w2_ref[...], preferred_element_type=F32, precision=HI) + b2_ref[...]))
    z = jnp.dot(z, w3_ref[...], preferred_element_type=F32, precision=HI)
    window = jnp.exp(-t * rate_ref[...])
    hf = z[:, :D] * window
    hb = jnp.where(pos == 0.0, 0.0, z[:, D:] * window)
    hs_ref[...] = hf + hb
    hd_ref[...] = hb - hf


def hyena_filter_taps(L, fw1, fb1, ff1, fw2, fb2, ff2, fw3):
    tl = 256
    freqs = jnp.linspace(1e-4, HY_BANDS - 1, HY_BANDS, dtype=F32).reshape(1, HY_BANDS)
    rates = jnp.abs(jnp.linspace(math.log(HY_TARGET) / HY_SLOW, math.log(HY_TARGET) / HY_FAST, D, dtype=F32))
    small = [freqs, fw1[0:1], fw1[1:1 + HY_BANDS], fw1[1 + HY_BANDS:], fb1.reshape(1, -1), ff1.reshape(1, -1),
             fw2, fb2.reshape(1, -1), ff2.reshape(1, -1), fw3, rates.reshape(1, D)]
    return pl.pallas_call(
        functools.partial(_hy_filter_kernel, L=L, tl=tl),
        grid=(L // tl,),
        in_specs=[_resident(a.shape) for a in small],
        out_specs=[pl.BlockSpec((tl, D), lambda i: (i, 0))] * 2,
        out_shape=[jax.ShapeDtypeStruct((L, D), F32)] * 2,
        compiler_params=_cparams(("parallel",), 32),
        name="hyena_filter_taps",
    )(*small)


def _hy_filter_dft_kernel(hs_ref, hd_ref, cf_ref, sf_ref, kre_ref, kim_ref, knyq_ref, *, L, fk):
    k = pl.program_id(1)
    hs = hs_ref[...]
    inv_n = 1.0 / (2 * L)
    row = k * fk + lax.broadcasted_iota(jnp.int32, (fk, 1), 0)
    wk = jnp.where(row == 0, inv_n, 2.0 * inv_n)
    kre_ref[...] = wk * _bdot(cf_ref[...], hs.astype(BF16))
    kim_ref[...] = wk * _bdot(sf_ref[...], hd_ref[...].astype(BF16))

    @pl.when(k == 0)
    def _():
        t = lax.broadcasted_iota(jnp.int32, hs.shape, 0)
        sgn = (1 - 2 * (t & 1)).astype(F32)
        knyq_ref[...] = inv_n * jnp.sum(hs * sgn, axis=0, keepdims=True)


def hyena_filter_dft(hs, hd, cmat, smat, *, ct=256, fk=256):
    L = hs.shape[0]
    return pl.pallas_call(
        functools.partial(_hy_filter_dft_kernel, L=L, fk=fk),
        grid=(D // ct, L // fk),
        in_specs=[
            pl.BlockSpec((L, ct), lambda j, k: (0, j)),
            pl.BlockSpec((L, ct), lambda j, k: (0, j)),
            pl.BlockSpec((fk, L), lambda j, k: (k, 0)),
            pl.BlockSpec((fk, L), lambda j, k: (k, 0)),
        ],
        out_specs=[
            pl.BlockSpec((fk, ct), lambda j, k: (k, j)),
            pl.BlockSpec((fk, ct), lambda j, k: (k, j)),
            pl.BlockSpec((1, ct), lambda j, k: (0, j)),
        ],
        out_shape=[jax.ShapeDtypeStruct((L, D), F32), jax.ShapeDtypeStruct((L, D), F32),
                   jax.ShapeDtypeStruct((1, D), F32)],
        compiler_params=_cparams(("parallel", "arbitrary"), 48),
        name="hyena_filter_dft",
    )(hs, hd, cmat, smat)


def _hy_pre_kernel(u0_ref, u1_ref, u2_ref, w0_ref, w1_ref, w2_ref, b0_ref, b1_ref, b2_ref, x0_ref, vg_ref):
    L = u0_ref.shape[1]
    t = lax.broadcasted_iota(jnp.int32, (L, 1), 0)
    first, last = t == 0, t == L - 1

    def dwconv(u_ref, w_ref, b_ref):
        u = u_ref[0].astype(F32)
        prev = jnp.where(first, 0.0, pltpu.roll(u, 1, 0))
        nxt = jnp.where(last, 0.0, pltpu.roll(u, L - 1, 0))
        return b_ref[...] + w_ref[0:1, :] * prev + w_ref[1:2, :] * u + w_ref[2:3, :] * nxt

    x0_ref[0] = dwconv(u0_ref, w0_ref, b0_ref).astype(BF16)
    vg_ref[0] = (dwconv(u2_ref, w2_ref, b2_ref) * dwconv(u1_ref, w1_ref, b1_ref)).astype(BF16)


def hyena_pre(u, conv_w, conv_b, *, ct=128):
    B, L, _ = u.shape
    nj = D // ct
    conv_b = conv_b.reshape(1, 3 * D)
    ublk = [pl.BlockSpec((1, L, ct), lambda b, j, s=s: (b, 0, s * nj + j)) for s in range(3)]
    wblk = [pl.BlockSpec((3, ct), lambda b, j, s=s: (0, s * nj + j)) for s in range(3)]
    bblk = [pl.BlockSpec((1, ct), lambda b, j, s=s: (0, s * nj + j)) for s in range(3)]
    oblk = pl.BlockSpec((1, L, ct), lambda b, j: (b, 0, j))
    return pl.pallas_call(
        _hy_pre_kernel,
        grid=(B, nj),
        in_specs=ublk + wblk + bblk,
        out_specs=[oblk, oblk],
        out_shape=[jax.ShapeDtypeStruct((B, L, D), BF16)] * 2,
        compiler_params=_cparams(("parallel", "parallel"), 48),
        name="hyena_pre",
    )(u, u, u, conv_w, conv_w, conv_w, conv_b, conv_b, conv_b)


def _hy_conv_kernel(vb_ref, cf_ref, sf_ref, ci_ref, si_ref, kre_ref, kim_ref, knyq_ref, o_ref):
    k = pl.program_id(2)
    vb = vb_ref[0]

    @pl.when(k == 0)
    def _():
        t = lax.broadcasted_iota(jnp.int32, vb.shape, 0)
        sgn = (1 - 2 * (t & 1)).astype(F32)
        vnyq = jnp.sum(vb.astype(F32) * sgn, axis=0, keepdims=True)
        o_ref[0] = sgn * (vnyq * knyq_ref[...])

    vre = _bdot(cf_ref[...], vb)
    vim = _bdot(sf_ref[...], vb)
    kre, kim = kre_ref[...], kim_ref[...]
    yre = vre * kre + vim * kim
    nyim = vim * kre - vre * kim
    o_ref[0] += _bdot(ci_ref[...], yre.astype(BF16)) + _bdot(si_ref[...], nyim.astype(BF16))


def hyena_long_conv(vb, cmat, smat, kre, kim, knyq, *, ct=512, fk=256):
    B, L, _ = vb.shape
    return pl.pallas_call(
        _hy_conv_kernel,
        grid=(B, D // ct, L // fk),
        in_specs=[
            pl.BlockSpec((1, L, ct), lambda b, j, k: (b, 0, j)),
            pl.BlockSpec((fk, L), lambda b, j, k: (k, 0)),
            pl.BlockSpec((fk, L), lambda b, j, k: (k, 0)),
            pl.BlockSpec((L, fk), lambda b, j, k: (0, k)),
            pl.BlockSpec((L, fk), lambda b, j, k: (0, k)),
            pl.BlockSpec((fk, ct), lambda b, j, k: (k, j)),
            pl.BlockSpec((fk, ct), lambda b, j, k: (k, j)),
            pl.BlockSpec((1, ct), lambda b, j, k: (0, j)),
        ],
        out_specs=pl.BlockSpec((1, L, ct), lambda b, j, k: (b, 0, j)),
        out_shape=jax.ShapeDtypeStruct((B, L, D), F32),
        compiler_params=_cparams(("parallel", "parallel", "arbitrary"), 58),
        name="hyena_long_conv",
    )(vb, cmat, smat, cmat, smat, kre, kim, knyq)


def _hy_out_kernel(y_ref, vg_ref, x0_ref, skip_ref, x_ref, mod_ref, w_ref, o_ref):
    a = ((y_ref[0] + vg_ref[0] * skip_ref[...]) * x0_ref[0]).astype(BF16)
    o_ref[0] = x_ref[0] + mod_ref[0, 2:3, :] * _bdot(a, w_ref[...])


def hyena_out(y, vg, x0, skip, x, mod, w, *, tm=512):
    B, L, _ = x.shape
    tok = pl.BlockSpec((1, tm, D), lambda b, i: (b, i, 0))
    return pl.pallas_call(
        _hy_out_kernel,
        grid=(B, L // tm),
        in_specs=[tok, tok, tok, _resident((1, D)), tok, pl.BlockSpec((1, 6, D), lambda b, i: (b, 0, 0)),
                  _resident((D, D))],
        out_specs=tok,
        out_shape=jax.ShapeDtypeStruct((B, L, D), F32),
        compiler_params=_cparams(("parallel", "parallel"), 40),
        name="hyena_out",
    )(y, vg, x0, skip.reshape(1, D), x, mod, w)


def dft_tables(L):
    step = 64
    t = jnp.arange(L, dtype=jnp.int32)[None, :]

    def cs(rows):
        ang = ((rows[:, None] * t) % (2 * L)).astype(F32) * (math.pi / L)
        return jnp.cos(ang), jnp.sin(ang)

    ca, sa = cs(jnp.arange(0, L, step, dtype=jnp.int32))
    cb, sb = cs(jnp.arange(step, dtype=jnp.int32))
    cos_t = ca[:, None, :] * cb[None] - sa[:, None, :] * sb[None]
    sin_t = sa[:, None, :] * cb[None] + ca[:, None, :] * sb[None]
    return cos_t.reshape(L, L).astype(BF16), sin_t.reshape(L, L).astype(BF16)


def hyena_layer(x, mod, nw, p):
    (w_in, conv_w, conv_b, fw1, fb1, ff1, fw2, fb2, ff2, fw3, skip, w_out) = p
    L = x.shape[1]
    cmat, smat = dft_tables(L)
    hs, hd = hyena_filter_taps(L, fw1, fb1, ff1, fw2, fb2, ff2, fw3)
    kre, kim, knyq = hyena_filter_dft(hs, hd, cmat, smat)
    u = norm_mod_matmul(x, mod, nw, w_in.astype(BF16), out_dtype=BF16)
    x0, vg = hyena_pre(u, conv_w, conv_b)
    y = hyena_long_conv(vg, cmat, smat, kre, kim, knyq, ct=512, fk=512 if L <= 2048 else 256)
    return hyena_out(y, vg, x0, skip, x, mod, w_out.astype(BF16))


def _gdn_in_kernel(x_ref, mod_ref, nw_ref, w_ref, wab_hi_ref, wab_lo_ref, o_ref, ab_ref, abt_ref, h_scr, *, tn):
    h = _norm_mod(x_ref[0], nw_ref[...], mod_ref, 0, 1)
    h_hi, h_lo = _split_bf16(h)
    h_scr[...] = h_hi
    for c0 in range(0, GDN_IN, tn):
        o_ref[0, :, c0:c0 + tn] = _bdot(h_scr[...], w_ref[:, c0:c0 + tn]).astype(BF16)
    ab = _bdot(h_hi, wab_hi_ref[...]) + _bdot(h_lo, wab_hi_ref[...]) + _bdot(h_hi, wab_lo_ref[...])
    ab_ref[0] = ab
    abt_ref[0] = ab.T


def _pair_major(t):
    lead = t.shape[:-1]
    t = t.reshape(lead + (2, GDN_HK, GDN_HV // GDN_HK))
    return jnp.swapaxes(t, -1, -2).reshape(lead + (2 * GDN_HV,))


def gdn_in_proj(x, mod, nw, w_in, w_ab, *, tm=256, tn=512):
    B, L, _ = x.shape
    wab = jnp.concatenate([_pair_major(w_ab[:, :2 * GDN_HV]), _pair_major(w_ab[:, 2 * GDN_HV:])], axis=1)
    wab_hi, wab_lo = _split_bf16(jnp.pad(wab, ((0, 0), (0, LANE - wab.shape[1]))))
    return pl.pallas_call(
        functools.partial(_gdn_in_kernel, tn=tn),
        grid=(B, L // tm),
        in_specs=[
            pl.BlockSpec((1, tm, D), lambda b, i: (b, i, 0)),
            pl.BlockSpec((1, 6, D), lambda b, i: (b, 0, 0)),
            _resident((1, D)),
            _resident((D, GDN_IN)),
            _resident((D, LANE)),
            _resident((D, LANE)),
        ],
        out_specs=[
            pl.BlockSpec((1, tm, GDN_IN), lambda b, i: (b, i, 0)),
            pl.BlockSpec((1, tm, LANE), lambda b, i: (b, i, 0)),
            pl.BlockSpec((1, LANE, tm), lambda b, i: (b, 0, i)),
        ],
        out_shape=[jax.ShapeDtypeStruct((B, L, GDN_IN), BF16), jax.ShapeDtypeStruct((B, L, LANE), F32),
                   jax.ShapeDtypeStruct((B, LANE, L), F32)],
        scratch_shapes=[pltpu.VMEM((tm, D), BF16)],
        compiler_params=_cparams(("parallel", "parallel"), 48),
        name="gdn_in_proj",
    )(x, mod, nw.reshape(1, D), w_in.astype(BF16), wab_hi, wab_lo)


CONV_ROWS = 64
CONV_HALO = 16


def _dwconv_rows(u_ref, w_ref, b_ref, i):
    L = u_ref.shape[1]
    R, G = CONV_ROWS, CONV_HALO
    r0 = pl.multiple_of(i * R, R)
    u = u_ref[0, pl.ds(r0, R), :].astype(F32)
    lo = pl.multiple_of(jnp.maximum(r0 - G, 0), G)
    hi = pl.multiple_of(jnp.minimum(r0 + R, L - G), G)
    before = jnp.where(i == 0, 0.0, u_ref[0, pl.ds(lo, G), :].astype(F32)[G - 1:G])
    after = jnp.where(i == L // R - 1, 0.0, u_ref[0, pl.ds(hi, G), :].astype(F32)[0:1])
    t = lax.broadcasted_iota(jnp.int32, (R, 1), 0)
    prev = jnp.where(t == 0, before, pltpu.roll(u, 1, 0))
    nxt = jnp.where(t == R - 1, after, pltpu.roll(u, R - 1, 0))
    return b_ref[...] + w_ref[0:1, :] * prev + w_ref[1:2, :] * u + w_ref[2:3, :] * nxt


def _gdn_pre_kernel(u_ref, w_ref, b_ref, o_ref):
    L, ct = u_ref.shape[1], u_ref.shape[2]
    j = pl.program_id(1)
    nq_tiles = GDN_HK * GDN_DK // ct

    def run(normalise):
        scale = jnp.where(j < nq_tiles, GDN_DK ** -0.5, 1.0)

        def body(i, carry):
            y = _dwconv_rows(u_ref, w_ref, b_ref, i)
            y = y * jax.nn.sigmoid(y)
            rows = pl.ds(pl.multiple_of(i * CONV_ROWS, CONV_ROWS), CONV_ROWS)
            if normalise:
                for h in range(ct // GDN_DK):
                    yh = y[:, h * GDN_DK:(h + 1) * GDN_DK]
                    o_ref[0, rows, h * GDN_DK:(h + 1) * GDN_DK] = (yh * (
                        lax.rsqrt(jnp.sum(yh * yh, axis=-1, keepdims=True) + EPS) * scale)).astype(BF16)
            else:
                o_ref[0, rows, :] = y.astype(BF16)
            return carry

        lax.fori_loop(0, L // CONV_ROWS, body, 0, unroll=4)

    @pl.when(j >= 2 * nq_tiles)
    def _():
        run(False)

    @pl.when(j < 2 * nq_tiles)
    def _():
        run(True)


def gdn_pre(proj, conv_w, conv_b, *, ct=256):
    B, L, _ = proj.shape
    return pl.pallas_call(
        _gdn_pre_kernel,
        grid=(B, GDN_QKV // ct),
        in_specs=[
            pl.BlockSpec((1, L, ct), lambda b, j: (b, 0, j)),
            pl.BlockSpec((3, ct), lambda b, j: (0, j)),
            pl.BlockSpec((1, ct), lambda b, j: (0, j)),
        ],
        out_specs=pl.BlockSpec((1, L, ct), lambda b, j: (b, 0, j)),
        out_shape=jax.ShapeDtypeStruct((B, L, GDN_QKV), BF16),
        compiler_params=_cparams(("parallel", "parallel"), 40),
        name="gdn_pre",
    )(proj, conv_w, conv_b.reshape(1, GDN_QKV))


def _softplus(x):
    return jnp.maximum(x, 0.0) + jnp.log1p(jnp.exp(-jnp.abs(x)))


def _gdn_gates_kernel(ab_ref, abt_ref, alog_c_ref, dtb_c_ref, alog_r_ref, dtb_r_ref, col_ref, row_ref, *, tl):
    C = GDN_CHUNK
    H = GDN_HV
    i = lax.broadcasted_iota(jnp.int32, (C, C), 0)
    j = lax.broadcasted_iota(jnp.int32, (C, C), 1)
    lower = (i >= j).astype(F32)
    upper = (i <= j).astype(F32)
    P = GDN_HK
    ab = ab_ref[0]
    g_c = -jnp.exp(alog_c_ref[...]) * _softplus(ab[:, :2 * H] + dtb_c_ref[...])
    logbeta_c = -_softplus(-ab[:, 2 * H:4 * H])
    abt = abt_ref[0]
    g_r = -jnp.exp(alog_r_ref[...]) * _softplus(abt[:2 * H, :] + dtb_r_ref[...])
    beta_r = jax.nn.sigmoid(abt[2 * H:4 * H, :])
    zc = jnp.zeros((C, LANE - 4 * H), F32)

    def pack(x, swap):
        return jnp.concatenate([x[P:], x[:P]] if swap else [x[:P], x[P:]], axis=1)

    for c in range(tl // C):
        gch = g_c[c * C:(c + 1) * C, :]
        pre = jnp.dot(lower, gch, preferred_element_type=F32, precision=HI)
        suf = jnp.dot(upper, gch, preferred_element_type=F32, precision=HI)
        tot = jnp.sum(gch, axis=0, keepdims=True)
        grc = g_r[:, c * C:(c + 1) * C]
        pre_r = jnp.dot(grc, upper, preferred_element_type=F32, precision=HI)
        suf_r = jnp.dot(grc, lower, preferred_element_type=F32, precision=HI)
        tot_r = jnp.sum(grc, axis=1, keepdims=True)
        for d in range(2):
            hs = slice(d * H, (d + 1) * H)
            gc = (pre if d == 0 else suf)[:, hs]
            lb = logbeta_c[c * C:(c + 1) * C, hs]
            col_ref[0, d, c * C:(c + 1) * C, :] = jnp.concatenate(
                [gc, gc + lb, jnp.exp(gc), jnp.exp(tot[:, hs] - gc), zc], axis=1)
            gr = (pre_r if d == 0 else suf_r)[hs, :]
            br = beta_r[hs, c * C:(c + 1) * C]
            row_ref[0, d, c] = jnp.concatenate(
                [pack(gr, False), pack(br, True), pack(br * jnp.exp(gr), True),
                 jnp.broadcast_to(jnp.exp(tot_r[hs, :]), (H, LANE))], axis=0)


GDN_ROWS = 3 * GDN_HK + GDN_HV


def gdn_gates(ab, abt, a_log, dt_bias, *, tl=512):
    B, L, _ = ab.shape
    N = L // GDN_CHUNK
    H2 = 2 * GDN_HV
    return pl.pallas_call(
        functools.partial(_gdn_gates_kernel, tl=tl),
        grid=(B, L // tl),
        in_specs=[
            pl.BlockSpec((1, tl, LANE), lambda b, i: (b, i, 0)),
            pl.BlockSpec((1, LANE, tl), lambda b, i: (b, 0, i)),
            _resident((1, H2)), _resident((1, H2)), _resident((H2, 1)), _resident((H2, 1)),
        ],
        out_specs=[
            pl.BlockSpec((1, 2, tl, LANE), lambda b, i: (b, 0, i, 0)),
            pl.BlockSpec((1, 2, tl // GDN_CHUNK, GDN_ROWS, LANE), lambda b, i: (b, 0, i, 0, 0)),
        ],
        out_shape=[jax.ShapeDtypeStruct((B, 2, L, LANE), F32),
                   jax.ShapeDtypeStruct((B, 2, N, GDN_ROWS, LANE), F32)],
        compiler_params=_cparams(("parallel", "parallel"), 32),
        name="gdn_gates",
    )(ab, abt, _pair_major(a_log.reshape(1, H2)), _pair_major(dt_bias.reshape(1, H2)),
      _pair_major(a_log.reshape(H2)).reshape(H2, 1), _pair_major(dt_bias.reshape(H2)).reshape(H2, 1))


def _gdn_chunk_kernel(q_ref, k_ref, v_ref, col_ref, row_ref, o_ref, s_scr):
    C = GDN_CHUNK
    H = GDN_HV
    d = pl.program_id(1)
    n = pl.program_id(2)

    @pl.when(n == 0)
    def _():
        s_scr[...] = jnp.zeros_like(s_scr)

    P = GDN_HK
    W = 2 * GDN_DV
    sgn = 1 - 2 * d
    i = lax.broadcasted_iota(jnp.int32, (C, LANE), 0)
    lane = lax.broadcasted_iota(jnp.int32, (C, LANE), 1)
    order = (i - (lane & (C - 1))) * sgn
    incl = order >= 0
    strict = order > 0
    left = lane < C
    eye2 = (order == 0).astype(F32)
    col = col_ref[0, 0]
    row = row_ref[0, 0, 0]
    zb = jnp.zeros((C, LANE), BF16)
    zs = jnp.zeros((GDN_DK, GDN_DV), BF16)

    def col_pair(base, p):
        return jnp.where(left, col[:, base + p:base + p + 1], col[:, base + P + p:base + P + p + 1])

    def col_wide(base, p):
        return jnp.concatenate(
            [jnp.broadcast_to(col[:, base + e * P + p:base + e * P + p + 1], (C, GDN_DV)) for e in range(2)], axis=1)

    def block_diag(a, b, z):
        return jnp.concatenate([jnp.concatenate([a, z], axis=1), jnp.concatenate([z, b], axis=1)], axis=0)

    def anti_diag(a, b, z):
        return jnp.concatenate([jnp.concatenate([z, a], axis=1), jnp.concatenate([b, z], axis=1)], axis=0)

    ks =[k_ref[0, :, p * GDN_DK:(p + 1) * GDN_DK] for p in range(P)]
    qs = [q_ref[0, :, p * GDN_DK:(p + 1) * GDN_DK] for p in range(P)]
    grams = [_dot_nt(jnp.concatenate([ks[p], qs[p]], axis=0), jnp.concatenate([ks[p], ks[p]], axis=0))
             for p in range(P)]
    ms, intras = [], []
    for p in range(P):
        gc_j = row[p:p + 1, :]
        decay = jnp.exp(jnp.where(incl, col_pair(0, p) - gc_j, NEG))
        a_coef = jnp.exp(jnp.where(strict, col_pair(H, p) - gc_j, NEG))
        ms.append(-(grams[p][:C] * a_coef))
        intras.append((grams[p][C:] * decay).astype(BF16))
    tops = [jnp.where(left, m, eye2) for m in ms]
    bots = [jnp.where(left, eye2, m) for m in ms]
    for _ in range(6):
        for p in range(P):
            m_hi, m_lo = _split_bf16(ms[p])
            top_hi, top_lo = _split_bf16(tops[p])
            bot_hi, bot_lo = _split_bf16(bots[p])
            r2 = _bdot(jnp.concatenate([m_hi, m_lo], axis=0), block_diag(top_hi, bot_hi, zb))
            r = r2[:C] + r2[C:] + _bdot(m_hi, block_diag(top_lo, bot_lo, zb))
            r0, r1 = r[:, :LANE], r[:, LANE:]
            tops[p] = jnp.where(left, r0, tops[p] + r0)
            bots[p] = jnp.where(left, bots[p] + r1, r1)
            ms[p] = jnp.where(left, r0, r1)
    us, ws = [], []
    for p in range(P):
        t = jnp.where(left, bots[p], tops[p])
        v0 = v_ref[0, :, 2 * p * GDN_DV:(2 * p + 1) * GDN_DV]
        v1 = v_ref[0, :, (2 * p + 1) * GDN_DV:(2 * p + 2) * GDN_DV]
        us.append(_bdot((t * row[P + p:P + p + 1, :]).astype(BF16), anti_diag(v1, v0, zb)))
        ws.append(_bdot((t * row[2 * P + p:2 * P + p + 1, :]).astype(BF16), anti_diag(ks[p], ks[p], zb)))
    for p in range(P):
        s = s_scr[p]
        sb = s.astype(BF16)
        lhs = jnp.concatenate([ws[p].astype(BF16), jnp.concatenate([qs[p], qs[p]], axis=1)], axis=0)
        ws_qs = _bdot(lhs, block_diag(sb[:, :GDN_DV], sb[:, GDN_DV:], zs))
        v_new = us[p] - ws_qs[:C]
        vb = v_new.astype(BF16)
        o_ref[0, 0, :, p * W:(p + 1) * W] = (col_wide(2 * H, p) * ws_qs[C:] + _bdot(
            intras[p], block_diag(vb[:, :GDN_DV], vb[:, GDN_DV:], zb))).astype(BF16)
        g_end = jnp.concatenate([row[3 * P + p:3 * P + p + 1, :], row[4 * P + p:4 * P + p + 1, :]], axis=1)
        s_scr[p] = s * g_end + _dot_tn(ks[p], (v_new * col_wide(3 * H, p)).astype(BF16))


def gdn_chunk_scan(qkv, col, row):
    B, L, _ = qkv.shape
    C = GDN_CHUNK
    N = L // C
    nq = GDN_HK * GDN_DK

    def cidx(d, n):
        return n + d * (N - 1 - 2 * n)

    return pl.pallas_call(
        _gdn_chunk_kernel,
        grid=(B, 2, N),
        in_specs=[
            pl.BlockSpec((1, C, nq), lambda b, d, n: (b, cidx(d, n), 0)),
            pl.BlockSpec((1, C, nq), lambda b, d, n: (b, cidx(d, n), 1)),
            pl.BlockSpec((1, C, GDN_HV * GDN_DV), lambda b, d, n: (b, cidx(d, n), 1)),
            pl.BlockSpec((1, 1, C, LANE), lambda b, d, n: (b, d, cidx(d, n), 0)),
            pl.BlockSpec((1, 1, 1, GDN_ROWS, LANE), lambda b, d, n: (b, d, cidx(d, n), 0, 0)),
        ],
        out_specs=pl.BlockSpec((1, 1, C, GDN_HV * GDN_DV), lambda b, d, n: (b, d, cidx(d, n), 0)),
        out_shape=jax.ShapeDtypeStruct((B, 2, L, GDN_HV * GDN_DV), BF16),
        scratch_shapes=[pltpu.VMEM((GDN_HK, GDN_DK, 2 * GDN_DV), F32)],
        compiler_params=_cparams(("parallel", "parallel", "arbitrary"), 32),
        name="gdn_chunk_scan",
    )(qkv, qkv, qkv, col, row)


def _gdn_out_kernel(o_ref, z_ref, nw_ref, x_ref, mod_ref, w_ref, out_ref, a_scr):
    o = o_ref[0, 0].astype(F32) + o_ref[0, 1].astype(F32)
    z = z_ref[0].astype(F32)
    gate = z * jax.nn.sigmoid(z)
    for h in range(GDN_HV):
        hs = slice(h * GDN_DV, (h + 1) * GDN_DV)
        a_scr[:, hs] = (_rms(o[:, hs]) * nw_ref[...] * gate[:, hs]).astype(BF16)
    out_ref[0] = x_ref[0] + mod_ref[0, 2:3, :] * _bdot(a_scr[...], w_ref[...])


def gdn_out(o2, proj, norm_w, x, mod, w, *, tm=256):
    B, L, _ = x.shape
    hd = GDN_HV * GDN_DV
    return pl.pallas_call(
        _gdn_out_kernel,
        grid=(B, L // tm),
        in_specs=[
            pl.BlockSpec((1, 2, tm, hd), lambda b, i: (b, 0, i, 0)),
            pl.BlockSpec((1, tm, hd), lambda b, i: (b, i, GDN_QKV // hd)),
            _resident((1, GDN_DV)),
            pl.BlockSpec((1, tm, D), lambda b, i: (b, i, 0)),
            pl.BlockSpec((1, 6, D), lambda b, i: (b, 0, 0)),
            _resident((hd, D)),
        ],
        out_specs=pl.BlockSpec((1, tm, D), lambda b, i: (b, i, 0)),
        out_shape=jax.ShapeDtypeStruct((B, L, D), F32),
        scratch_shapes=[pltpu.VMEM((tm, hd), BF16)],
        compiler_params=_cparams(("parallel", "parallel"), 40),
        name="gdn_out",
    )(o2, proj, norm_w.reshape(1, GDN_DV), x, mod, w)


def gdn_layer(x, mod, nw, p):
    (w_in, conv_w, conv_b, w_ab, a_log, dt_bias, norm_w, w_out) = p
    proj, ab, abt = gdn_in_proj(x, mod, nw, w_in, w_ab)
    qkv = gdn_pre(proj, conv_w, conv_b)
    col, row = gdn_gates(ab, abt, a_log, dt_bias)
    o2 = gdn_chunk_scan(qkv, col, row)
    return gdn_out(o2, proj, norm_w, x, mod, w_out.astype(BF16))


def _swa_kernel(q_ref, kp_ref, kc_ref, kn_ref, vp_ref, vc_ref, vn_ref, sink_ref, o_ref, s_scr, p_scr, *, nb):
    W = SWA_BLOCK
    KW = 3 * W
    nk = SWA_HKV * SWA_DH
    NS = SWA_STEP_BLOCKS
    n = pl.program_id(1)
    qi = lax.broadcasted_iota(jnp.int32, (W, KW), 0)
    kj = lax.broadcasted_iota(jnp.int32, (W, KW), 1)
    dist = jnp.abs(kj - W - qi)
    head_of_lane = lax.broadcasted_iota(jnp.int32, (KW, nk), 1) // SWA_DH
    kb_all = jnp.concatenate([kp_ref[0], kc_ref[0], kn_ref[0]], axis=0).astype(BF16)
    vb_all = jnp.concatenate([vp_ref[0], vc_ref[0], vn_ref[0]], axis=0).astype(BF16)
    group_of_row = lax.broadcasted_iota(jnp.int32, (SWA_G * W, 1), 0) // W
    head_of_out = lax.broadcasted_iota(jnp.int32, (SWA_G * W, nk), 1) // SWA_DH
    slopes, sinks = [], []
    for h in range(SWA_HKV):
        slope = jnp.zeros((SWA_G * W, 1), F32)
        sink = jnp.zeros((SWA_G * W, 1), F32)
        for g in range(SWA_G):
            hq = h * SWA_G + g
            slope = jnp.where(group_of_row == g, 2.0 ** (-8.0 * (hq + 1) / SWA_HQ) * LOG2E, slope)
            sink = jnp.where(group_of_row == g, sink_ref[0:1, hq:hq + 1] * LOG2E, sink)
        slopes.append(slope)
        sinks.append(sink)
    distms, vbds, s_alls = [], [], []
    for j in range(NS):
        nj = n * NS + j
        valid = (dist <= SWA_WINDOW) & ((kj >= W) | (nj > 0)) & ((kj < 2 * W) | (nj < nb - 1))
        distm = jnp.where(valid, dist.astype(F32), -NEG)
        distms.append(jnp.concatenate([distm] * SWA_G, axis=0))
        kb, vb = kb_all[j * W:j * W + KW], vb_all[j * W:j * W + KW]
        kbd = jnp.concatenate([jnp.where(head_of_lane == h, kb, 0) for h in range(SWA_HKV)], axis=0)
        vbds.append(jnp.concatenate([jnp.where(head_of_lane == h, vb, 0) for h in range(SWA_HKV)], axis=0))
        q = jnp.concatenate([q_ref[0, j * W:(j + 1) * W, g * nk:(g + 1) * nk] for g in range(SWA_G)], axis=0)
        s_alls.append(_dot_nt((q * (SWA_DH ** -0.5 * LOG2E)).astype(BF16), kbd))
    ms = [[None] * SWA_HKV for _ in range(NS)]
    rs = [[None] * SWA_HKV for _ in range(NS)]
    for j in range(NS):
        for h in range(SWA_HKV):
            s = s_alls[j][:, h * KW:(h + 1) * KW] - slopes[h] * distms[j]
            s_scr[j, :, h * KW:(h + 1) * KW] = s
            ms[j][h] = jnp.maximum(jnp.max(s, axis=-1, keepdims=True), sinks[h])
    for j in range(NS):
        for h in range(SWA_HKV):
            e = jnp.exp2(s_scr[j, :, h * KW:(h + 1) * KW] - ms[j][h])
            p_scr[j, :, h * KW:(h + 1) * KW] = e.astype(BF16)
            rs[j][h] = 1.0 / (jnp.sum(e, axis=-1, keepdims=True) + jnp.exp2(sinks[h] - ms[j][h]))
    for j in range(NS):
        o = _bdot(p_scr[j], vbds[j])
        r_all = jnp.broadcast_to(rs[j][0], (SWA_G * W, nk))
        for h in range(1, SWA_HKV):
            r_all = jnp.where(head_of_out == h, rs[j][h], r_all)
        o = (o * r_all).astype(BF16)
        for g in range(SWA_G):
            o_ref[0, j * W:(j + 1) * W, g * nk:(g + 1) * nk] = o[g * W:(g + 1) * W]


def swa_attention(qkv, sink):
    B, L, _ = qkv.shape
    W = SWA_BLOCK
    nb = L // W
    nq = SWA_HQ * SWA_DH
    nk = SWA_HKV * SWA_DH
    kcol, vcol = nq // nk, nq // nk + 1

    NS = SWA_STEP_BLOCKS

    def band(col):
        return [pl.BlockSpec((1, W, nk), lambda b, n: (b, jnp.maximum(n * NS - 1, 0), col)),
                pl.BlockSpec((1, NS * W, nk), lambda b, n: (b, n, col)),
                pl.BlockSpec((1, W, nk), lambda b, n: (b, jnp.minimum(n * NS + NS, nb - 1), col))]

    return pl.pallas_call(
        functools.partial(_swa_kernel, nb=nb),
        grid=(B, nb // NS),
        in_specs=[pl.BlockSpec((1, NS * W, nq), lambda b, n: (b, n, 0))] + band(kcol) + band(vcol) + [
            _resident((1, SWA_HQ))],
        out_specs=pl.BlockSpec((1, NS * W, nq), lambda b, n: (b, n, 0)),
        out_shape=jax.ShapeDtypeStruct((B, L, nq), BF16),
        scratch_shapes=[pltpu.VMEM((NS, SWA_G * W, SWA_HKV * 3 * W), F32),
                        pltpu.VMEM((NS, SWA_G * W, SWA_HKV * 3 * W), BF16)],
        compiler_params=_cparams(("parallel", "parallel"), 40),
        name="swa_attention",
    )(qkv, qkv, qkv, qkv, qkv, qkv, qkv, sink.reshape(1, SWA_HQ))


def swa_layer(x, mod, nw, p):
    (w_qkv, sink, w_out) = p
    nq = SWA_HQ * SWA_DH
    wq = w_qkv[:, :nq].reshape(D, SWA_HKV, SWA_G, SWA_DH).swapaxes(1, 2).reshape(D, nq)
    wo = w_out.reshape(SWA_HKV, SWA_G, SWA_DH, D).swapaxes(0, 1).reshape(nq, D)
    qkv = norm_mod_matmul(x, mod, nw, jnp.concatenate([wq, w_qkv[:, nq:]], axis=1).astype(BF16))
    o = swa_attention(qkv, sink)
    return matmul_gated_residual(o, x, mod, wo.astype(BF16))


def _mla_proj_kernel(d_ref, cos_ref, sin_ref, qnw_ref, kvnw_ref, wq_ref, wqr_ref, wk_ref, wv_ref,
                     q_ref, k_ref, v_ref):
    dd = d_ref[0]
    cq = (_rms(dd[:, :MLA_QRANK]) * qnw_ref[...]).astype(BF16)
    ckv = (_rms(dd[:, MLA_QRANK:MLA_QRANK + MLA_KVRANK]) * kvnw_ref[...]).astype(BF16)
    cs, sn = cos_ref[...], sin_ref[...]
    base = MLA_QRANK + MLA_KVRANK
    k_rope = dd[:, base:base + MLA_HP] * cs + dd[:, base + MLA_HP:base + 2 * MLA_HP] * sn
    qa = _bdot(cq, wq_ref[...])
    qb = _bdot(cq, wqr_ref[...])
    kn = _bdot(ckv, wk_ref[...])
    scale = (MLA_NOPE + MLA_ROPE) ** -0.5 * LOG2E
    for h in range(MLA_H):
        hs = slice(h * MLA_HP, (h + 1) * MLA_HP)
        q_ref[0, :, hs] = ((qa[:, hs] * cs + qb[:, hs] * sn) * scale).astype(BF16)
        k_ref[0, :, hs] = (kn[:, hs] + k_rope).astype(BF16)
    v = _bdot(ckv, wv_ref[...])
    lane = lax.broadcasted_iota(jnp.int32, v.shape, 1)
    v_ref[0] = jnp.where(lane % MLA_HP == MLA_DV, 1.0, v).astype(BF16)


def mla_project(dlat, cos_t, sin_t, q_norm_w, kv_norm_w, wq, wqr, wk, wv, *, tm=256):
    B, L, N = dlat.shape
    hp = MLA_H * MLA_HP
    return pl.pallas_call(
        _mla_proj_kernel,
        grid=(B, L // tm),
        in_specs=[
            pl.BlockSpec((1, tm, N), lambda b, i: (b, i, 0)),
            pl.BlockSpec((tm, MLA_HP), lambda b, i: (i, 0)),
            pl.BlockSpec((tm, MLA_HP), lambda b, i: (i, 0)),
            _resident((1, MLA_QRANK)), _resident((1, MLA_KVRANK)),
            _resident(wq.shape), _resident(wqr.shape), _resident(wk.shape), _resident(wv.shape),
        ],
        out_specs=[
            pl.BlockSpec((1, tm, hp), lambda b, i: (b, i, 0)),
            pl.BlockSpec((1, tm, hp), lambda b, i: (b, i, 0)),
            pl.BlockSpec((1, tm, hp), lambda b, i: (b, i, 0)),
        ],
        out_shape=[jax.ShapeDtypeStruct((B, L, hp), BF16)] * 3,
        compiler_params=_cparams(("parallel", "parallel"), 40),
        name="mla_project",
    )(dlat, cos_t, sin_t, q_norm_w.reshape(1, -1), kv_norm_w.reshape(1, -1), wq, wqr, wk, wv)


def _mla_attn_kernel(q_ref, k_ref, v_ref, o_ref):
    hss = [slice(h * MLA_HP, (h + 1) * MLA_HP) for h in range(q_ref.shape[2] // MLA_HP)]
    ss = [_dot_nt(q_ref[0, :, hs], k_ref[0, :, hs]) for hs in hss]
    es = [jnp.exp2(s - jnp.max(s, axis=-1, keepdims=True)).astype(BF16) for s in ss]
    ovs = [_bdot(e, v_ref[0, :, hs]) for e, hs in zip(es, hss)]
    outs = [ov[:, :MLA_DV] / ov[:, MLA_DV:MLA_DV + 1] for ov in ovs]
    o_ref[0] = jnp.concatenate(outs, axis=1).astype(BF16)


def mla_attention(q, k, v, *, tq=256):
    B, L, _ = q.shape
    group = 4
    return pl.pallas_call(
        _mla_attn_kernel,
        grid=(B, MLA_H // group, L // tq),
        in_specs=[
            pl.BlockSpec((1, tq, group * MLA_HP), lambda b, h, i: (b, i, h)),
            pl.BlockSpec((1, L, group * MLA_HP), lambda b, h, i: (b, 0, h)),
            pl.BlockSpec((1, L, group * MLA_HP), lambda b, h, i: (b, 0, h)),
        ],
        out_specs=pl.BlockSpec((1, tq, group * MLA_DV), lambda b, h, i: (b, i, h)),
        out_shape=jax.ShapeDtypeStruct((B, L, MLA_H * MLA_DV), BF16),
        compiler_params=_cparams(("parallel", "parallel", "parallel"), 56),
        name="mla_attention",
    )(q, k, v)


def _rot_half_cols(w):
    half = MLA_ROPE // 2
    return jnp.concatenate([-w[..., half:], w[..., :half]], axis=-1)


def mla_layer(x, mod, nw, p):
    (w_down, q_norm_w, w_uq, kv_norm_w, w_ukv, w_out) = p
    L = x.shape[1]
    pad_r = MLA_HP - MLA_NOPE - MLA_ROPE
    base = MLA_QRANK + MLA_KVRANK
    w_rope = w_down[:, base:]
    zl = jnp.zeros((D, MLA_NOPE), F32)
    zr = jnp.zeros((D, pad_r), F32)
    w_dext = jnp.concatenate([w_down[:, :base], zl, w_rope, zr, zl, _rot_half_cols(w_rope), zr], axis=1)
    wq3 = w_uq.reshape(MLA_QRANK, MLA_H, MLA_NOPE + MLA_ROPE)
    zq = jnp.zeros((MLA_QRANK, MLA_H, pad_r), F32)
    wq = jnp.concatenate([wq3, zq], axis=-1).reshape(MLA_QRANK, -1).astype(BF16)
    wqr = jnp.concatenate([jnp.zeros((MLA_QRANK, MLA_H, MLA_NOPE), F32), _rot_half_cols(wq3[..., MLA_NOPE:]), zq],
                          axis=-1).reshape(MLA_QRANK, -1).astype(BF16)
    wkv3 = w_ukv.reshape(MLA_KVRANK, MLA_H, MLA_NOPE + MLA_DV)
    wk = jnp.concatenate([wkv3[..., :MLA_NOPE], jnp.zeros((MLA_KVRANK, MLA_H, MLA_HP - MLA_NOPE), F32)],
                         axis=-1).reshape(MLA_KVRANK, -1).astype(BF16)
    wv = jnp.concatenate([wkv3[..., MLA_NOPE:], jnp.zeros((MLA_KVRANK, MLA_H, MLA_HP - MLA_DV), F32)],
                         axis=-1).reshape(MLA_KVRANK, -1).astype(BF16)
    inv = ROPE_THETA ** (-jnp.arange(0, MLA_ROPE, 2, dtype=F32) / MLA_ROPE)
    ang = jnp.arange(L, dtype=F32)[:, None] * inv[None, :]
    cos, sin = jnp.cos(ang), jnp.sin(ang)
    cos_t = jnp.concatenate([jnp.ones((L, MLA_NOPE), F32), cos, cos, jnp.zeros((L, pad_r), F32)], axis=1)
    sin_t = jnp.concatenate([jnp.zeros((L, MLA_NOPE), F32), sin, sin, jnp.zeros((L, pad_r), F32)], axis=1)

    dlat = norm_mod_matmul(x, mod, nw, w_dext.astype(BF16))
    q, k, v = mla_project(dlat, cos_t, sin_t, q_norm_w, kv_norm_w, wq, wqr, wk, wv)
    o = mla_attention(q, k, v)
    return matmul_gated_residual(o, x, mod, w_out.astype(BF16))


def encoder_trunk(x, c, ada_w, ada_b, norm_w, hy, gdn, swa, mla, ffn_w_gu, ffn_w_down, final_norm_w):
    mods = ada_modulation(c, ada_w, ada_b)
    layers = (hyena_layer, gdn_layer, swa_layer, mla_layer)
    params = (hy, gdn, swa, mla)
    for i in range(DEPTH):
        kind, j = i % 4, i // 4
        x = layers[kind](x, mods[i], norm_w[i, 0], [p[j] for p in params[kind]])
        x = ffn_block(x, mods[i], norm_w[i, 1], ffn_w_gu[i].astype(BF16), ffn_w_down[i].astype(BF16),
                      final_norm_w if i == DEPTH - 1 else None)
    return x


def kernel(x_prompt, x_sample, c_prompt, c_sample, ada_w, ada_b, norm_w, hy_w_in, hy_conv_w, hy_conv_b, hy_filt_w1, hy_filt_b1, hy_filt_freq1, hy_filt_w2, hy_filt_b2, hy_filt_freq2, hy_filt_w3, hy_skip, hy_w_out, gdn_w_in, gdn_conv_w, gdn_conv_b, gdn_w_ab, gdn_a_log, gdn_dt_bias, gdn_norm_w, gdn_w_out, swa_w_qkv, swa_sink, swa_w_out, mla_w_down, mla_q_norm_w, mla_w_uq, mla_kv_norm_w, mla_w_ukv, mla_w_out, ffn_w_gu, ffn_w_down, final_norm_w):
    hy = (hy_w_in, hy_conv_w, hy_conv_b, hy_filt_w1, hy_filt_b1, hy_filt_freq1,
          hy_filt_w2, hy_filt_b2, hy_filt_freq2, hy_filt_w3, hy_skip, hy_w_out)
    gdn = (gdn_w_in, gdn_conv_w, gdn_conv_b, gdn_w_ab, gdn_a_log, gdn_dt_bias, gdn_norm_w, gdn_w_out)
    swa = (swa_w_qkv, swa_sink, swa_w_out)
    mla = (mla_w_down, mla_q_norm_w, mla_w_uq, mla_kv_norm_w, mla_w_ukv, mla_w_out)
    args = (ada_w, ada_b, norm_w, hy, gdn, swa, mla, ffn_w_gu, ffn_w_down, final_norm_w)
    return (encoder_trunk(x_prompt, c_prompt, *args), encoder_trunk(x_sample, c_sample, *args))
```

```python
import functools
import math

import jax
import jax.numpy as jnp
from jax import lax
from jax.experimental import pallas as pl
from jax.experimental.pallas import tpu as pltpu

F32 = jnp.float32
BF16 = jnp.bfloat16
HI = lax.Precision.HIGHEST

D = 1024
DEPTH = 4
EPS = 1e-6
D_FF = 2816

HY_BANDS = 16
HY_FILT = 64
HY_TARGET = 1e-2
HY_FAST = 0.3
HY_SLOW = 1.5

GDN_HK = 8
GDN_HV = 16
GDN_DK = 128
GDN_DV = 128
GDN_CHUNK = 64
GDN_QKV = 2 * GDN_HK * GDN_DK + GDN_HV * GDN_DV
GDN_IN = GDN_QKV + GDN_HV * GDN_DV

SWA_HQ = 16
SWA_HKV = 4
SWA_G = SWA_HQ // SWA_HKV
SWA_DH = 64
SWA_WINDOW = 128
SWA_BLOCK = 128
SWA_STEP_BLOCKS = 4

MLA_H = 16
MLA_NOPE = 64
MLA_ROPE = 32
MLA_DV = 64
MLA_QRANK = 256
MLA_KVRANK = 256
MLA_HP = 128
ROPE_THETA = 10000.0

LANE = 128
LOG2E = math.log2(math.e)
MIB = 2 ** 20
NEG = -1e30


def _cparams(sem, vmem_mb):
    return pltpu.CompilerParams(dimension_semantics=sem, vmem_limit_bytes=vmem_mb * MIB)


def _resident(shape):
    nd = len(shape)
    return pl.BlockSpec(shape, lambda *_: (0,) * nd, pipeline_mode=pl.Buffered(1))


def _bdot(a, b):
    return jnp.dot(a, b, preferred_element_type=F32)


def _dot_nt(a, b, precision=None):
    return lax.dot_general(a, b, (((1,), (1,)), ((), ())), preferred_element_type=F32, precision=precision)


def _dot_tn(a, b, precision=None):
    return lax.dot_general(a, b, (((0,), (0,)), ((), ())), preferred_element_type=F32, precision=precision)


def _split_bf16(x):
    hi = x.astype(BF16)
    return hi, (x - hi.astype(F32)).astype(BF16)


def _rms(x):
    return x * lax.rsqrt(jnp.mean(x * x, axis=-1, keepdims=True) + EPS)


def _norm_mod(x, nw, mod_ref, sh_row, sc_row):
    return _rms(x) * nw * (1.0 + mod_ref[0, sc_row:sc_row + 1, :]) + mod_ref[0, sh_row:sh_row + 1, :]


def _ada_kernel(c_ref, w_ref, b_ref, o_ref):
    c = c_ref[...]
    ca = c * jax.nn.sigmoid(c)
    o_ref[0] = jnp.dot(ca, w_ref[0], preferred_element_type=F32, precision=HI) + b_ref[0]


def ada_modulation(c, ada_w, ada_b):
    B = c.shape[0]
    tn = 1024
    out = pl.pallas_call(
        _ada_kernel,
        grid=(DEPTH, 6 * D // tn),
        in_specs=[
            pl.BlockSpec((B, D), lambda i, j: (0, 0)),
            pl.BlockSpec((1, D, tn), lambda i, j: (i, 0, j)),
            pl.BlockSpec((1, 1, tn), lambda i, j: (i, 0, j)),
        ],
        out_specs=pl.BlockSpec((1, B, tn), lambda i, j: (i, 0, j)),
        out_shape=jax.ShapeDtypeStruct((DEPTH, B, 6 * D), F32),
        compiler_params=_cparams(("parallel", "parallel"), 32),
        name="ada_modulation",
    )(c, ada_w, ada_b.reshape(DEPTH, 1, 6 * D))
    return out.reshape(DEPTH, B, 6, D)


def _nmm_kernel(x_ref, mod_ref, nw_ref, w_ref, o_ref, h_scr, *, sh_row, sc_row, tn):
    h_scr[...] = _norm_mod(x_ref[0], nw_ref[...], mod_ref, sh_row, sc_row).astype(BF16)
    n = w_ref.shape[1]
    for c0 in range(0, n, tn):
        c1 = min(c0 + tn, n)
        o_ref[0, :, c0:c1] = _bdot(h_scr[...], w_ref[:, c0:c1]).astype(o_ref.dtype)


def norm_mod_matmul(x, mod, nw, w, *, sh_row=0, sc_row=1, tm=512, tn=512, out_dtype=F32):
    B, L, _ = x.shape
    N = w.shape[1]
    return pl.pallas_call(
        functools.partial(_nmm_kernel, sh_row=sh_row, sc_row=sc_row, tn=tn),
        grid=(B, L // tm),
        in_specs=[
            pl.BlockSpec((1, tm, D), lambda b, i: (b, i, 0)),
            pl.BlockSpec((1, 6, D), lambda b, i: (b, 0, 0)),
            _resident((1, D)),
            _resident((D, N)),
        ],
        out_specs=pl.BlockSpec((1, tm, N), lambda b, i: (b, i, 0)),
        out_shape=jax.ShapeDtypeStruct((B, L, N), out_dtype),
        scratch_shapes=[pltpu.VMEM((tm, D), BF16)],
        compiler_params=_cparams(("parallel", "parallel"), 48),
        name="norm_mod_matmul",
    )(x, mod, nw.reshape(1, D), w)


def _mmres_kernel(a_ref, x_ref, mod_ref, w_ref, o_ref, *, g_row):
    y = _bdot(a_ref[0].astype(BF16), w_ref[...])
    o_ref[0] = x_ref[0] + mod_ref[0, g_row:g_row + 1, :] * y


def matmul_gated_residual(a, x, mod, w, *, g_row=2, tm=512):
    B, L, K = a.shape
    return pl.pallas_call(
        functools.partial(_mmres_kernel, g_row=g_row),
        grid=(B, L // tm),
        in_specs=[
            pl.BlockSpec((1, tm, K), lambda b, i: (b, i, 0)),
            pl.BlockSpec((1, tm, D), lambda b, i: (b, i, 0)),
            pl.BlockSpec((1, 6, D), lambda b, i: (b, 0, 0)),
            _resident((K, D)),
        ],
        out_specs=pl.BlockSpec((1, tm, D), lambda b, i: (b, i, 0)),
        out_shape=jax.ShapeDtypeStruct((B, L, D), F32),
        compiler_params=_cparams(("parallel", "parallel"), 40),
        name="matmul_gated_residual",
    )(a, x, mod, w)


def _ffn_kernel(x_ref, mod_ref, nw_ref, wgu_ref, wd_ref, *rest, tf, final):
    fnw_ref = rest[0] if final else None
    o_ref, h_scr, act_scr = rest[-3:]
    x = x_ref[0]
    h_scr[...] = _norm_mod(x, nw_ref[...], mod_ref, 3, 4).astype(BF16)
    for c0 in range(0, D_FF, tf):
        gate = _bdot(h_scr[...], wgu_ref[:, c0:c0 + tf])
        up = _bdot(h_scr[...], wgu_ref[:, D_FF + c0:D_FF + c0 + tf])
        act_scr[:, c0:c0 + tf] = (gate * jax.nn.sigmoid(gate) * up).astype(BF16)
    y = x + mod_ref[0, 5:6, :] * _bdot(act_scr[...], wd_ref[...])
    o_ref[0] = _rms(y) * fnw_ref[...] if final else y


def ffn_block(x, mod, nw, w_gu, w_down, final_nw=None, *, tm=512, tf=256):
    B, L, _ = x.shape
    final = final_nw is not None
    return pl.pallas_call(
        functools.partial(_ffn_kernel, tf=tf, final=final),
        grid=(B, L // tm),
        in_specs=[
            pl.BlockSpec((1, tm, D), lambda b, i: (b, i, 0)),
            pl.BlockSpec((1, 6, D), lambda b, i: (b, 0, 0)),
            _resident((1, D)),
            _resident((D, 2 * D_FF)),
            _resident((D_FF, D)),
        ] + ([_resident((1, D))] if final else []),
        out_specs=pl.BlockSpec((1, tm, D), lambda b, i: (b, i, 0)),
        out_shape=jax.ShapeDtypeStruct((B, L, D), F32),
        scratch_shapes=[pltpu.VMEM((tm, D), BF16), pltpu.VMEM((tm, D_FF), BF16)],
        compiler_params=_cparams(("parallel", "parallel"), 48),
        name="ffn_block",
    )(x, mod, nw.reshape(1, D), w_gu, w_down, *([final_nw.reshape(1, D)] if final else []))


def _hy_filter_kernel(freq_ref, w1t_ref, w1c_ref, w1s_ref, b1_ref, f1_ref, w2_ref, b2_ref, f2_ref, w3_ref,
                      rate_ref, hs_ref, hd_ref, *, L, tl):
    pos = (pl.program_id(0) * tl + lax.broadcasted_iota(jnp.int32, (tl, 1), 0)).astype(F32)
    t = pos / max(L - 1, 1)
    ang = freq_ref[...] * (2.0 * math.pi / L) * pos
    z = (t * w1t_ref[...] + jnp.dot(jnp.cos(ang), w1c_ref[...], preferred_element_type=F32, precision=HI)
         - jnp.dot(jnp.sin(ang), w1s_ref[...], preferred_element_type=F32, precision=HI) + b1_ref[...])
    z = jnp.sin(f1_ref[...] * z)
    z = jnp.sin(f2_ref[...] * (jnp.dot(z, w2_ref[...], preferred_element_type=F32, precision=HI) + b2_ref[...]))
    z = jnp.dot(z, w3_ref[...], preferred_element_type=F32, precision=HI)
    window = jnp.exp(-t * rate_ref[...])
    hf = z[:, :D] * window
    hb = jnp.where(pos == 0.0, 0.0, z[:, D:] * window)
    hs_ref[...] = hf + hb
    hd_ref[...] = hb - hf


def hyena_filter_taps(L, fw1, fb1, ff1, fw2, fb2, ff2, fw3):
    tl = 256
    freqs = jnp.linspace(1e-4, HY_BANDS - 1, HY_BANDS, dtype=F32).reshape(1, HY_BANDS)
    rates = jnp.abs(jnp.linspace(math.log(HY_TARGET) / HY_SLOW, math.log(HY_TARGET) / HY_FAST, D, dtype=F32))
    small = [freqs, fw1[0:1], fw1[1:1 + HY_BANDS], fw1[1 + HY_BANDS:], fb1.reshape(1, -1), ff1.reshape(1, -1),
             fw2, fb2.reshape(1, -1), ff2.reshape(1, -1), fw3, rates.reshape(1, D)]
    return pl.pallas_call(
        functools.partial(_hy_filter_kernel, L=L, tl=tl),
        grid=(L // tl,),
        in_specs=[_resident(a.shape) for a in small],
        out_specs=[pl.BlockSpec((tl, D), lambda i: (i, 0))] * 2,
        out_shape=[jax.ShapeDtypeStruct((L, D), F32)] * 2,
        compiler_params=_cparams(("parallel",), 32),
        name="hyena_filter_taps",
    )(*small)


def _hy_filter_dft_kernel(hs_ref, hd_ref, cf_ref, sf_ref, kre_ref, kim_ref, knyq_ref, *, L, fk):
    k = pl.program_id(1)
    hs = hs_ref[...]
    inv_n = 1.0 / (2 * L)
    row = k * fk + lax.broadcasted_iota(jnp.int32, (fk, 1), 0)
    wk = jnp.where(row == 0, inv_n, 2.0 * inv_n)
    kre_ref[...] = wk * _bdot(cf_ref[...], hs.astype(BF16))
    kim_ref[...] = wk * _bdot(sf_ref[...], hd_ref[...].astype(BF16))

    @pl.when(k == 0)
    def _():
        t = lax.broadcasted_iota(jnp.int32, hs.shape, 0)
        sgn = (1 - 2 * (t & 1)).astype(F32)
        knyq_ref[...] = inv_n * jnp.sum(hs * sgn, axis=0, keepdims=True)


def hyena_filter_dft(hs, hd, cmat, smat, *, ct=256, fk=256):
    L = hs.shape[0]
    return pl.pallas_call(
        functools.partial(_hy_filter_dft_kernel, L=L, fk=fk),
        grid=(D // ct, L // fk),
        in_specs=[
            pl.BlockSpec((L, ct), lambda j, k: (0, j)),
            pl.BlockSpec((L, ct), lambda j, k: (0, j)),
            pl.BlockSpec((fk, L), lambda j, k: (k, 0)),
            pl.BlockSpec((fk, L), lambda j, k: (k, 0)),
        ],
        out_specs=[
            pl.BlockSpec((fk, ct), lambda j, k: (k, j)),
            pl.BlockSpec((fk, ct), lambda j, k: (k, j)),
            pl.BlockSpec((1, ct), lambda j, k: (0, j)),
        ],
        out_shape=[jax.ShapeDtypeStruct((L, D), F32), jax.ShapeDtypeStruct((L, D), F32),
                   jax.ShapeDtypeStruct((1, D), F32)],
        compiler_params=_cparams(("parallel", "arbitrary"), 48),
        name="hyena_filter_dft",
    )(hs, hd, cmat, smat)


def _hy_pre_kernel(u0_ref, u1_ref, u2_ref, w0_ref, w1_ref, w2_ref, b0_ref, b1_ref, b2_ref, x0_ref, vg_ref):
    L = u0_ref.shape[1]
    t = lax.broadcasted_iota(jnp.int32, (L, 1), 0)
    first, last = t == 0, t == L - 1

    def dwconv(u_ref, w_ref, b_ref):
        u = u_ref[0].astype(F32)
        prev = jnp.where(first, 0.0, pltpu.roll(u, 1, 0))
        nxt = jnp.where(last, 0.0, pltpu.roll(u, L - 1, 0))
        return b_ref[...] + w_ref[0:1, :] * prev + w_ref[1:2, :] * u + w_ref[2:3, :] * nxt

    x0_ref[0] = dwconv(u0_ref, w0_ref, b0_ref).astype(BF16)
    vg_ref[0] = (dwconv(u2_ref, w2_ref, b2_ref) * dwconv(u1_ref, w1_ref, b1_ref)).astype(BF16)


def hyena_pre(u, conv_w, conv_b, *, ct=128):
    B, L, _ = u.shape
    nj = D // ct
    conv_b = conv_b.reshape(1, 3 * D)
    ublk = [pl.BlockSpec((1, L, ct), lambda b, j, s=s: (b, 0, s * nj + j)) for s in range(3)]
    wblk = [pl.BlockSpec((3, ct), lambda b, j, s=s: (0, s * nj + j)) for s in range(3)]
    bblk = [pl.BlockSpec((1, ct), lambda b, j, s=s: (0, s * nj + j)) for s in range(3)]
    oblk = pl.BlockSpec((1, L, ct), lambda b, j: (b, 0, j))
    return pl.pallas_call(
        _hy_pre_kernel,
        grid=(B, nj),
        in_specs=ublk + wblk + bblk,
        out_specs=[oblk, oblk],
        out_shape=[jax.ShapeDtypeStruct((B, L, D), BF16)] * 2,
        compiler_params=_cparams(("parallel", "parallel"), 48),
        name="hyena_pre",
    )(u, u, u, conv_w, conv_w, conv_w, conv_b, conv_b, conv_b)


def _hy_conv_kernel(vb_ref, cf_ref, sf_ref, ci_ref, si_ref, kre_ref, kim_ref, knyq_ref, o_ref):
    k = pl.program_id(2)
    vb = vb_ref[0]

    @pl.when(k == 0)
    def _():
        t = lax.broadcasted_iota(jnp.int32, vb.shape, 0)
        sgn = (1 - 2 * (t & 1)).astype(F32)
        vnyq = jnp.sum(vb.astype(F32) * sgn, axis=0, keepdims=True)
        o_ref[0] = sgn * (vnyq * knyq_ref[...])

    vre = _bdot(cf_ref[...], vb)
    vim = _bdot(sf_ref[...], vb)
    kre, kim = kre_ref[...], kim_ref[...]
    yre = vre * kre + vim * kim
    nyim = vim * kre - vre * kim
    o_ref[0] += _bdot(ci_ref[...], yre.astype(BF16)) + _bdot(si_ref[...], nyim.astype(BF16))


def hyena_long_conv(vb, cmat, smat, kre, kim, knyq, *, ct=512, fk=256):
    B, L, _ = vb.shape
    return pl.pallas_call(
        _hy_conv_kernel,
        grid=(B, D // ct, L // fk),
        in_specs=[
            pl.BlockSpec((1, L, ct), lambda b, j, k: (b, 0, j)),
            pl.BlockSpec((fk, L), lambda b, j, k: (k, 0)),
            pl.BlockSpec((fk, L), lambda b, j, k: (k, 0)),
            pl.BlockSpec((L, fk), lambda b, j, k: (0, k)),
            pl.BlockSpec((L, fk), lambda b, j, k: (0, k)),
            pl.BlockSpec((fk, ct), lambda b, j, k: (k, j)),
            pl.BlockSpec((fk, ct), lambda b, j, k: (k, j)),
            pl.BlockSpec((1, ct), lambda b, j, k: (0, j)),
        ],
        out_specs=pl.BlockSpec((1, L, ct), lambda b, j, k: (b, 0, j)),
        out_shape=jax.ShapeDtypeStruct((B, L, D), F32),
        compiler_params=_cparams(("parallel", "parallel", "arbitrary"), 58),
        name="hyena_long_conv",
    )(vb, cmat, smat, cmat, smat, kre, kim, knyq)


def _hy_out_kernel(y_ref, vg_ref, x0_ref, skip_ref, x_ref, mod_ref, w_ref, o_ref):
    a = ((y_ref[0] + vg_ref[0] * skip_ref[...]) * x0_ref[0]).astype(BF16)
    o_ref[0] = x_ref[0] + mod_ref[0, 2:3, :] * _bdot(a, w_ref[...])


def hyena_out(y, vg, x0, skip, x, mod, w, *, tm=512):
    B, L, _ = x.shape
    tok = pl.BlockSpec((1, tm, D), lambda b, i: (b, i, 0))
    return pl.pallas_call(
        _hy_out_kernel,
        grid=(B, L // tm),
        in_specs=[tok, tok, tok, _resident((1, D)), tok, pl.BlockSpec((1, 6, D), lambda b, i: (b, 0, 0)),
                  _resident((D, D))],
        out_specs=tok,
        out_shape=jax.ShapeDtypeStruct((B, L, D), F32),
        compiler_params=_cparams(("parallel", "parallel"), 40),
        name="hyena_out",
    )(y, vg, x0, skip.reshape(1, D), x, mod, w)


def dft_tables(L):
    step = 64
    t = jnp.arange(L, dtype=jnp.int32)[None, :]

    def cs(rows):
        ang = ((rows[:, None] * t) % (2 * L)).astype(F32) * (math.pi / L)
        return jnp.cos(ang), jnp.sin(ang)

    ca, sa = cs(jnp.arange(0, L, step, dtype=jnp.int32))
    cb, sb = cs(jnp.arange(step, dtype=jnp.int32))
    cos_t = ca[:, None, :] * cb[None] - sa[:, None, :] * sb[None]
    sin_t = sa[:, None, :] * cb[None] + ca[:, None, :] * sb[None]
    return cos_t.reshape(L, L).astype(BF16), sin_t.reshape(L, L).astype(BF16)


def hyena_layer(x, mod, nw, p):
    (w_in, conv_w, conv_b, fw1, fb1, ff1, fw2, fb2, ff2, fw3, skip, w_out) = p
    L = x.shape[1]
    cmat, smat = dft_tables(L)
    hs, hd = hyena_filter_taps(L, fw1, fb1, ff1, fw2, fb2, ff2, fw3)
    kre, kim, knyq = hyena_filter_dft(hs, hd, cmat, smat)
    u = norm_mod_matmul(x, mod, nw, w_in.astype(BF16), out_dtype=BF16)
    x0, vg = hyena_pre(u, conv_w, conv_b)
    y = hyena_long_conv(vg, cmat, smat, kre, kim, knyq, ct=512, fk=512 if L <= 2048 else 256)
    return hyena_out(y, vg, x0, skip, x, mod, w_out.astype(BF16))


def _gdn_in_kernel(x_ref, mod_ref, nw_ref, w_ref, wab_hi_ref, wab_lo_ref, o_ref, ab_ref, abt_ref, h_scr, *, tn):
    h = _norm_mod(x_ref[0], nw_ref[...], mod_ref, 0, 1)
    h_hi, h_lo = _split_bf16(h)
    h_scr[...] = h_hi
    for c0 in range(0, GDN_IN, tn):
        o_ref[0, :, c0:c0 + tn] = _bdot(h_scr[...], w_ref[:, c0:c0 + tn]).astype(BF16)
    ab = _bdot(h_hi, wab_hi_ref[...]) + _bdot(h_lo, wab_hi_ref[...]) + _bdot(h_hi, wab_lo_ref[...])
    ab_ref[0] = ab
    abt_ref[0] = ab.T


def _pair_major(t):
    lead = t.shape[:-1]
    t = t.reshape(lead + (2, GDN_HK, GDN_HV // GDN_HK))
    return jnp.swapaxes(t, -1, -2).reshape(lead + (2 * GDN_HV,))


def gdn_in_proj(x, mod, nw, w_in, w_ab, *, tm=256, tn=512):
    B, L, _ = x.shape
    wab = jnp.concatenate([_pair_major(w_ab[:, :2 * GDN_HV]), _pair_major(w_ab[:, 2 * GDN_HV:])], axis=1)
    wab_hi, wab_lo = _split_bf16(jnp.pad(wab, ((0, 0), (0, LANE - wab.shape[1]))))
    return pl.pallas_call(
        functools.partial(_gdn_in_kernel, tn=tn),
        grid=(B, L // tm),
        in_specs=[
            pl.BlockSpec((1, tm, D), lambda b, i: (b, i, 0)),
            pl.BlockSpec((1, 6, D), lambda b, i: (b, 0, 0)),
            _resident((1, D)),
            _resident((D, GDN_IN)),
            _resident((D, LANE)),
            _resident((D, LANE)),
        ],
        out_specs=[
            pl.BlockSpec((1, tm, GDN_IN), lambda b, i: (b, i, 0)),
            pl.BlockSpec((1, tm, LANE), lambda b, i: (b, i, 0)),
            pl.BlockSpec((1, LANE, tm), lambda b, i: (b, 0, i)),
        ],
        out_shape=[jax.ShapeDtypeStruct((B, L, GDN_IN), BF16), jax.ShapeDtypeStruct((B, L, LANE), F32),
                   jax.ShapeDtypeStruct((B, LANE, L), F32)],
        scratch_shapes=[pltpu.VMEM((tm, D), BF16)],
        compiler_params=_cparams(("parallel", "parallel"), 48),
        name="gdn_in_proj",
    )(x, mod, nw.reshape(1, D), w_in.astype(BF16), wab_hi, wab_lo)


CONV_ROWS = 64
CONV_HALO = 16


def _dwconv_rows(u_ref, w_ref, b_ref, i):
    L = u_ref.shape[1]
    R, G = CONV_ROWS, CONV_HALO
    r0 = pl.multiple_of(i * R, R)
    u = u_ref[0, pl.ds(r0, R), :].astype(F32)
    lo = pl.multiple_of(jnp.maximum(r0 - G, 0), G)
    hi = pl.multiple_of(jnp.minimum(r0 + R, L - G), G)
    before = jnp.where(i == 0, 0.0, u_ref[0, pl.ds(lo, G), :].astype(F32)[G - 1:G])
    after = jnp.where(i == L // R - 1, 0.0, u_ref[0, pl.ds(hi, G), :].astype(F32)[0:1])
    t = lax.broadcasted_iota(jnp.int32, (R, 1), 0)
    prev = jnp.where(t == 0, before, pltpu.roll(u, 1, 0))
    nxt = jnp.where(t == R - 1, after, pltpu.roll(u, R - 1, 0))
    return b_ref[...] + w_ref[0:1, :] * prev + w_ref[1:2, :] * u + w_ref[2:3, :] * nxt


def _gdn_pre_kernel(u_ref, w_ref, b_ref, o_ref):
    L, ct = u_ref.shape[1], u_ref.shape[2]
    j = pl.program_id(1)
    nq_tiles = GDN_HK * GDN_DK // ct

    def run(normalise):
        scale = jnp.where(j < nq_tiles, GDN_DK ** -0.5, 1.0)

        def body(i, carry):
            y = _dwconv_rows(u_ref, w_ref, b_ref, i)
            y = y * jax.nn.sigmoid(y)
            rows = pl.ds(pl.multiple_of(i * CONV_ROWS, CONV_ROWS), CONV_ROWS)
            if normalise:
                for h in range(ct // GDN_DK):
                    yh = y[:, h * GDN_DK:(h + 1) * GDN_DK]
                    o_ref[0, rows, h * GDN_DK:(h + 1) * GDN_DK] = (yh * (
                        lax.rsqrt(jnp.sum(yh * yh, axis=-1, keepdims=True) + EPS) * scale)).astype(BF16)
            else:
                o_ref[0, rows, :] = y.astype(BF16)
            return carry

        lax.fori_loop(0, L // CONV_ROWS, body, 0, unroll=4)

    @pl.when(j >= 2 * nq_tiles)
    def _():
        run(False)

    @pl.when(j < 2 * nq_tiles)
    def _():
        run(True)


def gdn_pre(proj, conv_w, conv_b, *, ct=256):
    B, L, _ = proj.shape
    return pl.pallas_call(
        _gdn_pre_kernel,
        grid=(B, GDN_QKV // ct),
        in_specs=[
            pl.BlockSpec((1, L, ct), lambda b, j: (b, 0, j)),
            pl.BlockSpec((3, ct), lambda b, j: (0, j)),
            pl.BlockSpec((1, ct), lambda b, j: (0, j)),
        ],
        out_specs=pl.BlockSpec((1, L, ct), lambda b, j: (b, 0, j)),
        out_shape=jax.ShapeDtypeStruct((B, L, GDN_QKV), BF16),
        compiler_params=_cparams(("parallel", "parallel"), 40),
        name="gdn_pre",
    )(proj, conv_w, conv_b.reshape(1, GDN_QKV))


def _softplus(x):
    return jnp.maximum(x, 0.0) + jnp.log1p(jnp.exp(-jnp.abs(x)))


def _gdn_gates_kernel(ab_ref, abt_ref, alog_c_ref, dtb_c_ref, alog_r_ref, dtb_r_ref, col_ref, row_ref, *, tl):
    C = GDN_CHUNK
    H = GDN_HV
    i = lax.broadcasted_iota(jnp.int32, (C, C), 0)
    j = lax.broadcasted_iota(jnp.int32, (C, C), 1)
    lower = (i >= j).astype(F32)
    upper = (i <= j).astype(F32)
    P = GDN_HK
    ab = ab_ref[0]
    g_c = -jnp.exp(alog_c_ref[...]) * _softplus(ab[:, :2 * H] + dtb_c_ref[...])
    logbeta_c = -_softplus(-ab[:, 2 * H:4 * H])
    abt = abt_ref[0]
    g_r = -jnp.exp(alog_r_ref[...]) * _softplus(abt[:2 * H, :] + dtb_r_ref[...])
    beta_r = jax.nn.sigmoid(abt[2 * H:4 * H, :])
    zc = jnp.zeros((C, LANE - 4 * H), F32)

    def pack(x, swap):
        return jnp.concatenate([x[P:], x[:P]] if swap else [x[:P], x[P:]], axis=1)

    for c in range(tl // C):
        gch = g_c[c * C:(c + 1) * C, :]
        pre = jnp.dot(lower, gch, preferred_element_type=F32, precision=HI)
        suf = jnp.dot(upper, gch, preferred_element_type=F32, precision=HI)
        tot = jnp.sum(gch, axis=0, keepdims=True)
        grc = g_r[:, c * C:(c + 1) * C]
        pre_r = jnp.dot(grc, upper, preferred_element_type=F32, precision=HI)
        suf_r = jnp.dot(grc, lower, preferred_element_type=F32, precision=HI)
        tot_r = jnp.sum(grc, axis=1, keepdims=True)
        for d in range(2):
            hs = slice(d * H, (d + 1) * H)
            gc = (pre if d == 0 else suf)[:, hs]
            lb = logbeta_c[c * C:(c + 1) * C, hs]
            col_ref[0, d, c * C:(c + 1) * C, :] = jnp.concatenate(
                [gc, gc + lb, jnp.exp(gc), jnp.exp(tot[:, hs] - gc), zc], axis=1)
            gr = (pre_r if d == 0 else suf_r)[hs, :]
            br = beta_r[hs, c * C:(c + 1) * C]
            row_ref[0, d, c] = jnp.concatenate(
                [pack(gr, False), pack(br, True), pack(br * jnp.exp(gr), True),
                 jnp.broadcast_to(jnp.exp(tot_r[hs, :]), (H, LANE))], axis=0)


GDN_ROWS = 3 * GDN_HK + GDN_HV


def gdn_gates(ab, abt, a_log, dt_bias, *, tl=512):
    B, L, _ = ab.shape
    N = L // GDN_CHUNK
    H2 = 2 * GDN_HV
    return pl.pallas_call(
        functools.partial(_gdn_gates_kernel, tl=tl),
        grid=(B, L // tl),
        in_specs=[
            pl.BlockSpec((1, tl, LANE), lambda b, i: (b, i, 0)),
            pl.BlockSpec((1, LANE, tl), lambda b, i: (b, 0, i)),
            _resident((1, H2)), _resident((1, H2)), _resident((H2, 1)), _resident((H2, 1)),
        ],
        out_specs=[
            pl.BlockSpec((1, 2, tl, LANE), lambda b, i: (b, 0, i, 0)),
            pl.BlockSpec((1, 2, tl // GDN_CHUNK, GDN_ROWS, LANE), lambda b, i: (b, 0, i, 0, 0)),
        ],
        out_shape=[jax.ShapeDtypeStruct((B, 2, L, LANE), F32),
                   jax.ShapeDtypeStruct((B, 2, N, GDN_ROWS, LANE), F32)],
        compiler_params=_cparams(("parallel", "parallel"), 32),
        name="gdn_gates",
    )(ab, abt, _pair_major(a_log.reshape(1, H2)), _pair_major(dt_bias.reshape(1, H2)),
      _pair_major(a_log.reshape(H2)).reshape(H2, 1), _pair_major(dt_bias.reshape(H2)).reshape(H2, 1))


def _gdn_chunk_kernel(qf_ref, kf_ref, vf_ref, colf_ref, rowf_ref, qb_ref, kb_ref, vb_ref, colb_ref, rowb_ref,
                      of_ref, ob_ref, s_scr):
    C = GDN_CHUNK
    H = GDN_HV

    @pl.when(pl.program_id(1) == 0)
    def _():
        s_scr[...] = jnp.zeros_like(s_scr)

    P = GDN_HK
    W = 2 * GDN_DV
    items = [(d, p) for d in range(2) for p in range(P)]
    q_refs, k_refs, v_refs, o_refs = (qf_ref, qb_ref), (kf_ref, kb_ref), (vf_ref, vb_ref), (of_ref, ob_ref)
    cols = (colf_ref[0, 0], colb_ref[0, 0])
    rows = (rowf_ref[0, 0, 0], rowb_ref[0, 0, 0])
    i = lax.broadcasted_iota(jnp.int32, (C, LANE), 0)
    lane = lax.broadcasted_iota(jnp.int32, (C, LANE), 1)
    diff = i - (lane & (C - 1))
    incls = (diff >= 0, diff <= 0)
    stricts = (diff > 0, diff < 0)
    left = lane < C
    eye2 = (diff == 0).astype(F32)
    zb = jnp.zeros((C, LANE), BF16)
    zs = jnp.zeros((GDN_DK, GDN_DV), BF16)

    def col_pair(base, d, p):
        col = cols[d]
        return jnp.where(left, col[:, base + p:base + p + 1], col[:, base + P + p:base + P + p + 1])

    def col_wide(base, d, p):
        col = cols[d]
        return jnp.concatenate(
            [jnp.broadcast_to(col[:, base + e * P + p:base + e * P + p + 1], (C, GDN_DV)) for e in range(2)], axis=1)

    def block_diag(a, b, z):
        return jnp.concatenate([jnp.concatenate([a, z], axis=1), jnp.concatenate([z, b], axis=1)], axis=0)

    def anti_diag(a, b, z):
        return jnp.concatenate([jnp.concatenate([z, a], axis=1), jnp.concatenate([b, z], axis=1)], axis=0)

    ks = [k_refs[d][0, :, p * GDN_DK:(p + 1) * GDN_DK] for d, p in items]
    qs = [q_refs[d][0, :, p * GDN_DK:(p + 1) * GDN_DK] for d, p in items]
    grams = [_dot_nt(jnp.concatenate([k, q], axis=0), jnp.concatenate([k, k], axis=0)) for k, q in zip(ks, qs)]
    ms, intras = [], []
    for x, (d, p) in enumerate(items):
        gc_j = rows[d][p:p + 1, :]
        decay = jnp.exp(jnp.where(incls[d], col_pair(0, d, p) - gc_j, NEG))
        a_coef = jnp.exp(jnp.where(stricts[d], col_pair(H, d, p) - gc_j, NEG))
        ms.append(-(grams[x][:C] * a_coef))
        intras.append((grams[x][C:] * decay).astype(BF16))
    tops = [jnp.where(left, m, eye2) for m in ms]
    bots = [jnp.where(left, eye2, m) for m in ms]
    for _ in range(6):
        for x in range(len(items)):
            m_hi, m_lo = _split_bf16(ms[x])
            top_hi, top_lo = _split_bf16(tops[x])
            bot_hi, bot_lo = _split_bf16(bots[x])
            r2 = _bdot(jnp.concatenate([m_hi, m_lo], axis=0), block_diag(top_hi, bot_hi, zb))
            r = r2[:C] + r2[C:] + _bdot(m_hi, block_diag(top_lo, bot_lo, zb))
            r0, r1 = r[:, :LANE], r[:, LANE:]
            tops[x] = jnp.where(left, r0, tops[x] + r0)
            bots[x] = jnp.where(left, bots[x] + r1, r1)
            ms[x] = jnp.where(left, r0, r1)
    us, ws = [], []
    for x, (d, p) in enumerate(items):
        t = jnp.where(left, bots[x], tops[x])
        v0 = v_refs[d][0, :, 2 * p * GDN_DV:(2 * p + 1) * GDN_DV]
        v1 = v_refs[d][0, :, (2 * p + 1) * GDN_DV:(2 * p + 2) * GDN_DV]
        us.append(_bdot((t * rows[d][P + p:P + p + 1, :]).astype(BF16), anti_diag(v1, v0, zb)))
        ws.append(_bdot((t * rows[d][2 * P + p:2 * P + p + 1, :]).astype(BF16), anti_diag(ks[x], ks[x], zb)))
    for x, (d, p) in enumerate(items):
        s = s_scr[d, p]
        sb = s.astype(BF16)
        lhs = jnp.concatenate([ws[x].astype(BF16), jnp.concatenate([qs[x], qs[x]], axis=1)], axis=0)
        ws_qs = _bdot(lhs, block_diag(sb[:, :GDN_DV], sb[:, GDN_DV:], zs))
        v_new = us[x] - ws_qs[:C]
        vb = v_new.astype(BF16)
        o_refs[d][0, :, p * W:(p + 1) * W] = (col_wide(2 * H, d, p) * ws_qs[C:] + _bdot(
            intras[x], block_diag(vb[:, :GDN_DV], vb[:, GDN_DV:], zb))).astype(BF16)
        row = rows[d]
        g_end = jnp.concatenate([row[3 * P + p:3 * P + p + 1, :], row[4 * P + p:4 * P + p + 1, :]], axis=1)
        s_scr[d, p] = s * g_end + _dot_tn(ks[x], (v_new * col_wide(3 * H, d, p)).astype(BF16))


def gdn_chunk_scan(qkv, col, row):
    B, L, _ = qkv.shape
    C = GDN_CHUNK
    N = L // C
    nq = GDN_HK * GDN_DK
    hd = GDN_HV * GDN_DV

    def specs(d):
        def cidx(n):
            return n + d * (N - 1 - 2 * n)

        ins = [pl.BlockSpec((1, C, nq), lambda b, n: (b, cidx(n), 0)),
               pl.BlockSpec((1, C, nq), lambda b, n: (b, cidx(n), 1)),
               pl.BlockSpec((1, C, hd), lambda b, n: (b, cidx(n), 1)),
               pl.BlockSpec((1, 1, C, LANE), lambda b, n: (b, d, cidx(n), 0)),
               pl.BlockSpec((1, 1, 1, GDN_ROWS, LANE), lambda b, n: (b, d, cidx(n), 0, 0))]
        return ins, pl.BlockSpec((1, C, hd), lambda b, n: (b, cidx(n), 0))

    (in_f, out_f), (in_b, out_b) = specs(0), specs(1)
    return pl.pallas_call(
        _gdn_chunk_kernel,
        grid=(B, N),
        in_specs=in_f + in_b,
        out_specs=[out_f, out_b],
        out_shape=[jax.ShapeDtypeStruct((B, L, hd), BF16)] * 2,
        scratch_shapes=[pltpu.VMEM((2, GDN_HK, GDN_DK, 2 * GDN_DV), F32)],
        compiler_params=_cparams(("parallel", "arbitrary"), 32),
        name="gdn_chunk_scan",
    )(qkv, qkv, qkv, col, row, qkv, qkv, qkv, col, row)


def _gdn_out_kernel(of_ref, ob_ref, z_ref, nw_ref, x_ref, mod_ref, w_ref, out_ref, a_scr):
    o = of_ref[0].astype(F32) + ob_ref[0].astype(F32)
    z = z_ref[0].astype(F32)
    gate = z * jax.nn.sigmoid(z)
    for h in range(GDN_HV):
        hs = slice(h * GDN_DV, (h + 1) * GDN_DV)
        a_scr[:, hs] = (_rms(o[:, hs]) * nw_ref[...] * gate[:, hs]).astype(BF16)
    out_ref[0] = x_ref[0] + mod_ref[0, 2:3, :] * _bdot(a_scr[...], w_ref[...])


def gdn_out(o_f, o_b, proj, norm_w, x, mod, w, *, tm=256):
    B, L, _ = x.shape
    hd = GDN_HV * GDN_DV
    return pl.pallas_call(
        _gdn_out_kernel,
        grid=(B, L // tm),
        in_specs=[
            pl.BlockSpec((1, tm, hd), lambda b, i: (b, i, 0)),
            pl.BlockSpec((1, tm, hd), lambda b, i: (b, i, 0)),
            pl.BlockSpec((1, tm, hd), lambda b, i: (b, i, GDN_QKV // hd)),
            _resident((1, GDN_DV)),
            pl.BlockSpec((1, tm, D), lambda b, i: (b, i, 0)),
            pl.BlockSpec((1, 6, D), lambda b, i: (b, 0, 0)),
            _resident((hd, D)),
        ],
        out_specs=pl.BlockSpec((1, tm, D), lambda b, i: (b, i, 0)),
        out_shape=jax.ShapeDtypeStruct((B, L, D), F32),
        scratch_shapes=[pltpu.VMEM((tm, hd), BF16)],
        compiler_params=_cparams(("parallel", "parallel"), 40),
        name="gdn_out",
    )(o_f, o_b, proj, norm_w.reshape(1, GDN_DV), x, mod, w)


def gdn_layer(x, mod, nw, p):
    (w_in, conv_w, conv_b, w_ab, a_log, dt_bias, norm_w, w_out) = p
    proj, ab, abt = gdn_in_proj(x, mod, nw, w_in, w_ab)
    qkv = gdn_pre(proj, conv_w, conv_b)
    col, row = gdn_gates(ab, abt, a_log, dt_bias)
    o_f, o_b = gdn_chunk_scan(qkv, col, row)
    return gdn_out(o_f, o_b, proj, norm_w, x, mod, w_out.astype(BF16))


def _swa_kernel(q_ref, kp_ref, kc_ref, kn_ref, vp_ref, vc_ref, vn_ref, sink_ref, o_ref, s_scr, p_scr, *, nb):
    W = SWA_BLOCK
    KW = 3 * W
    nk = SWA_HKV * SWA_DH
    NS = SWA_STEP_BLOCKS
    n = pl.program_id(1)
    qi = lax.broadcasted_iota(jnp.int32, (W, KW), 0)
    kj = lax.broadcasted_iota(jnp.int32, (W, KW), 1)
    dist = jnp.abs(kj - W - qi)
    head_of_lane = lax.broadcasted_iota(jnp.int32, (KW, nk), 1) // SWA_DH
    kb_all = jnp.concatenate([kp_ref[0], kc_ref[0], kn_ref[0]], axis=0).astype(BF16)
    vb_all = jnp.concatenate([vp_ref[0], vc_ref[0], vn_ref[0]], axis=0).astype(BF16)
    group_of_row = lax.broadcasted_iota(jnp.int32, (SWA_G * W, 1), 0) // W
    head_of_out = lax.broadcasted_iota(jnp.int32, (SWA_G * W, nk), 1) // SWA_DH
    slopes, sinks = [], []
    for h in range(SWA_HKV):
        slope = jnp.zeros((SWA_G * W, 1), F32)
        sink = jnp.zeros((SWA_G * W, 1), F32)
        for g in range(SWA_G):
            hq = h * SWA_G + g
            slope = jnp.where(group_of_row == g, 2.0 ** (-8.0 * (hq + 1) / SWA_HQ) * LOG2E, slope)
            sink = jnp.where(group_of_row == g, sink_ref[0:1, hq:hq + 1] * LOG2E, sink)
        slopes.append(slope)
        sinks.append(sink)
    distms, vbds, s_alls = [], [], []
    for j in range(NS):
        nj = n * NS + j
        valid = (dist <= SWA_WINDOW) & ((kj >= W) | (nj > 0)) & ((kj < 2 * W) | (nj < nb - 1))
        distm = jnp.where(valid, dist.astype(F32), -NEG)
        distms.append(jnp.concatenate([distm] * SWA_G, axis=0))
        kb, vb = kb_all[j * W:j * W + KW], vb_all[j * W:j * W + KW]
        kbd = jnp.concatenate([jnp.where(head_of_lane == h, kb, 0) for h in range(SWA_HKV)], axis=0)
        vbds.append(jnp.concatenate([jnp.where(head_of_lane == h, vb, 0) for h in range(SWA_HKV)], axis=0))
        q = jnp.concatenate([q_ref[0, j * W:(j + 1) * W, g * nk:(g + 1) * nk] for g in range(SWA_G)], axis=0)
        s_alls.append(_dot_nt((q * (SWA_DH ** -0.5 * LOG2E)).astype(BF16), kbd))
    ms = [[None] * SWA_HKV for _ in range(NS)]
    rs = [[None] * SWA_HKV for _ in range(NS)]
    for j in range(NS):
        for h in range(SWA_HKV):
            s = s_alls[j][:, h * KW:(h + 1) * KW] - slopes[h] * distms[j]
            s_scr[j, :, h * KW:(h + 1) * KW] = s
            ms[j][h] = jnp.maximum(jnp.max(s, axis=-1, keepdims=True), sinks[h])
    for j in range(NS):
        for h in range(SWA_HKV):
            e = jnp.exp2(s_scr[j, :, h * KW:(h + 1) * KW] - ms[j][h])
            p_scr[j, :, h * KW:(h + 1) * KW] = e.astype(BF16)
            rs[j][h] = 1.0 / (jnp.sum(e, axis=-1, keepdims=True) + jnp.exp2(sinks[h] - ms[j][h]))
    for j in range(NS):
        o = _bdot(p_scr[j], vbds[j])
        r_all = jnp.broadcast_to(rs[j][0], (SWA_G * W, nk))
        for h in range(1, SWA_HKV):
            r_all = jnp.where(head_of_out == h, rs[j][h], r_all)
        o = (o * r_all).astype(BF16)
        for g in range(SWA_G):
            o_ref[0, j * W:(j + 1) * W, g * nk:(g + 1) * nk] = o[g * W:(g + 1) * W]


def swa_attention(qkv, sink):
    B, L, _ = qkv.shape
    W = SWA_BLOCK
    nb = L // W
    nq = SWA_HQ * SWA_DH
    nk = SWA_HKV * SWA_DH
    kcol, vcol = nq // nk, nq // nk + 1

    NS = SWA_STEP_BLOCKS

    def band(col):
        return [pl.BlockSpec((1, W, nk), lambda b, n: (b, jnp.maximum(n * NS - 1, 0), col)),
                pl.BlockSpec((1, NS * W, nk), lambda b, n: (b, n, col)),
                pl.BlockSpec((1, W, nk), lambda b, n: (b, jnp.minimum(n * NS + NS, nb - 1), col))]

    return pl.pallas_call(
        functools.partial(_swa_kernel, nb=nb),
        grid=(B, nb // NS),
        in_specs=[pl.BlockSpec((1, NS * W, nq), lambda b, n: (b, n, 0))] + band(kcol) + band(vcol) + [
            _resident((1, SWA_HQ))],
        out_specs=pl.BlockSpec((1, NS * W, nq), lambda b, n: (b, n, 0)),
        out_shape=jax.ShapeDtypeStruct((B, L, nq), BF16),
        scratch_shapes=[pltpu.VMEM((NS, SWA_G * W, SWA_HKV * 3 * W), F32),
                        pltpu.VMEM((NS, SWA_G * W, SWA_HKV * 3 * W), BF16)],
        compiler_params=_cparams(("parallel", "parallel"), 40),
        name="swa_attention",
    )(qkv, qkv, qkv, qkv, qkv, qkv, qkv, sink.reshape(1, SWA_HQ))


def swa_layer(x, mod, nw, p):
    (w_qkv, sink, w_out) = p
    nq = SWA_HQ * SWA_DH
    wq = w_qkv[:, :nq].reshape(D, SWA_HKV, SWA_G, SWA_DH).swapaxes(1, 2).reshape(D, nq)
    wo = w_out.reshape(SWA_HKV, SWA_G, SWA_DH, D).swapaxes(0, 1).reshape(nq, D)
    qkv = norm_mod_matmul(x, mod, nw, jnp.concatenate([wq, w_qkv[:, nq:]], axis=1).astype(BF16))
    o = swa_attention(qkv, sink)
    return matmul_gated_residual(o, x, mod, wo.astype(BF16))


def _mla_proj_kernel(d_ref, cos_ref, sin_ref, qnw_ref, kvnw_ref, wq_ref, wqr_ref, wk_ref, wv_ref,
                     q_ref, k_ref, v_ref):
    dd = d_ref[0]
    cq = (_rms(dd[:, :MLA_QRANK]) * qnw_ref[...]).astype(BF16)
    ckv = (_rms(dd[:, MLA_QRANK:MLA_QRANK + MLA_KVRANK]) * kvnw_ref[...]).astype(BF16)
    cs, sn = cos_ref[...], sin_ref[...]
    base = MLA_QRANK + MLA_KVRANK
    k_rope = dd[:, base:base + MLA_HP] * cs + dd[:, base + MLA_HP:base + 2 * MLA_HP] * sn
    qa = _bdot(cq, wq_ref[...])
    qb = _bdot(cq, wqr_ref[...])
    kn = _bdot(ckv, wk_ref[...])
    scale = (MLA_NOPE + MLA_ROPE) ** -0.5 * LOG2E
    for h in range(MLA_H):
        hs = slice(h * MLA_HP, (h + 1) * MLA_HP)
        q_ref[0, :, hs] = ((qa[:, hs] * cs + qb[:, hs] * sn) * scale).astype(BF16)
        k_ref[0, :, hs] = (kn[:, hs] + k_rope).astype(BF16)
    v = _bdot(ckv, wv_ref[...])
    lane = lax.broadcasted_iota(jnp.int32, v.shape, 1)
    v_ref[0] = jnp.where(lane % MLA_HP == MLA_DV, 1.0, v).astype(BF16)


def mla_project(dlat, cos_t, sin_t, q_norm_w, kv_norm_w, wq, wqr, wk, wv, *, tm=256):
    B, L, N = dlat.shape
    hp = MLA_H * MLA_HP
    return pl.pallas_call(
        _mla_proj_kernel,
        grid=(B, L // tm),
        in_specs=[
            pl.BlockSpec((1, tm, N), lambda b, i: (b, i, 0)),
            pl.BlockSpec((tm, MLA_HP), lambda b, i: (i, 0)),
            pl.BlockSpec((tm, MLA_HP), lambda b, i: (i, 0)),
            _resident((1, MLA_QRANK)), _resident((1, MLA_KVRANK)),
            _resident(wq.shape), _resident(wqr.shape), _resident(wk.shape), _resident(wv.shape),
        ],
        out_specs=[
            pl.BlockSpec((1, tm, hp), lambda b, i: (b, i, 0)),
            pl.BlockSpec((1, tm, hp), lambda b, i: (b, i, 0)),
            pl.BlockSpec((1, tm, hp), lambda b, i: (b, i, 0)),
        ],
        out_shape=[jax.ShapeDtypeStruct((B, L, hp), BF16)] * 3,
        compiler_params=_cparams(("parallel", "parallel"), 40),
        name="mla_project",
    )(dlat, cos_t, sin_t, q_norm_w.reshape(1, -1), kv_norm_w.reshape(1, -1), wq, wqr, wk, wv)


def _mla_attn_kernel(q_ref, k_ref, v_ref, o_ref):
    hss = [slice(h * MLA_HP, (h + 1) * MLA_HP) for h in range(q_ref.shape[2] // MLA_HP)]
    ss = [_dot_nt(q_ref[0, :, hs], k_ref[0, :, hs]) for hs in hss]
    es = [jnp.exp2(s - jnp.max(s, axis=-1, keepdims=True)).astype(BF16) for s in ss]
    ovs = [_bdot(e, v_ref[0, :, hs]) for e, hs in zip(es, hss)]
    outs = [ov[:, :MLA_DV] / ov[:, MLA_DV:MLA_DV + 1] for ov in ovs]
    o_ref[0] = jnp.concatenate(outs, axis=1).astype(BF16)


def mla_attention(q, k, v, *, tq=256):
    B, L, _ = q.shape
    group = 4
    return pl.pallas_call(
        _mla_attn_kernel,
        grid=(B, MLA_H // group, L // tq),
        in_specs=[
            pl.BlockSpec((1, tq, group * MLA_HP), lambda b, h, i: (b, i, h)),
            pl.BlockSpec((1, L, group * MLA_HP), lambda b, h, i: (b, 0, h)),
            pl.BlockSpec((1, L, group * MLA_HP), lambda b, h, i: (b, 0, h)),
        ],
        out_specs=pl.BlockSpec((1, tq, group * MLA_DV), lambda b, h, i: (b, i, h)),
        out_shape=jax.ShapeDtypeStruct((B, L, MLA_H * MLA_DV), BF16),
        compiler_params=_cparams(("parallel", "parallel", "parallel"), 56),
        name="mla_attention",
    )(q, k, v)


def _rot_half_cols(w):
    half = MLA_ROPE // 2
    return jnp.concatenate([-w[..., half:], w[..., :half]], axis=-1)


def mla_layer(x, mod, nw, p):
    (w_down, q_norm_w, w_uq, kv_norm_w, w_ukv, w_out) = p
    L = x.shape[1]
    pad_r = MLA_HP - MLA_NOPE - MLA_ROPE
    base = MLA_QRANK + MLA_KVRANK
    w_rope = w_down[:, base:]
    zl = jnp.zeros((D, MLA_NOPE), F32)
    zr = jnp.zeros((D, pad_r), F32)
    w_dext = jnp.concatenate([w_down[:, :base], zl, w_rope, zr, zl, _rot_half_cols(w_rope), zr], axis=1)
    wq3 = w_uq.reshape(MLA_QRANK, MLA_H, MLA_NOPE + MLA_ROPE)
    zq = jnp.zeros((MLA_QRANK, MLA_H, pad_r), F32)
    wq = jnp.concatenate([wq3, zq], axis=-1).reshape(MLA_QRANK, -1).astype(BF16)
    wqr = jnp.concatenate([jnp.zeros((MLA_QRANK, MLA_H, MLA_NOPE), F32), _rot_half_cols(wq3[..., MLA_NOPE:]), zq],
                          axis=-1).reshape(MLA_QRANK, -1).astype(BF16)
    wkv3 = w_ukv.reshape(MLA_KVRANK, MLA_H, MLA_NOPE + MLA_DV)
    wk = jnp.concatenate([wkv3[..., :MLA_NOPE], jnp.zeros((MLA_KVRANK, MLA_H, MLA_HP - MLA_NOPE), F32)],
                         axis=-1).reshape(MLA_KVRANK, -1).astype(BF16)
    wv = jnp.concatenate([wkv3[..., MLA_NOPE:], jnp.zeros((MLA_KVRANK, MLA_H, MLA_HP - MLA_DV), F32)],
                         axis=-1).reshape(MLA_KVRANK, -1).astype(BF16)
    inv = ROPE_THETA ** (-jnp.arange(0, MLA_ROPE, 2, dtype=F32) / MLA_ROPE)
    ang = jnp.arange(L, dtype=F32)[:, None] * inv[None, :]
    cos, sin = jnp.cos(ang), jnp.sin(ang)
    cos_t = jnp.concatenate([jnp.ones((L, MLA_NOPE), F32), cos, cos, jnp.zeros((L, pad_r), F32)], axis=1)
    sin_t = jnp.concatenate([jnp.zeros((L, MLA_NOPE), F32), sin, sin, jnp.zeros((L, pad_r), F32)], axis=1)

    dlat = norm_mod_matmul(x, mod, nw, w_dext.astype(BF16))
    q, k, v = mla_project(dlat, cos_t, sin_t, q_norm_w, kv_norm_w, wq, wqr, wk, wv)
    o = mla_attention(q, k, v)
    return matmul_gated_residual(o, x, mod, w_out.astype(BF16))


def encoder_trunk(x, c, ada_w, ada_b, norm_w, hy, gdn, swa, mla, ffn_w_gu, ffn_w_down, final_norm_w):
    mods = ada_modulation(c, ada_w, ada_b)
    layers = (hyena_layer, gdn_layer, swa_layer, mla_layer)
    params = (hy, gdn, swa, mla)
    for i in range(DEPTH):
        kind, j = i % 4, i // 4
        x = layers[kind](x, mods[i], norm_w[i, 0], [p[j] for p in params[kind]])
        x = ffn_block(x, mods[i], norm_w[i, 1], ffn_w_gu[i].astype(BF16), ffn_w_down[i].astype(BF16),
                      final_norm_w if i == DEPTH - 1 else None)
    return x


def kernel(x_prompt, x_sample, c_prompt, c_sample, ada_w, ada_b, norm_w, hy_w_in, hy_conv_w, hy_conv_b, hy_filt_w1, hy_filt_b1, hy_filt_freq1, hy_filt_w2, hy_filt_b2, hy_filt_freq2, hy_filt_w3, hy_skip, hy_w_out, gdn_w_in, gdn_conv_w, gdn_conv_b, gdn_w_ab, gdn_a_log, gdn_dt_bias, gdn_norm_w, gdn_w_out, swa_w_qkv, swa_sink, swa_w_out, mla_w_down, mla_q_norm_w, mla_w_uq, mla_kv_norm_w, mla_w_ukv, mla_w_out, ffn_w_gu, ffn_w_down, final_norm_w):
    hy = (hy_w_in, hy_conv_w, hy_conv_b, hy_filt_w1, hy_filt_b1, hy_filt_freq1,
          hy_filt_w2, hy_filt_b2, hy_filt_freq2, hy_filt_w3, hy_skip, hy_w_out)
    gdn = (gdn_w_in, gdn_conv_w, gdn_conv_b, gdn_w_ab, gdn_a_log, gdn_dt_bias, gdn_norm_w, gdn_w_out)
    swa = (swa_w_qkv, swa_sink, swa_w_out)
    mla = (mla_w_down, mla_q_norm_w, mla_w_uq, mla_kv_norm_w, mla_w_ukv, mla_w_out)
    args = (ada_w, ada_b, norm_w, hy, gdn, swa, mla, ffn_w_gu, ffn_w_down, final_norm_w)
    return (encoder_trunk(x_prompt, c_prompt, *args), encoder_trunk(x_sample, c_sample, *args))
```

```python
import functools
import math

import jax
import jax.numpy as jnp
from jax import lax
from jax.experimental import pallas as pl
from jax.experimental.pallas import tpu as pltpu

F32 = jnp.float32
BF16 = jnp.bfloat16
HI = lax.Precision.HIGHEST

D = 1024
DEPTH = 4
EPS = 1e-6
D_FF = 2816

HY_BANDS = 16
HY_FILT = 64
HY_TARGET = 1e-2
HY_FAST = 0.3
HY_SLOW = 1.5

GDN_HK = 8
GDN_HV = 16
GDN_DK = 128
GDN_DV = 128
GDN_CHUNK = 64
GDN_QKV = 2 * GDN_HK * GDN_DK + GDN_HV * GDN_DV
GDN_IN = GDN_QKV + GDN_HV * GDN_DV

SWA_HQ = 16
SWA_HKV = 4
SWA_G = SWA_HQ // SWA_HKV
SWA_DH = 64
SWA_WINDOW = 128
SWA_BLOCK = 128
SWA_STEP_BLOCKS = 4

MLA_H = 16
MLA_NOPE = 64
MLA_ROPE = 32
MLA_DV = 64
MLA_QRANK = 256
MLA_KVRANK = 256
MLA_HP = 128
ROPE_THETA = 10000.0

LANE = 128
LOG2E = math.log2(math.e)
MIB = 2 ** 20
NEG = -1e30


def _cparams(sem, vmem_mb):
    return pltpu.CompilerParams(dimension_semantics=sem, vmem_limit_bytes=vmem_mb * MIB)


def _resident(shape):
    nd = len(shape)
    return pl.BlockSpec(shape, lambda *_: (0,) * nd, pipeline_mode=pl.Buffered(1))


def _bdot(a, b):
    return jnp.dot(a, b, preferred_element_type=F32)


def _dot_nt(a, b, precision=None):
    return lax.dot_general(a, b, (((1,), (1,)), ((), ())), preferred_element_type=F32, precision=precision)


def _dot_tn(a, b, precision=None):
    return lax.dot_general(a, b, (((0,), (0,)), ((), ())), preferred_element_type=F32, precision=precision)


def _split_bf16(x):
    hi = x.astype(BF16)
    return hi, (x - hi.astype(F32)).astype(BF16)


def _rms(x):
    return x * lax.rsqrt(jnp.mean(x * x, axis=-1, keepdims=True) + EPS)


def _norm_mod(x, nw, mod_ref, sh_row, sc_row):
    return _rms(x) * nw * (1.0 + mod_ref[0, sc_row:sc_row + 1, :]) + mod_ref[0, sh_row:sh_row + 1, :]


def _ada_kernel(c_ref, w_ref, b_ref, o_ref):
    c = c_ref[...]
    ca = c * jax.nn.sigmoid(c)
    o_ref[0] = jnp.dot(ca, w_ref[0], preferred_element_type=F32, precision=HI) + b_ref[0]


def ada_modulation(c, ada_w, ada_b):
    B = c.shape[0]
    tn = 1024
    out = pl.pallas_call(
        _ada_kernel,
        grid=(DEPTH, 6 * D // tn),
        in_specs=[
            pl.BlockSpec((B, D), lambda i, j: (0, 0)),
            pl.BlockSpec((1, D, tn), lambda i, j: (i, 0, j)),
            pl.BlockSpec((1, 1, tn), lambda i, j: (i, 0, j)),
        ],
        out_specs=pl.BlockSpec((1, B, tn), lambda i, j: (i, 0, j)),
        out_shape=jax.ShapeDtypeStruct((DEPTH, B, 6 * D), F32),
        compiler_params=_cparams(("parallel", "parallel"), 32),
        name="ada_modulation",
    )(c, ada_w, ada_b.reshape(DEPTH, 1, 6 * D))
    return out.reshape(DEPTH, B, 6, D)


def _nmm_kernel(x_ref, mod_ref, nw_ref, w_ref, o_ref, h_scr, *, sh_row, sc_row, tn):
    h_scr[...] = _norm_mod(x_ref[0], nw_ref[...], mod_ref, sh_row, sc_row).astype(BF16)
    n = w_ref.shape[1]
    for c0 in range(0, n, tn):
        c1 = min(c0 + tn, n)
        o_ref[0, :, c0:c1] = _bdot(h_scr[...], w_ref[:, c0:c1]).astype(o_ref.dtype)


def norm_mod_matmul(x, mod, nw, w, *, sh_row=0, sc_row=1, tm=512, tn=512, out_dtype=F32):
    B, L, _ = x.shape
    N = w.shape[1]
    return pl.pallas_call(
        functools.partial(_nmm_kernel, sh_row=sh_row, sc_row=sc_row, tn=tn),
        grid=(B, L // tm),
        in_specs=[
            pl.BlockSpec((1, tm, D), lambda b, i: (b, i, 0)),
            pl.BlockSpec((1, 6, D), lambda b, i: (b, 0, 0)),
            _resident((1, D)),
            _resident((D, N)),
        ],
        out_specs=pl.BlockSpec((1, tm, N), lambda b, i: (b, i, 0)),
        out_shape=jax.ShapeDtypeStruct((B, L, N), out_dtype),
        scratch_shapes=[pltpu.VMEM((tm, D), BF16)],
        compiler_params=_cparams(("parallel", "parallel"), 48),
        name="norm_mod_matmul",
    )(x, mod, nw.reshape(1, D), w)


def _mmres_kernel(a_ref, x_ref, mod_ref, w_ref, o_ref, *, g_row):
    y = _bdot(a_ref[0].astype(BF16), w_ref[...])
    o_ref[0] = x_ref[0] + mod_ref[0, g_row:g_row + 1, :] * y


def matmul_gated_residual(a, x, mod, w, *, g_row=2, tm=512):
    B, L, K = a.shape
    return pl.pallas_call(
        functools.partial(_mmres_kernel, g_row=g_row),
        grid=(B, L // tm),
        in_specs=[
            pl.BlockSpec((1, tm, K), lambda b, i: (b, i, 0)),
            pl.BlockSpec((1, tm, D), lambda b, i: (b, i, 0)),
            pl.BlockSpec((1, 6, D), lambda b, i: (b, 0, 0)),
            _resident((K, D)),
        ],
        out_specs=pl.BlockSpec((1, tm, D), lambda b, i: (b, i, 0)),
        out_shape=jax.ShapeDtypeStruct((B, L, D), F32),
        compiler_params=_cparams(("parallel", "parallel"), 40),
        name="matmul_gated_residual",
    )(a, x, mod, w)


def _ffn_kernel(x_ref, mod_ref, nw_ref, wgu_ref, wd_ref, *rest, tf, final):
    fnw_ref = rest[0] if final else None
    o_ref, h_scr, act_scr = rest[-3:]
    x = x_ref[0]
    h_scr[...] = _norm_mod(x, nw_ref[...], mod_ref, 3, 4).astype(BF16)
    for c0 in range(0, D_FF, tf):
        gate = _bdot(h_scr[...], wgu_ref[:, c0:c0 + tf])
        up = _bdot(h_scr[...], wgu_ref[:, D_FF + c0:D_FF + c0 + tf])
        act_scr[:, c0:c0 + tf] = (gate * jax.nn.sigmoid(gate) * up).astype(BF16)
    y = x + mod_ref[0, 5:6, :] * _bdot(act_scr[...], wd_ref[...])
    o_ref[0] = _rms(y) * fnw_ref[...] if final else y


def ffn_block(x, mod, nw, w_gu, w_down, final_nw=None, *, tm=512, tf=256):
    B, L, _ = x.shape
    final = final_nw is not None
    return pl.pallas_call(
        functools.partial(_ffn_kernel, tf=tf, final=final),
        grid=(B, L // tm),
        in_specs=[
            pl.BlockSpec((1, tm, D), lambda b, i: (b, i, 0)),
            pl.BlockSpec((1, 6, D), lambda b, i: (b, 0, 0)),
            _resident((1, D)),
            _resident((D, 2 * D_FF)),
            _resident((D_FF, D)),
        ] + ([_resident((1, D))] if final else []),
        out_specs=pl.BlockSpec((1, tm, D), lambda b, i: (b, i, 0)),
        out_shape=jax.ShapeDtypeStruct((B, L, D), F32),
        scratch_shapes=[pltpu.VMEM((tm, D), BF16), pltpu.VMEM((tm, D_FF), BF16)],
        compiler_params=_cparams(("parallel", "parallel"), 48),
        name="ffn_block",
    )(x, mod, nw.reshape(1, D), w_gu, w_down, *([final_nw.reshape(1, D)] if final else []))


def _hy_filter_kernel(freq_ref, w1t_ref, w1c_ref, w1s_ref, b1_ref, f1_ref, w2_ref, b2_ref, f2_ref, w3_ref,
                      rate_ref, hs_ref, hd_ref, *, L, tl):
    pos = (pl.program_id(0) * tl + lax.broadcasted_iota(jnp.int32, (tl, 1), 0)).astype(F32)
    t = pos / max(L - 1, 1)
    ang = freq_ref[...] * (2.0 * math.pi / L) * pos
    z = (t * w1t_ref[...] + jnp.dot(jnp.cos(ang), w1c_ref[...], preferred_element_type=F32, precision=HI)
         - jnp.dot(jnp.sin(ang), w1s_ref[...], preferred_element_type=F32, precision=HI) + b1_ref[...])
    z = jnp.sin(f1_ref[...] * z)
    z = jnp.sin(f2_ref[...] * (jnp.dot(z, w2_ref[...], preferred_element_type=F32, precision=HI) + b2_ref[...]))
    z = jnp.dot(z, w3_ref[...], preferred_element_type=F32, precision=HI)
    window = jnp.exp(-t * rate_ref[...])
    hf = z[:, :D] * window
    hb = jnp.where(pos == 0.0, 0.0, z[:, D:] * window)
    hs_ref[...] = hf + hb
    hd_ref[...] = hb - hf


def hyena_filter_taps(L, fw1, fb1, ff1, fw2, fb2, ff2, fw3):
    tl = 256
    freqs = jnp.linspace(1e-4, HY_BANDS - 1, HY_BANDS, dtype=F32).reshape(1, HY_BANDS)
    rates = jnp.abs(jnp.linspace(math.log(HY_TARGET) / HY_SLOW, math.log(HY_TARGET) / HY_FAST, D, dtype=F32))
    small = [freqs, fw1[0:1], fw1[1:1 + HY_BANDS], fw1[1 + HY_BANDS:], fb1.reshape(1, -1), ff1.reshape(1, -1),
             fw2, fb2.reshape(1, -1), ff2.reshape(1, -1), fw3, rates.reshape(1, D)]
    return pl.pallas_call(
        functools.partial(_hy_filter_kernel, L=L, tl=tl),
        grid=(L // tl,),
        in_specs=[_resident(a.shape) for a in small],
        out_specs=[pl.BlockSpec((tl, D), lambda i: (i, 0))] * 2,
        out_shape=[jax.ShapeDtypeStruct((L, D), F32)] * 2,
        compiler_params=_cparams(("parallel",), 32),
        name="hyena_filter_taps",
    )(*small)


def _hy_filter_dft_kernel(hs_ref, hd_ref, cf_ref, sf_ref, kre_ref, kim_ref, knyq_ref, *, L, fk):
    k = pl.program_id(1)
    hs = hs_ref[...]
    inv_n = 1.0 / (2 * L)
    row = k * fk + lax.broadcasted_iota(jnp.int32, (fk, 1), 0)
    wk = jnp.where(row == 0, inv_n, 2.0 * inv_n)
    kre_ref[...] = wk * _bdot(cf_ref[...], hs.astype(BF16))
    kim_ref[...] = wk * _bdot(sf_ref[...], hd_ref[...].astype(BF16))

    @pl.when(k == 0)
    def _():
        t = lax.broadcasted_iota(jnp.int32, hs.shape, 0)
        sgn = (1 - 2 * (t & 1)).astype(F32)
        knyq_ref[...] = inv_n * jnp.sum(hs * sgn, axis=0, keepdims=True)


def hyena_filter_dft(hs, hd, cmat, smat, *, ct=256, fk=256):
    L = hs.shape[0]
    return pl.pallas_call(
        functools.partial(_hy_filter_dft_kernel, L=L, fk=fk),
        grid=(D // ct, L // fk),
        in_specs=[
            pl.BlockSpec((L, ct), lambda j, k: (0, j)),
            pl.BlockSpec((L, ct), lambda j, k: (0, j)),
            pl.BlockSpec((fk, L), lambda j, k: (k, 0)),
            pl.BlockSpec((fk, L), lambda j, k: (k, 0)),
        ],
        out_specs=[
            pl.BlockSpec((fk, ct), lambda j, k: (k, j)),
            pl.BlockSpec((fk, ct), lambda j, k: (k, j)),
            pl.BlockSpec((1, ct), lambda j, k: (0, j)),
        ],
        out_shape=[jax.ShapeDtypeStruct((L, D), F32), jax.ShapeDtypeStruct((L, D), F32),
                   jax.ShapeDtypeStruct((1, D), F32)],
        compiler_params=_cparams(("parallel", "arbitrary"), 48),
        name="hyena_filter_dft",
    )(hs, hd, cmat, smat)


def _hy_pre_kernel(u0_ref, u1_ref, u2_ref, w0_ref, w1_ref, w2_ref, b0_ref, b1_ref, b2_ref, x0_ref, vg_ref):
    L = u0_ref.shape[1]
    t = lax.broadcasted_iota(jnp.int32, (L, 1), 0)
    first, last = t == 0, t == L - 1

    def dwconv(u_ref, w_ref, b_ref):
        u = u_ref[0].astype(F32)
        prev = jnp.where(first, 0.0, pltpu.roll(u, 1, 0))
        nxt = jnp.where(last, 0.0, pltpu.roll(u, L - 1, 0))
        return b_ref[...] + w_ref[0:1, :] * prev + w_ref[1:2, :] * u + w_ref[2:3, :] * nxt

    x0_ref[0] = dwconv(u0_ref, w0_ref, b0_ref).astype(BF16)
    vg_ref[0] = (dwconv(u2_ref, w2_ref, b2_ref) * dwconv(u1_ref, w1_ref, b1_ref)).astype(BF16)


def hyena_pre(u, conv_w, conv_b, *, ct=128):
    B, L, _ = u.shape
    nj = D // ct
    conv_b = conv_b.reshape(1, 3 * D)
    ublk = [pl.BlockSpec((1, L, ct), lambda b, j, s=s: (b, 0, s * nj + j)) for s in range(3)]
    wblk = [pl.BlockSpec((3, ct), lambda b, j, s=s: (0, s * nj + j)) for s in range(3)]
    bblk = [pl.BlockSpec((1, ct), lambda b, j, s=s: (0, s * nj + j)) for s in range(3)]
    oblk = pl.BlockSpec((1, L, ct), lambda b, j: (b, 0, j))
    return pl.pallas_call(
        _hy_pre_kernel,
        grid=(B, nj),
        in_specs=ublk + wblk + bblk,
        out_specs=[oblk, oblk],
        out_shape=[jax.ShapeDtypeStruct((B, L, D), BF16)] * 2,
        compiler_params=_cparams(("parallel", "parallel"), 48),
        name="hyena_pre",
    )(u, u, u, conv_w, conv_w, conv_w, conv_b, conv_b, conv_b)


def _hy_conv_kernel(vb_ref, cf_ref, sf_ref, ci_ref, si_ref, kre_ref, kim_ref, knyq_ref, o_ref):
    k = pl.program_id(2)
    vb = vb_ref[0]

    @pl.when(k == 0)
    def _():
        t = lax.broadcasted_iota(jnp.int32, vb.shape, 0)
        sgn = (1 - 2 * (t & 1)).astype(F32)
        vnyq = jnp.sum(vb.astype(F32) * sgn, axis=0, keepdims=True)
        o_ref[0] = sgn * (vnyq * knyq_ref[...])

    vre = _bdot(cf_ref[...], vb)
    vim = _bdot(sf_ref[...], vb)
    kre, kim = kre_ref[...], kim_ref[...]
    yre = vre * kre + vim * kim
    nyim = vim * kre - vre * kim
    o_ref[0] += _bdot(ci_ref[...], yre.astype(BF16)) + _bdot(si_ref[...], nyim.astype(BF16))


def hyena_long_conv(vb, cmat, smat, kre, kim, knyq, *, ct=512, fk=256):
    B, L, _ = vb.shape
    return pl.pallas_call(
        _hy_conv_kernel,
        grid=(B, D // ct, L // fk),
        in_specs=[
            pl.BlockSpec((1, L, ct), lambda b, j, k: (b, 0, j)),
            pl.BlockSpec((fk, L), lambda b, j, k: (k, 0)),
            pl.BlockSpec((fk, L), lambda b, j, k: (k, 0)),
            pl.BlockSpec((L, fk), lambda b, j, k: (0, k)),
            pl.BlockSpec((L, fk), lambda b, j, k: (0, k)),
            pl.BlockSpec((fk, ct), lambda b, j, k: (k, j)),
            pl.BlockSpec((fk, ct), lambda b, j, k: (k, j)),
            pl.BlockSpec((1, ct), lambda b, j, k: (0, j)),
        ],
        out_specs=pl.BlockSpec((1, L, ct), lambda b, j, k: (b, 0, j)),
        out_shape=jax.ShapeDtypeStruct((B, L, D), F32),
        compiler_params=_cparams(("parallel", "parallel", "arbitrary"), 58),
        name="hyena_long_conv",
    )(vb, cmat, smat, cmat, smat, kre, kim, knyq)


def _hy_out_kernel(y_ref, vg_ref, x0_ref, skip_ref, x_ref, mod_ref, w_ref, o_ref):
    a = ((y_ref[0] + vg_ref[0] * skip_ref[...]) * x0_ref[0]).astype(BF16)
    o_ref[0] = x_ref[0] + mod_ref[0, 2:3, :] * _bdot(a, w_ref[...])


def hyena_out(y, vg, x0, skip, x, mod, w, *, tm=512):
    B, L, _ = x.shape
    tok = pl.BlockSpec((1, tm, D), lambda b, i: (b, i, 0))
    return pl.pallas_call(
        _hy_out_kernel,
        grid=(B, L // tm),
        in_specs=[tok, tok, tok, _resident((1, D)), tok, pl.BlockSpec((1, 6, D), lambda b, i: (b, 0, 0)),
                  _resident((D, D))],
        out_specs=tok,
        out_shape=jax.ShapeDtypeStruct((B, L, D), F32),
        compiler_params=_cparams(("parallel", "parallel"), 40),
        name="hyena_out",
    )(y, vg, x0, skip.reshape(1, D), x, mod, w)


def dft_tables(L):
    step = 64
    t = jnp.arange(L, dtype=jnp.int32)[None, :]

    def cs(rows):
        ang = ((rows[:, None] * t) % (2 * L)).astype(F32) * (math.pi / L)
        return jnp.cos(ang), jnp.sin(ang)

    ca, sa = cs(jnp.arange(0, L, step, dtype=jnp.int32))
    cb, sb = cs(jnp.arange(step, dtype=jnp.int32))
    cos_t = ca[:, None, :] * cb[None] - sa[:, None, :] * sb[None]
    sin_t = sa[:, None, :] * cb[None] + ca[:, None, :] * sb[None]
    return cos_t.reshape(L, L).astype(BF16), sin_t.reshape(L, L).astype(BF16)


def hyena_layer(x, mod, nw, p):
    (w_in, conv_w, conv_b, fw1, fb1, ff1, fw2, fb2, ff2, fw3, skip, w_out) = p
    L = x.shape[1]
    cmat, smat = dft_tables(L)
    hs, hd = hyena_filter_taps(L, fw1, fb1, ff1, fw2, fb2, ff2, fw3)
    kre, kim, knyq = hyena_filter_dft(hs, hd, cmat, smat)
    u = norm_mod_matmul(x, mod, nw, w_in.astype(BF16), out_dtype=BF16)
    x0, vg = hyena_pre(u, conv_w, conv_b)
    y = hyena_long_conv(vg, cmat, smat, kre, kim, knyq, ct=512, fk=512 if L <= 2048 else 256)
    return hyena_out(y, vg, x0, skip, x, mod, w_out.astype(BF16))


def _gdn_in_kernel(x_ref, mod_ref, nw_ref, w_ref, wab_hi_ref, wab_lo_ref, o_ref, ab_ref, abt_ref, h_scr, *, tn):
    h = _norm_mod(x_ref[0], nw_ref[...], mod_ref, 0, 1)
    h_hi, h_lo = _split_bf16(h)
    h_scr[...] = h_hi
    for c0 in range(0, GDN_IN, tn):
        o_ref[0, :, c0:c0 + tn] = _bdot(h_scr[...], w_ref[:, c0:c0 + tn]).astype(BF16)
    ab = _bdot(h_hi, wab_hi_ref[...]) + _bdot(h_lo, wab_hi_ref[...]) + _bdot(h_hi, wab_lo_ref[...])
    ab_ref[0] = ab
    abt_ref[0] = ab.T


def _pair_major(t):
    lead = t.shape[:-1]
    t = t.reshape(lead + (2, GDN_HK, GDN_HV // GDN_HK))
    return jnp.swapaxes(t, -1, -2).reshape(lead + (2 * GDN_HV,))


def gdn_in_proj(x, mod, nw, w_in, w_ab, *, tm=256, tn=512):
    B, L, _ = x.shape
    wab = jnp.concatenate([_pair_major(w_ab[:, :2 * GDN_HV]), _pair_major(w_ab[:, 2 * GDN_HV:])], axis=1)
    wab_hi, wab_lo = _split_bf16(jnp.pad(wab, ((0, 0), (0, LANE - wab.shape[1]))))
    return pl.pallas_call(
        functools.partial(_gdn_in_kernel, tn=tn),
        grid=(B, L // tm),
        in_specs=[
            pl.BlockSpec((1, tm, D), lambda b, i: (b, i, 0)),
            pl.BlockSpec((1, 6, D), lambda b, i: (b, 0, 0)),
            _resident((1, D)),
            _resident((D, GDN_IN)),
            _resident((D, LANE)),
            _resident((D, LANE)),
        ],
        out_specs=[
            pl.BlockSpec((1, tm, GDN_IN), lambda b, i: (b, i, 0)),
            pl.BlockSpec((1, tm, LANE), lambda b, i: (b, i, 0)),
            pl.BlockSpec((1, LANE, tm), lambda b, i: (b, 0, i)),
        ],
        out_shape=[jax.ShapeDtypeStruct((B, L, GDN_IN), BF16), jax.ShapeDtypeStruct((B, L, LANE), F32),
                   jax.ShapeDtypeStruct((B, LANE, L), F32)],
        scratch_shapes=[pltpu.VMEM((tm, D), BF16)],
        compiler_params=_cparams(("parallel", "parallel"), 48),
        name="gdn_in_proj",
    )(x, mod, nw.reshape(1, D), w_in.astype(BF16), wab_hi, wab_lo)


CONV_ROWS = 64
CONV_HALO = 16


def _dwconv_rows(u_ref, w_ref, b_ref, i):
    L = u_ref.shape[1]
    R, G = CONV_ROWS, CONV_HALO
    r0 = pl.multiple_of(i * R, R)
    u = u_ref[0, pl.ds(r0, R), :].astype(F32)
    lo = pl.multiple_of(jnp.maximum(r0 - G, 0), G)
    hi = pl.multiple_of(jnp.minimum(r0 + R, L - G), G)
    before = jnp.where(i == 0, 0.0, u_ref[0, pl.ds(lo, G), :].astype(F32)[G - 1:G])
    after = jnp.where(i == L // R - 1, 0.0, u_ref[0, pl.ds(hi, G), :].astype(F32)[0:1])
    t = lax.broadcasted_iota(jnp.int32, (R, 1), 0)
    prev = jnp.where(t == 0, before, pltpu.roll(u, 1, 0))
    nxt = jnp.where(t == R - 1, after, pltpu.roll(u, R - 1, 0))
    return b_ref[...] + w_ref[0:1, :] * prev + w_ref[1:2, :] * u + w_ref[2:3, :] * nxt


def _gdn_pre_kernel(u_ref, w_ref, b_ref, o_ref):
    L, ct = u_ref.shape[1], u_ref.shape[2]
    j = pl.program_id(1)
    nq_tiles = GDN_HK * GDN_DK // ct

    def run(normalise):
        scale = jnp.where(j < nq_tiles, GDN_DK ** -0.5, 1.0)

        def body(i, carry):
            y = _dwconv_rows(u_ref, w_ref, b_ref, i)
            y = y * jax.nn.sigmoid(y)
            rows = pl.ds(pl.multiple_of(i * CONV_ROWS, CONV_ROWS), CONV_ROWS)
            if normalise:
                for h in range(ct // GDN_DK):
                    yh = y[:, h * GDN_DK:(h + 1) * GDN_DK]
                    o_ref[0, rows, h * GDN_DK:(h + 1) * GDN_DK] = (yh * (
                        lax.rsqrt(jnp.sum(yh * yh, axis=-1, keepdims=True) + EPS) * scale)).astype(BF16)
            else:
                o_ref[0, rows, :] = y.astype(BF16)
            return carry

        lax.fori_loop(0, L // CONV_ROWS, body, 0, unroll=4)

    @pl.when(j >= 2 * nq_tiles)
    def _():
        run(False)

    @pl.when(j < 2 * nq_tiles)
    def _():
        run(True)


def gdn_pre(proj, conv_w, conv_b, *, ct=256):
    B, L, _ = proj.shape
    return pl.pallas_call(
        _gdn_pre_kernel,
        grid=(B, GDN_QKV // ct),
        in_specs=[
            pl.BlockSpec((1, L, ct), lambda b, j: (b, 0, j)),
            pl.BlockSpec((3, ct), lambda b, j: (0, j)),
            pl.BlockSpec((1, ct), lambda b, j: (0, j)),
        ],
        out_specs=pl.BlockSpec((1, L, ct), lambda b, j: (b, 0, j)),
        out_shape=jax.ShapeDtypeStruct((B, L, GDN_QKV), BF16),
        compiler_params=_cparams(("parallel", "parallel"), 40),
        name="gdn_pre",
    )(proj, conv_w, conv_b.reshape(1, GDN_QKV))


def _softplus(x):
    return jnp.maximum(x, 0.0) + jnp.log1p(jnp.exp(-jnp.abs(x)))


def _gdn_gates_kernel(ab_ref, abt_ref, alog_c_ref, dtb_c_ref, alog_r_ref, dtb_r_ref, col_ref, row_ref, *, tl):
    C = GDN_CHUNK
    H = GDN_HV
    i = lax.broadcasted_iota(jnp.int32, (C, C), 0)
    j = lax.broadcasted_iota(jnp.int32, (C, C), 1)
    lower = (i >= j).astype(F32)
    upper = (i <= j).astype(F32)
    P = GDN_HK
    ab = ab_ref[0]
    g_c = -jnp.exp(alog_c_ref[...]) * _softplus(ab[:, :2 * H] + dtb_c_ref[...])
    logbeta_c = -_softplus(-ab[:, 2 * H:4 * H])
    abt = abt_ref[0]
    g_r = -jnp.exp(alog_r_ref[...]) * _softplus(abt[:2 * H, :] + dtb_r_ref[...])
    beta_r = jax.nn.sigmoid(abt[2 * H:4 * H, :])
    zc = jnp.zeros((C, LANE - 4 * H), F32)

    def pack(x, swap):
        return jnp.concatenate([x[P:], x[:P]] if swap else [x[:P], x[P:]], axis=1)

    for c in range(tl // C):
        gch = g_c[c * C:(c + 1) * C, :]
        pre = jnp.dot(lower, gch, preferred_element_type=F32, precision=HI)
        suf = jnp.dot(upper, gch, preferred_element_type=F32, precision=HI)
        tot = jnp.sum(gch, axis=0, keepdims=True)
        grc = g_r[:, c * C:(c + 1) * C]
        pre_r = jnp.dot(grc, upper, preferred_element_type=F32, precision=HI)
        suf_r = jnp.dot(grc, lower, preferred_element_type=F32, precision=HI)
        tot_r = jnp.sum(grc, axis=1, keepdims=True)
        for d in range(2):
            hs = slice(d * H, (d + 1) * H)
            gc = (pre if d == 0 else suf)[:, hs]
            lb = logbeta_c[c * C:(c + 1) * C, hs]
            col_ref[0, d, c * C:(c + 1) * C, :] = jnp.concatenate(
                [gc, gc + lb, jnp.exp(gc), jnp.exp(tot[:, hs] - gc), zc], axis=1)
            gr = (pre_r if d == 0 else suf_r)[hs, :]
            br = beta_r[hs, c * C:(c + 1) * C]
            row_ref[0, d, c] = jnp.concatenate(
                [pack(gr, False), pack(br, True), pack(br * jnp.exp(gr), True),
                 jnp.broadcast_to(jnp.exp(tot_r[hs, :]), (H, LANE))], axis=0)


GDN_ROWS = 3 * GDN_HK + GDN_HV


def gdn_gates(ab, abt, a_log, dt_bias, *, tl=512):
    B, L, _ = ab.shape
    N = L // GDN_CHUNK
    H2 = 2 * GDN_HV
    return pl.pallas_call(
        functools.partial(_gdn_gates_kernel, tl=tl),
        grid=(B, L // tl),
        in_specs=[
            pl.BlockSpec((1, tl, LANE), lambda b, i: (b, i, 0)),
            pl.BlockSpec((1, LANE, tl), lambda b, i: (b, 0, i)),
            _resident((1, H2)), _resident((1, H2)), _resident((H2, 1)), _resident((H2, 1)),
        ],
        out_specs=[
            pl.BlockSpec((1, 2, tl, LANE), lambda b, i: (b, 0, i, 0)),
            pl.BlockSpec((1, 2, tl // GDN_CHUNK, GDN_ROWS, LANE), lambda b, i: (b, 0, i, 0, 0)),
        ],
        out_shape=[jax.ShapeDtypeStruct((B, 2, L, LANE), F32),
                   jax.ShapeDtypeStruct((B, 2, N, GDN_ROWS, LANE), F32)],
        compiler_params=_cparams(("parallel", "parallel"), 32),
        name="gdn_gates",
    )(ab, abt, _pair_major(a_log.reshape(1, H2)), _pair_major(dt_bias.reshape(1, H2)),
      _pair_major(a_log.reshape(H2)).reshape(H2, 1), _pair_major(dt_bias.reshape(H2)).reshape(H2, 1))


def _gdn_chunk_kernel(qf_ref, kf_ref, vf_ref, colf_ref, rowf_ref, qb_ref, kb_ref, vb_ref, colb_ref, rowb_ref,
                      of_ref, ob_ref, s_scr):
    C = GDN_CHUNK
    H = GDN_HV

    @pl.when(pl.program_id(1) == 0)
    def _():
        s_scr[...] = jnp.zeros_like(s_scr)

    P = GDN_HK
    W = 2 * GDN_DV
    items = [(d, p) for d in range(2) for p in range(P)]
    q_refs, k_refs, v_refs, o_refs = (qf_ref, qb_ref), (kf_ref, kb_ref), (vf_ref, vb_ref), (of_ref, ob_ref)
    cols = (colf_ref[0, 0], colb_ref[0, 0])
    rows = (rowf_ref[0, 0, 0], rowb_ref[0, 0, 0])
    i = lax.broadcasted_iota(jnp.int32, (C, LANE), 0)
    lane = lax.broadcasted_iota(jnp.int32, (C, LANE), 1)
    diff = i - (lane & (C - 1))
    incls = (diff >= 0, diff <= 0)
    stricts = (diff > 0, diff < 0)
    left = lane < C
    eye2 = (diff == 0).astype(F32)
    zb = jnp.zeros((C, LANE), BF16)
    zs = jnp.zeros((GDN_DK, GDN_DV), BF16)

    def col_pair(base, d, p):
        col = cols[d]
        return jnp.where(left, col[:, base + p:base + p + 1], col[:, base + P + p:base + P + p + 1])

    def col_wide(base, d, p):
        col = cols[d]
        return jnp.concatenate(
            [jnp.broadcast_to(col[:, base + e * P + p:base + e * P + p + 1], (C, GDN_DV)) for e in range(2)], axis=1)

    def block_diag(a, b, z):
        return jnp.concatenate([jnp.concatenate([a, z], axis=1), jnp.concatenate([z, b], axis=1)], axis=0)

    def anti_diag(a, b, z):
        return jnp.concatenate([jnp.concatenate([z, a], axis=1), jnp.concatenate([b, z], axis=1)], axis=0)

    ks = [k_refs[d][0, :, p * GDN_DK:(p + 1) * GDN_DK] for d, p in items]
    qs = [q_refs[d][0, :, p * GDN_DK:(p + 1) * GDN_DK] for d, p in items]
    grams = [_dot_nt(jnp.concatenate([k, q], axis=0), jnp.concatenate([k, k], axis=0)) for k, q in zip(ks, qs)]
    ms, intras = [], []
    for x, (d, p) in enumerate(items):
        gc_j = rows[d][p:p + 1, :]
        decay = jnp.exp(jnp.where(incls[d], col_pair(0, d, p) - gc_j, NEG))
        a_coef = jnp.exp(jnp.where(stricts[d], col_pair(H, d, p) - gc_j, NEG))
        ms.append(-(grams[x][:C] * a_coef))
        intras.append((grams[x][C:] * decay).astype(BF16))
    tops = [jnp.where(left, m, eye2) for m in ms]
    bots = [jnp.where(left, eye2, m) for m in ms]
    for _ in range(6):
        for x in range(len(items)):
            m_hi, m_lo = _split_bf16(ms[x])
            top_hi, top_lo = _split_bf16(tops[x])
            bot_hi, bot_lo = _split_bf16(bots[x])
            r2 = _bdot(jnp.concatenate([m_hi, m_lo], axis=0), block_diag(top_hi, bot_hi, zb))
            r = r2[:C] + r2[C:] + _bdot(m_hi, block_diag(top_lo, bot_lo, zb))
            r0, r1 = r[:, :LANE], r[:, LANE:]
            tops[x] = jnp.where(left, r0, tops[x] + r0)
            bots[x] = jnp.where(left, bots[x] + r1, r1)
            ms[x] = jnp.where(left, r0, r1)
    us, ws = [], []
    for x, (d, p) in enumerate(items):
        t = jnp.where(left, bots[x], tops[x])
        v0 = v_refs[d][0, :, 2 * p * GDN_DV:(2 * p + 1) * GDN_DV]
        v1 = v_refs[d][0, :, (2 * p + 1) * GDN_DV:(2 * p + 2) * GDN_DV]
        us.append(_bdot((t * rows[d][P + p:P + p + 1, :]).astype(BF16), anti_diag(v1, v0, zb)))
        ws.append(_bdot((t * rows[d][2 * P + p:2 * P + p + 1, :]).astype(BF16), anti_diag(ks[x], ks[x], zb)))
    for x, (d, p) in enumerate(items):
        s = s_scr[d, p]
        sb = s.astype(BF16)
        lhs = jnp.concatenate([ws[x].astype(BF16), jnp.concatenate([qs[x], qs[x]], axis=1)], axis=0)
        ws_qs = _bdot(lhs, block_diag(sb[:, :GDN_DV], sb[:, GDN_DV:], zs))
        v_new = us[x] - ws_qs[:C]
        vb = v_new.astype(BF16)
        o_refs[d][0, :, p * W:(p + 1) * W] = (col_wide(2 * H, d, p) * ws_qs[C:] + _bdot(
            intras[x], block_diag(vb[:, :GDN_DV], vb[:, GDN_DV:], zb))).astype(BF16)
        row = rows[d]
        g_end = jnp.concatenate([row[3 * P + p:3 * P + p + 1, :], row[4 * P + p:4 * P + p + 1, :]], axis=1)
        s_scr[d, p] = s * g_end + _dot_tn(ks[x], (v_new * col_wide(3 * H, d, p)).astype(BF16))


def gdn_chunk_scan(qkv, col, row):
    B, L, _ = qkv.shape
    C = GDN_CHUNK
    N = L // C
    nq = GDN_HK * GDN_DK
    hd = GDN_HV * GDN_DV

    def specs(d):
        def cidx(n):
            return n + d * (N - 1 - 2 * n)

        ins = [pl.BlockSpec((1, C, nq), lambda b, n: (b, cidx(n), 0)),
               pl.BlockSpec((1, C, nq), lambda b, n: (b, cidx(n), 1)),
               pl.BlockSpec((1, C, hd), lambda b, n: (b, cidx(n), 1)),
               pl.BlockSpec((1, 1, C, LANE), lambda b, n: (b, d, cidx(n), 0)),
               pl.BlockSpec((1, 1, 1, GDN_ROWS, LANE), lambda b, n: (b, d, cidx(n), 0, 0))]
        return ins, pl.BlockSpec((1, C, hd), lambda b, n: (b, cidx(n), 0))

    (in_f, out_f), (in_b, out_b) = specs(0), specs(1)
    return pl.pallas_call(
        _gdn_chunk_kernel,
        grid=(B, N),
        in_specs=in_f + in_b,
        out_specs=[out_f, out_b],
        out_shape=[jax.ShapeDtypeStruct((B, L, hd), BF16)] * 2,
        scratch_shapes=[pltpu.VMEM((2, GDN_HK, GDN_DK, 2 * GDN_DV), F32)],
        compiler_params=_cparams(("parallel", "arbitrary"), 32),
        name="gdn_chunk_scan",
    )(qkv, qkv, qkv, col, row, qkv, qkv, qkv, col, row)


def _gdn_out_kernel(of_ref, ob_ref, z_ref, nw_ref, x_ref, mod_ref, w_ref, out_ref, a_scr):
    o = of_ref[0].astype(F32) + ob_ref[0].astype(F32)
    z = z_ref[0].astype(F32)
    gate = z * jax.nn.sigmoid(z)
    for h in range(GDN_HV):
        hs = slice(h * GDN_DV, (h + 1) * GDN_DV)
        a_scr[:, hs] = (_rms(o[:, hs]) * nw_ref[...] * gate[:, hs]).astype(BF16)
    out_ref[0] = x_ref[0] + mod_ref[0, 2:3, :] * _bdot(a_scr[...], w_ref[...])


def gdn_out(o_f, o_b, proj, norm_w, x, mod, w, *, tm=256):
    B, L, _ = x.shape
    hd = GDN_HV * GDN_DV
    return pl.pallas_call(
        _gdn_out_kernel,
        grid=(B, L // tm),
        in_specs=[
            pl.BlockSpec((1, tm, hd), lambda b, i: (b, i, 0)),
            pl.BlockSpec((1, tm, hd), lambda b, i: (b, i, 0)),
            pl.BlockSpec((1, tm, hd), lambda b, i: (b, i, GDN_QKV // hd)),
            _resident((1, GDN_DV)),
            pl.BlockSpec((1, tm, D), lambda b, i: (b, i, 0)),
            pl.BlockSpec((1, 6, D), lambda b, i: (b, 0, 0)),
            _resident((hd, D)),
        ],
        out_specs=pl.BlockSpec((1, tm, D), lambda b, i: (b, i, 0)),
        out_shape=jax.ShapeDtypeStruct((B, L, D), F32),
        scratch_shapes=[pltpu.VMEM((tm, hd), BF16)],
        compiler_params=_cparams(("parallel", "parallel"), 40),
        name="gdn_out",
    )(o_f, o_b, proj, norm_w.reshape(1, GDN_DV), x, mod, w)


def gdn_layer(x, mod, nw, p):
    (w_in, conv_w, conv_b, w_ab, a_log, dt_bias, norm_w, w_out) = p
    proj, ab, abt = gdn_in_proj(x, mod, nw, w_in, w_ab)
    qkv = gdn_pre(proj, conv_w, conv_b)
    col, row = gdn_gates(ab, abt, a_log, dt_bias)
    o_f, o_b = gdn_chunk_scan(qkv, col, row)
    return gdn_out(o_f, o_b, proj, norm_w, x, mod, w_out.astype(BF16))


def _swa_kernel(q_ref, kp_ref, kc_ref, kn_ref, vp_ref, vc_ref, vn_ref, sink_ref, o_ref, s_scr, p_scr, *, nb):
    W = SWA_BLOCK
    KW = 3 * W
    nk = SWA_HKV * SWA_DH
    NS = SWA_STEP_BLOCKS
    n = pl.program_id(1)
    qi = lax.broadcasted_iota(jnp.int32, (W, KW), 0)
    kj = lax.broadcasted_iota(jnp.int32, (W, KW), 1)
    dist = jnp.abs(kj - W - qi)
    head_of_lane = lax.broadcasted_iota(jnp.int32, (KW, nk), 1) // SWA_DH
    kb_all = jnp.concatenate([kp_ref[0], kc_ref[0], kn_ref[0]], axis=0).astype(BF16)
    vb_all = jnp.concatenate([vp_ref[0], vc_ref[0], vn_ref[0]], axis=0).astype(BF16)
    group_of_row = lax.broadcasted_iota(jnp.int32, (SWA_G * W, 1), 0) // W
    head_of_out = lax.broadcasted_iota(jnp.int32, (SWA_G * W, nk), 1) // SWA_DH
    slopes, sinks = [], []
    for h in range(SWA_HKV):
        slope = jnp.zeros((SWA_G * W, 1), F32)
        sink = jnp.zeros((SWA_G * W, 1), F32)
        for g in range(SWA_G):
            hq = h * SWA_G + g
            slope = jnp.where(group_of_row == g, 2.0 ** (-8.0 * (hq + 1) / SWA_HQ) * LOG2E, slope)
            sink = jnp.where(group_of_row == g, sink_ref[0:1, hq:hq + 1] * LOG2E, sink)
        slopes.append(slope)
        sinks.append(sink)
    distms, vbds, s_alls = [], [], []
    for j in range(NS):
        nj = n * NS + j
        valid = (dist <= SWA_WINDOW) & ((kj >= W) | (nj > 0)) & ((kj < 2 * W) | (nj < nb - 1))
        distm = jnp.where(valid, dist.astype(F32), -NEG)
        distms.append(jnp.concatenate([distm] * SWA_G, axis=0))
        kb, vb = kb_all[j * W:j * W + KW], vb_all[j * W:j * W + KW]
        kbd = jnp.concatenate([jnp.where(head_of_lane == h, kb, 0) for h in range(SWA_HKV)], axis=0)
        vbds.append(jnp.concatenate([jnp.where(head_of_lane == h, vb, 0) for h in range(SWA_HKV)], axis=0))
        q = jnp.concatenate([q_ref[0, j * W:(j + 1) * W, g * nk:(g + 1) * nk] for g in range(SWA_G)], axis=0)
        s_alls.append(_dot_nt((q * (SWA_DH ** -0.5 * LOG2E)).astype(BF16), kbd))
    ms = [[None] * SWA_HKV for _ in range(NS)]
    rs = [[None] * SWA_HKV for _ in range(NS)]
    for j in range(NS):
        for h in range(SWA_HKV):
            s = s_alls[j][:, h * KW:(h + 1) * KW] - slopes[h] * distms[j]
            s_scr[j, :, h * KW:(h + 1) * KW] = s
            ms[j][h] = jnp.maximum(jnp.max(s, axis=-1, keepdims=True), sinks[h])
    for j in range(NS):
        for h in range(SWA_HKV):
            e = jnp.exp2(s_scr[j, :, h * KW:(h + 1) * KW] - ms[j][h])
            p_scr[j, :, h * KW:(h + 1) * KW] = e.astype(BF16)
            rs[j][h] = 1.0 / (jnp.sum(e, axis=-1, keepdims=True) + jnp.exp2(sinks[h] - ms[j][h]))
    for j in range(NS):
        o = _bdot(p_scr[j], vbds[j])
        r_all = jnp.broadcast_to(rs[j][0], (SWA_G * W, nk))
        for h in range(1, SWA_HKV):
            r_all = jnp.where(head_of_out == h, rs[j][h], r_all)
        o = (o * r_all).astype(BF16)
        for g in range(SWA_G):
            o_ref[0, j * W:(j + 1) * W, g * nk:(g + 1) * nk] = o[g * W:(g + 1) * W]


def swa_attention(qkv, sink):
    B, L, _ = qkv.shape
    W = SWA_BLOCK
    nb = L // W
    nq = SWA_HQ * SWA_DH
    nk = SWA_HKV * SWA_DH
    kcol, vcol = nq // nk, nq // nk + 1

    NS = SWA_STEP_BLOCKS

    def band(col):
        return [pl.BlockSpec((1, W, nk), lambda b, n: (b, jnp.maximum(n * NS - 1, 0), col)),
                pl.BlockSpec((1, NS * W, nk), lambda b, n: (b, n, col)),
                pl.BlockSpec((1, W, nk), lambda b, n: (b, jnp.minimum(n * NS + NS, nb - 1), col))]

    return pl.pallas_call(
        functools.partial(_swa_kernel, nb=nb),
        grid=(B, nb // NS),
        in_specs=[pl.BlockSpec((1, NS * W, nq), lambda b, n: (b, n, 0))] + band(kcol) + band(vcol) + [
            _resident((1, SWA_HQ))],
        out_specs=pl.BlockSpec((1, NS * W, nq), lambda b, n: (b, n, 0)),
        out_shape=jax.ShapeDtypeStruct((B, L, nq), BF16),
        scratch_shapes=[pltpu.VMEM((NS, SWA_G * W, SWA_HKV * 3 * W), F32),
                        pltpu.VMEM((NS, SWA_G * W, SWA_HKV * 3 * W), BF16)],
        compiler_params=_cparams(("parallel", "parallel"), 40),
        name="swa_attention",
    )(qkv, qkv, qkv, qkv, qkv, qkv, qkv, sink.reshape(1, SWA_HQ))


def swa_layer(x, mod, nw, p):
    (w_qkv, sink, w_out) = p
    nq = SWA_HQ * SWA_DH
    wq = w_qkv[:, :nq].reshape(D, SWA_HKV, SWA_G, SWA_DH).swapaxes(1, 2).reshape(D, nq)
    wo = w_out.reshape(SWA_HKV, SWA_G, SWA_DH, D).swapaxes(0, 1).reshape(nq, D)
    qkv = norm_mod_matmul(x, mod, nw, jnp.concatenate([wq, w_qkv[:, nq:]], axis=1).astype(BF16))
    o = swa_attention(qkv, sink)
    return matmul_gated_residual(o, x, mod, wo.astype(BF16))


def _mla_proj_kernel(x_ref, mod_ref, nw_ref, wd_ref, cos_ref, sin_ref, qnw_ref, kvnw_ref, wq_ref, wqr_ref, wk_ref,
                     wv_ref, q_ref, k_ref, v_ref):
    h = _norm_mod(x_ref[0], nw_ref[...], mod_ref, 0, 1).astype(BF16)
    dd = _bdot(h, wd_ref[...])
    cq = (_rms(dd[:, :MLA_QRANK]) * qnw_ref[...]).astype(BF16)
    ckv = (_rms(dd[:, MLA_QRANK:MLA_QRANK + MLA_KVRANK]) * kvnw_ref[...]).astype(BF16)
    cs, sn = cos_ref[...], sin_ref[...]
    base = MLA_QRANK + MLA_KVRANK
    k_rope = dd[:, base:base + MLA_HP] * cs + dd[:, base + MLA_HP:base + 2 * MLA_HP] * sn
    qa = _bdot(cq, wq_ref[...])
    qb = _bdot(cq, wqr_ref[...])
    kn = _bdot(ckv, wk_ref[...])
    scale = (MLA_NOPE + MLA_ROPE) ** -0.5 * LOG2E
    for h in range(MLA_H):
        hs = slice(h * MLA_HP, (h + 1) * MLA_HP)
        q_ref[0, :, hs] = ((qa[:, hs] * cs + qb[:, hs] * sn) * scale).astype(BF16)
        k_ref[0, :, hs] = (kn[:, hs] + k_rope).astype(BF16)
    v = _bdot(ckv, wv_ref[...])
    lane = lax.broadcasted_iota(jnp.int32, v.shape, 1)
    v_ref[0] = jnp.where(lane % MLA_HP == MLA_DV, 1.0, v).astype(BF16)


def mla_project(x, mod, nw, w_dext, cos_t, sin_t, q_norm_w, kv_norm_w, wq, wqr, wk, wv, *, tm=256):
    B, L, _ = x.shape
    hp = MLA_H * MLA_HP
    return pl.pallas_call(
        _mla_proj_kernel,
        grid=(B, L // tm),
        in_specs=[
            pl.BlockSpec((1, tm, D), lambda b, i: (b, i, 0)),
            pl.BlockSpec((1, 6, D), lambda b, i: (b, 0, 0)),
            _resident((1, D)),
            _resident(w_dext.shape),
            pl.BlockSpec((tm, MLA_HP), lambda b, i: (i, 0)),
            pl.BlockSpec((tm, MLA_HP), lambda b, i: (i, 0)),
            _resident((1, MLA_QRANK)), _resident((1, MLA_KVRANK)),
            _resident(wq.shape), _resident(wqr.shape), _resident(wk.shape), _resident(wv.shape),
        ],
        out_specs=[
            pl.BlockSpec((1, tm, hp), lambda b, i: (b, i, 0)),
            pl.BlockSpec((1, tm, hp), lambda b, i: (b, i, 0)),
            pl.BlockSpec((1, tm, hp), lambda b, i: (b, i, 0)),
        ],
        out_shape=[jax.ShapeDtypeStruct((B, L, hp), BF16)] * 3,
        compiler_params=_cparams(("parallel", "parallel"), 40),
        name="mla_project",
    )(x, mod, nw.reshape(1, D), w_dext, cos_t, sin_t, q_norm_w.reshape(1, -1), kv_norm_w.reshape(1, -1),
      wq, wqr, wk, wv)


def _mla_attn_kernel(q_ref, k_ref, v_ref, o_ref):
    hss = [slice(h * MLA_HP, (h + 1) * MLA_HP) for h in range(q_ref.shape[2] // MLA_HP)]
    ss = [_dot_nt(q_ref[0, :, hs], k_ref[0, :, hs]) for hs in hss]
    es = [jnp.exp2(s - jnp.max(s, axis=-1, keepdims=True)).astype(BF16) for s in ss]
    ovs = [_bdot(e, v_ref[0, :, hs]) for e, hs in zip(es, hss)]
    outs = [ov[:, :MLA_DV] / ov[:, MLA_DV:MLA_DV + 1] for ov in ovs]
    o_ref[0] = jnp.concatenate(outs, axis=1).astype(BF16)


def mla_attention(q, k, v, *, tq=256):
    B, L, _ = q.shape
    group = 4
    return pl.pallas_call(
        _mla_attn_kernel,
        grid=(B, MLA_H // group, L // tq),
        in_specs=[
            pl.BlockSpec((1, tq, group * MLA_HP), lambda b, h, i: (b, i, h)),
            pl.BlockSpec((1, L, group * MLA_HP), lambda b, h, i: (b, 0, h)),
            pl.BlockSpec((1, L, group * MLA_HP), lambda b, h, i: (b, 0, h)),
        ],
        out_specs=pl.BlockSpec((1, tq, group * MLA_DV), lambda b, h, i: (b, i, h)),
        out_shape=jax.ShapeDtypeStruct((B, L, MLA_H * MLA_DV), BF16),
        compiler_params=_cparams(("parallel", "parallel", "parallel"), 56),
        name="mla_attention",
    )(q, k, v)


def _rot_half_cols(w):
    half = MLA_ROPE // 2
    return jnp.concatenate([-w[..., half:], w[..., :half]], axis=-1)


def mla_layer(x, mod, nw, p):
    (w_down, q_norm_w, w_uq, kv_norm_w, w_ukv, w_out) = p
    L = x.shape[1]
    pad_r = MLA_HP - MLA_NOPE - MLA_ROPE
    base = MLA_QRANK + MLA_KVRANK
    w_rope = w_down[:, base:]
    zl = jnp.zeros((D, MLA_NOPE), F32)
    zr = jnp.zeros((D, pad_r), F32)
    w_dext = jnp.concatenate([w_down[:, :base], zl, w_rope, zr, zl, _rot_half_cols(w_rope), zr], axis=1)
    wq3 = w_uq.reshape(MLA_QRANK, MLA_H, MLA_NOPE + MLA_ROPE)
    zq = jnp.zeros((MLA_QRANK, MLA_H, pad_r), F32)
    wq = jnp.concatenate([wq3, zq], axis=-1).reshape(MLA_QRANK, -1).astype(BF16)
    wqr = jnp.concatenate([jnp.zeros((MLA_QRANK, MLA_H, MLA_NOPE), F32), _rot_half_cols(wq3[..., MLA_NOPE:]), zq],
                          axis=-1).reshape(MLA_QRANK, -1).astype(BF16)
    wkv3 = w_ukv.reshape(MLA_KVRANK, MLA_H, MLA_NOPE + MLA_DV)
    wk = jnp.concatenate([wkv3[..., :MLA_NOPE], jnp.zeros((MLA_KVRANK, MLA_H, MLA_HP - MLA_NOPE), F32)],
                         axis=-1).reshape(MLA_KVRANK, -1).astype(BF16)
    wv = jnp.concatenate([wkv3[..., MLA_NOPE:], jnp.zeros((MLA_KVRANK, MLA_H, MLA_HP - MLA_DV), F32)],
                         axis=-1).reshape(MLA_KVRANK, -1).astype(BF16)
    inv = ROPE_THETA ** (-jnp.arange(0, MLA_ROPE, 2, dtype=F32) / MLA_ROPE)
    ang = jnp.arange(L, dtype=F32)[:, None] * inv[None, :]
    cos, sin = jnp.cos(ang), jnp.sin(ang)
    cos_t = jnp.concatenate([jnp.ones((L, MLA_NOPE), F32), cos, cos, jnp.zeros((L, pad_r), F32)], axis=1)
    sin_t = jnp.concatenate([jnp.zeros((L, MLA_NOPE), F32), sin, sin, jnp.zeros((L, pad_r), F32)], axis=1)

    q, k, v = mla_project(x, mod, nw, w_dext.astype(BF16), cos_t, sin_t, q_norm_w, kv_norm_w, wq, wqr, wk, wv)
    o = mla_attention(q, k, v)
    return matmul_gated_residual(o, x, mod, w_out.astype(BF16))


def encoder_trunk(x, c, ada_w, ada_b, norm_w, hy, gdn, swa, mla, ffn_w_gu, ffn_w_down, final_norm_w):
    mods = ada_modulation(c, ada_w, ada_b)
    layers = (hyena_layer, gdn_layer, swa_layer, mla_layer)
    params = (hy, gdn, swa, mla)
    for i in range(DEPTH):
        kind, j = i % 4, i // 4
        x = layers[kind](x, mods[i], norm_w[i, 0], [p[j] for p in params[kind]])
        x = ffn_block(x, mods[i], norm_w[i, 1], ffn_w_gu[i].astype(BF16), ffn_w_down[i].astype(BF16),
                      final_norm_w if i == DEPTH - 1 else None)
    return x


def kernel(x_prompt, x_sample, c_prompt, c_sample, ada_w, ada_b, norm_w, hy_w_in, hy_conv_w, hy_conv_b, hy_filt_w1, hy_filt_b1, hy_filt_freq1, hy_filt_w2, hy_filt_b2, hy_filt_freq2, hy_filt_w3, hy_skip, hy_w_out, gdn_w_in, gdn_conv_w, gdn_conv_b, gdn_w_ab, gdn_a_log, gdn_dt_bias, gdn_norm_w, gdn_w_out, swa_w_qkv, swa_sink, swa_w_out, mla_w_down, mla_q_norm_w, mla_w_uq, mla_kv_norm_w, mla_w_ukv, mla_w_out, ffn_w_gu, ffn_w_down, final_norm_w):
    hy = (hy_w_in, hy_conv_w, hy_conv_b, hy_filt_w1, hy_filt_b1, hy_filt_freq1,
          hy_filt_w2, hy_filt_b2, hy_filt_freq2, hy_filt_w3, hy_skip, hy_w_out)
    gdn = (gdn_w_in, gdn_conv_w, gdn_conv_b, gdn_w_ab, gdn_a_log, gdn_dt_bias, gdn_norm_w, gdn_w_out)
    swa = (swa_w_qkv, swa_sink, swa_w_out)
    mla = (mla_w_down, mla_q_norm_w, mla_w_uq, mla_kv_norm_w, mla_w_ukv, mla_w_out)
    args = (ada_w, ada_b, norm_w, hy, gdn, swa, mla, ffn_w_gu, ffn_w_down, final_norm_w)
    return (encoder_trunk(x_prompt, c_prompt, *args), encoder_trunk(x_sample, c_sample, *args))
```
